```python
import jax, jax.numpy as jnp
from jax import lax
import numpy as np

D_MODEL = 2048
BATCH = 2
SEQ = 16384
DEPTH = 2
DEC_BATCH = 8
DEC_SEQ = 32
PAST_LEN = 1024

CHUNK = 64
N_MIXERS = 2
N_A = (DEPTH + 1) // 2
N_B = DEPTH // 2
D_MIXER = D_MODEL
CONV_W = 3
GMLP_CHUNK = 128
GMLP_GROUPS = 8
GMLP_GROUP_DIM = D_MIXER // GMLP_GROUPS
N_MEM = 256
XA_HEADS = 4
XA_HEAD_DIM = D_MODEL // 8
D_XQ = XA_HEADS * XA_HEAD_DIM
D_MIX = D_MIXER + D_XQ
D_FF = 5632
EPS = 1e-6

kernel_name = "hybrid_shortconv_gmlp_memxattn_convffn_step"


def rmsnorm(x, g):
    x32 = x.astype(jnp.float32)
    y = x32 * lax.rsqrt(jnp.mean(x32 * x32, axis=-1, keepdims=True) + EPS)
    return (y * g.astype(jnp.float32)).astype(x.dtype)


def layernorm(x, g, b):
    x32 = x.astype(jnp.float32)
    xc = x32 - jnp.mean(x32, axis=-1, keepdims=True)
    y = xc * lax.rsqrt(jnp.mean(xc * xc, axis=-1, keepdims=True) + EPS)
    return (y * g.astype(jnp.float32) + b.astype(jnp.float32)).astype(x.dtype)


def causal_dwconv(z, w, prev):
    T = z.shape[1]
    zp = jnp.concatenate([prev.astype(z.dtype), z], axis=1)
    y = w[0] * zp[:, :T]
    for k in range(1, CONV_W):
        y = y + w[k] * zp[:, k:k + T]
    return y, zp[:, -(CONV_W - 1):]


def short_conv_mixer(proj, conv_w, prev):
    b_gate, c_gate, h = jnp.split(proj, 3, axis=-1)
    y, buf = causal_dwconv(c_gate * h, conv_w, prev)
    return b_gate * y, buf


def chunk_mlp_mixer(proj, norm_g, norm_b, ws, bias):
    z = jax.nn.gelu(proj, approximate=False)
    u, v = jnp.split(z, 2, axis=-1)
    v = layernorm(v, norm_g, norm_b)
    B, T, _ = v.shape
    n_chunks = -(-T // GMLP_CHUNK)
    pad = n_chunks * GMLP_CHUNK - T
    vp = jnp.pad(v, ((0, 0), (0, pad), (0, 0))).reshape(
        B, n_chunks, GMLP_CHUNK, GMLP_GROUPS, GMLP_GROUP_DIM)
    mask = jnp.tril(jnp.ones((GMLP_CHUNK, GMLP_CHUNK), dtype=bool))
    wm = jnp.where(mask, ws, jnp.zeros((), ws.dtype)).astype(v.dtype)
    mixed = jnp.einsum('gts,bcsgd->bctgd', wm, vp) + jnp.swapaxes(bias, 0, 1)[:, :, None].astype(v.dtype)
    mixed = mixed.reshape(B, n_chunks * GMLP_CHUNK, D_MIXER)[:, :T]
    return u * mixed, v


def memory_kv(mem, g, wk, wv):
    B = mem.shape[0]
    mn = rmsnorm(mem, g)
    k = (mn @ wk).reshape(B, N_MEM, XA_HEADS, XA_HEAD_DIM)
    v = (mn @ wv).reshape(B, N_MEM, XA_HEADS, XA_HEAD_DIM)
    return k, v


def memory_attention(q, k, v):
    B, T, _ = q.shape
    q = q.reshape(B, T, XA_HEADS, XA_HEAD_DIM)
    s = jnp.einsum('bthd,bmhd->bhtm', q, k).astype(jnp.float32) * (XA_HEAD_DIM ** -0.5)
    p = jax.nn.softmax(s, axis=-1).astype(v.dtype)
    return jnp.einsum('bhtm,bmhd->bthd', p, v).reshape(B, T, D_XQ)


def trunk(x, mem_k, mem_v, conv_a_prev, ffn_prev, norm_mix_g, w_in_a, conv_a_w, w_in_b,
          gmlp_norm_g, gmlp_norm_b, gmlp_ws, gmlp_bias, w_out, norm_ffn_g, w_up,
          ffn_conv_w, ffn_conv_b, w_down, norm_final_g):
    conv_a_new, ffn_new, gmlp_v = [], [], []
    for i in range(DEPTH):
        kind, j = i % N_MIXERS, i // N_MIXERS
        xn = rmsnorm(x, norm_mix_g[i])
        if kind == 0:
            proj = xn @ w_in_a[j]
            y_mix, buf = short_conv_mixer(proj[..., :3 * D_MIXER], conv_a_w[j], conv_a_prev[j])
            conv_a_new.append(buf)
        else:
            proj = xn @ w_in_b[j]
            y_mix, v_rows = chunk_mlp_mixer(proj[..., :2 * D_MIXER], gmlp_norm_g[j],
                                            gmlp_norm_b[j], gmlp_ws[j], gmlp_bias[j])
            gmlp_v.append(v_rows)
        y_mem = memory_attention(proj[..., -D_XQ:], mem_k[i], mem_v[i])
        x = x + jnp.concatenate([y_mix, y_mem], axis=-1) @ w_out[i]
        xn = rmsnorm(x, norm_ffn_g[i])
        z, buf = causal_dwconv(xn @ w_up[i], ffn_conv_w[i], ffn_prev[i])
        a, g = jnp.split(z + ffn_conv_b[i], 2, axis=-1)
        x = x + (jax.nn.silu(g) * a) @ w_down[i]
        ffn_new.append(buf)
    return rmsnorm(x, norm_final_g), jnp.stack(conv_a_new), jnp.stack(ffn_new), jnp.stack(gmlp_v)


def setup_inputs(seed: int = 0) -> dict:
    key = jax.random.key(seed)
    ks = iter(jax.random.split(key, 32))
    f32 = jnp.float32

    def nrm(shape, scale=1.0):
        return jax.random.normal(next(ks), shape, f32) * scale

    def gain(shape):
        return 1.0 + 0.01 * jax.random.normal(next(ks), shape, f32)

    D = D_MODEL
    return {
        'x_prompt': nrm((BATCH, SEQ, D)),
        'x_sample': nrm((DEC_BATCH, DEC_SEQ, D)),
        'mem_prompt': nrm((BATCH, N_MEM, D)),
        'cache_conv_a': nrm((N_A, DEC_BATCH, CONV_W - 1, D_MIXER)),
        'cache_ffn_conv': nrm((DEPTH, DEC_BATCH, CONV_W - 1, 2 * D_FF)),
        'cache_mem_k': nrm((DEPTH, DEC_BATCH, N_MEM, XA_HEADS, XA_HEAD_DIM)),
        'cache_mem_v': nrm((DEPTH, DEC_BATCH, N_MEM, XA_HEADS, XA_HEAD_DIM)),
        'norm_mix_g': gain((DEPTH, D)),
        'norm_mem_g': gain((DEPTH, D)),
        'w_mem_k': nrm((DEPTH, D, D_XQ), D ** -0.5),
        'w_mem_v': nrm((DEPTH, D, D_XQ), D ** -0.5),
        'w_in_a': nrm((N_A, D, 3 * D_MIXER + D_XQ), D ** -0.5),
        'conv_a_w': nrm((N_A, CONV_W, D_MIXER), CONV_W ** -0.5),
        'w_in_b': nrm((N_B, D, 2 * D_MIXER + D_XQ), D ** -0.5),
        'gmlp_norm_g': gain((N_B, D_MIXER)),
        'gmlp_norm_b': nrm((N_B, D_MIXER), 0.01),
        'gmlp_ws': nrm((N_B, GMLP_GROUPS, GMLP_CHUNK, GMLP_CHUNK), GMLP_CHUNK ** -0.5),
        'gmlp_bias': gain((N_B, GMLP_GROUPS, GMLP_CHUNK)),
        'w_out': nrm((DEPTH, D_MIX, D), D_MIX ** -0.5),
        'norm_ffn_g': gain((DEPTH, D)),
        'w_up': nrm((DEPTH, D, 2 * D_FF), D ** -0.5),
        'ffn_conv_w': nrm((DEPTH, CONV_W, 2 * D_FF), CONV_W ** -0.5),
        'ffn_conv_b': nrm((DEPTH, 2 * D_FF), 0.01),
        'w_down': nrm((DEPTH, D_FF, D), D_FF ** -0.5),
        'norm_final_g': gain((D,)),
    }


def reference(x_prompt, x_sample, mem_prompt, cache_conv_a, cache_ffn_conv, cache_mem_k,
              cache_mem_v, norm_mix_g, norm_mem_g, w_mem_k, w_mem_v, w_in_a, conv_a_w,
              w_in_b, gmlp_norm_g, gmlp_norm_b, gmlp_ws, gmlp_bias, w_out, norm_ffn_g,
              w_up, ffn_conv_w, ffn_conv_b, w_down, norm_final_g):
    weights = (norm_mix_g, w_in_a, conv_a_w, w_in_b, gmlp_norm_g, gmlp_norm_b, gmlp_ws,
               gmlp_bias, w_out, norm_ffn_g, w_up, ffn_conv_w, ffn_conv_b, w_down, norm_final_g)

    B = x_prompt.shape[0]
    kv = [memory_kv(mem_prompt, norm_mem_g[i], w_mem_k[i], w_mem_v[i]) for i in range(DEPTH)]
    mem_k_prompt = jnp.stack([k for k, _ in kv])
    mem_v_prompt = jnp.stack([v for _, v in kv])
    conv_a_zero = jnp.zeros((N_A, B, CONV_W - 1, D_MIXER), x_prompt.dtype)
    ffn_zero = jnp.zeros((DEPTH, B, CONV_W - 1, 2 * D_FF), x_prompt.dtype)
    y_prompt, conv_a_prompt, ffn_conv_prompt, _ = trunk(
        x_prompt, mem_k_prompt, mem_v_prompt, conv_a_zero, ffn_zero, *weights)

    y_sample, conv_a_sample, ffn_conv_sample, gmlp_v_sample = trunk(
        x_sample, cache_mem_k, cache_mem_v, cache_conv_a, cache_ffn_conv, *weights)

    return (y_prompt, y_sample, conv_a_prompt, ffn_conv_prompt, mem_k_prompt, mem_v_prompt,
            conv_a_sample, ffn_conv_sample, gmlp_v_sample)
```

```python
import functools

import jax
import jax.numpy as jnp
from jax import lax
from jax.experimental import pallas as pl
from jax.experimental.pallas import tpu as pltpu

EPS = 1e-6
GMLP_CHUNK = 128
SUBLANES = 8
V7X_VMEM_BYTES = 64 * 1024 * 1024
VMEM_REQUEST_CAP = 58 * 1024 * 1024

F32 = jnp.float32
BF16 = jnp.bfloat16


def _vmem_limit(est_bytes):
    return int(min(VMEM_REQUEST_CAP, max(32 * 1024 * 1024, est_bytes * 5 // 4)))


def _rms(x, g):
    return x * lax.rsqrt(jnp.mean(x * x, axis=-1, keepdims=True) + EPS) * g


def _gelu(x):
    return 0.5 * x * (1.0 + lax.erf(x * (2.0 ** -0.5)))


def _conv3(u, h0, h1, w):
    s, c = u.shape
    r1 = pltpu.roll(u, 1, axis=0)
    r2 = pltpu.roll(u, 2, axis=0)
    row = lax.broadcasted_iota(jnp.int32, (SUBLANES, c), 0)
    top1 = jnp.where(row == 0, h1, r1[:SUBLANES])
    top2 = jnp.where(row == 0, h0, jnp.where(row == 1, h1, r2[:SUBLANES]))
    if s > SUBLANES:
        r1 = jnp.concatenate([top1, r1[SUBLANES:]], axis=0)
        r2 = jnp.concatenate([top2, r2[SUBLANES:]], axis=0)
    else:
        r1, r2 = top1, top2
    return w[0:1] * r2 + w[1:2] * r1 + w[2:3] * u


def _conv_segments(u, w, prev_ref, carry_ref, buf_ref, slot, first, seg):
    nseg = u.shape[0] // seg

    def load_prev():
        for s in range(nseg):
            carry_ref[slot, s] = prev_ref[s]

    if first is True:
        load_prev()
    else:
        pl.when(first)(load_prev)
    outs = []
    for s in range(nseg):
        us = u[s * seg:(s + 1) * seg]
        h0 = carry_ref[slot, s, SUBLANES - 2:SUBLANES - 1, :]
        h1 = carry_ref[slot, s, SUBLANES - 1:SUBLANES, :]
        outs.append(_conv3(us, h0, h1, w))
        tail = us[seg - SUBLANES:seg]
        carry_ref[slot, s] = tail
        buf_ref[s, slot] = tail
    return outs[0] if nseg == 1 else jnp.concatenate(outs, axis=0)


def _pad_history(prev):
    return jnp.pad(prev, ((0, 0), (SUBLANES - prev.shape[1], 0), (0, 0)))


def _unpack_history(buf, rows):
    nb, slots, _, c = buf.shape
    return buf[:, :, SUBLANES - rows:, :].swapaxes(1, 2).reshape(nb, rows, slots * c)


def _attention(q, k_ref, v_ref, seg, head_dim):
    tm, width = q.shape
    nseg = tm // seg
    scale = head_dim ** -0.5
    rows = []
    for s in range(nseg):
        heads = []
        for h in range(width // head_dim):
            cs = slice(h * head_dim, (h + 1) * head_dim)
            qh = q[s * seg:(s + 1) * seg, cs].astype(BF16)
            kh = k_ref[s, :, cs].astype(BF16)
            vh = v_ref[s, :, cs].astype(BF16)
            sc = lax.dot_general(qh, kh, (((1,), (1,)), ((), ())),
                                 preferred_element_type=F32) * scale
            m = jnp.max(sc, axis=-1, keepdims=True)
            p = jnp.exp(sc - m)
            l = jnp.sum(p, axis=-1, keepdims=True)
            o = jnp.dot(p.astype(BF16), vh, preferred_element_type=F32)
            heads.append(o / l)
        rows.append(heads[0] if len(heads) == 1 else jnp.concatenate(heads, axis=1))
    return rows[0] if nseg == 1 else jnp.concatenate(rows, axis=0)


def _init_tile(x_ref, g_ref, o_ref, xn_ref):
    x = x_ref[...]
    xn_ref[...] = _rms(x, g_ref[...]).astype(BF16)
    o_ref[...] = x


def _mix_conv_kernel(x_ref, g_ref, wb_ref, wc_ref, wh_ref, wq_ref, cw_ref, prev_ref, k_ref, v_ref,
                     wom_ref, woh_ref, o_ref, buf_ref, xn_ref, carry_ref, *, seg, tiles_per_batch, nc,
                     head_dim):
    i = pl.program_id(0)
    j = pl.program_id(1)

    @pl.when(j == 0)
    def _():
        _init_tile(x_ref, g_ref, o_ref, xn_ref)

    @pl.when(j < nc)
    def _():
        xn = xn_ref[...]
        b = jnp.dot(xn, wb_ref[...], preferred_element_type=F32)
        c = jnp.dot(xn, wc_ref[...], preferred_element_type=F32)
        h = jnp.dot(xn, wh_ref[...], preferred_element_type=F32)
        first = True if tiles_per_batch == 1 else (i % tiles_per_batch) == 0
        y = b * _conv_segments(c * h, cw_ref[...], prev_ref, carry_ref, buf_ref, j, first, seg)
        o_ref[...] += jnp.dot(y.astype(BF16), wom_ref[...], preferred_element_type=F32)

    @pl.when(j >= nc)
    def _():
        q = jnp.dot(xn_ref[...], wq_ref[...], preferred_element_type=F32)
        y = _attention(q, k_ref, v_ref, seg, head_dim)
        o_ref[...] += jnp.dot(y.astype(BF16), woh_ref[...], preferred_element_type=F32)


def _mix_conv(x, g, w_in, conv_w, prev, mem_k, mem_v, w_out, *, tm, seg, batch_len, cw, cwh, head_dim):
    m, d = x.shape
    dm = conv_w.shape[1]
    dq = mem_k.shape[2]
    nseg = tm // seg
    tpb = batch_len // tm if nseg == 1 else 1
    nc, nh = dm // cw, dq // cwh
    nb = prev.shape[0]
    mc = lambda j: jnp.minimum(j, nc - 1)
    hc = lambda j: jnp.maximum(j - nc, 0)
    n_mem = mem_k.shape[1]
    in_specs = [
        pl.BlockSpec((tm, d), lambda i, j: (i, 0)),
        pl.BlockSpec((1, d), lambda i, j: (0, 0)),
        pl.BlockSpec((d, cw), lambda i, j: (0, mc(j))),
        pl.BlockSpec((d, cw), lambda i, j: (0, nc + mc(j))),
        pl.BlockSpec((d, cw), lambda i, j: (0, 2 * nc + mc(j))),
        pl.BlockSpec((d, cwh), lambda i, j: (0, 3 * dm // cwh + hc(j))),
        pl.BlockSpec((3, cw), lambda i, j: (0, mc(j))),
        pl.BlockSpec((nseg, SUBLANES, cw), lambda i, j: (i // tpb, 0, mc(j))),
        pl.BlockSpec((nseg, n_mem, cwh), lambda i, j: (i // tpb, 0, hc(j))),
        pl.BlockSpec((nseg, n_mem, cwh), lambda i, j: (i // tpb, 0, hc(j))),
        pl.BlockSpec((cw, d), lambda i, j: (mc(j), 0)),
        pl.BlockSpec((cwh, d), lambda i, j: (dm // cwh + hc(j), 0)),
    ]
    out_specs = [
        pl.BlockSpec((tm, d), lambda i, j: (i, 0)),
        pl.BlockSpec((nseg, nc, SUBLANES, cw), lambda i, j: (i // tpb, 0, 0, 0)),
    ]
    scratch = [pltpu.VMEM((tm, d), BF16), pltpu.VMEM((nc, nseg, SUBLANES, cw), F32)]
    est = (4 * tm * d * 4 + tm * d * 2 + 2 * (3 * d * cw + d * cwh + (cw + cwh) * d) * 2
           + 4 * nseg * n_mem * cwh * 4 + 8 * tm * cw * 4)
    x_out, buf = pl.pallas_call(
        functools.partial(_mix_conv_kernel, seg=seg, tiles_per_batch=tpb, nc=nc, head_dim=head_dim),
        grid=(m // tm, nc + nh),
        in_specs=in_specs,
        out_specs=out_specs,
        out_shape=[jax.ShapeDtypeStruct((m, d), F32),
                   jax.ShapeDtypeStruct((nb, nc, SUBLANES, cw), F32)],
        scratch_shapes=scratch,
        compiler_params=pltpu.CompilerParams(
            dimension_semantics=("arbitrary", "arbitrary"), vmem_limit_bytes=_vmem_limit(est)),
        name="mix_conv",
    )(x, g, w_in, w_in, w_in, w_in, conv_w, _pad_history(prev), mem_k, mem_v, w_out, w_out)
    return x_out, _unpack_history(buf, prev.shape[1])


def _mix_gmlp_kernel(x_ref, g_ref, wv_ref, wu_ref, wq_ref, lng_ref, lnb_ref, ws_ref, bias_ref, k_ref,
                     v_ref, wom_ref, woh_ref, o_ref, *rest, seg, rblk, nc, head_dim, group_dim,
                     emit_v):
    if emit_v:
        vout_ref, xn_ref, vs_ref = rest
    else:
        vout_ref = None
        xn_ref, vs_ref = rest
    j = pl.program_id(1)
    tm = x_ref.shape[0]
    cw = wv_ref.shape[1]

    @pl.when(j == 0)
    def _():
        _init_tile(x_ref, g_ref, o_ref, xn_ref)

    @pl.when(j < nc)
    def _():
        vs_ref[j] = _gelu(jnp.dot(xn_ref[...], wv_ref[...], preferred_element_type=F32))

    @pl.when(j == nc)
    def _():
        width = nc * cw
        tot = vs_ref[0].sum(axis=-1, keepdims=True)
        for c in range(1, nc):
            tot += vs_ref[c].sum(axis=-1, keepdims=True)
        mean = tot / width
        sq = None
        for c in range(nc):
            dv = vs_ref[c] - mean
            part = (dv * dv).sum(axis=-1, keepdims=True)
            sq = part if sq is None else sq + part
        rstd = lax.rsqrt(sq / width + EPS)
        for c in range(nc):
            cs = slice(c * cw, (c + 1) * cw)
            vn = (vs_ref[c] - mean) * rstd * lng_ref[:, cs] + lnb_ref[:, cs]
            vs_ref[c] = vn
            if emit_v:
                vout_ref[:, cs] = vn

    @pl.when(jnp.logical_and(j >= nc, j < 2 * nc))
    def _():
        u = _gelu(jnp.dot(xn_ref[...], wu_ref[...], preferred_element_type=F32))
        vn = vs_ref[j - nc].astype(BF16)
        tri = (lax.broadcasted_iota(jnp.int32, (rblk, rblk), 0)
               >= lax.broadcasted_iota(jnp.int32, (rblk, rblk), 1))
        cols = []
        for gi in range(cw // group_dim):
            wmat = jnp.where(tri, ws_ref[gi, :rblk, :rblk], 0.0).astype(BF16)
            bcol = bias_ref[gi, :rblk, :]
            blocks = []
            for r in range(tm // rblk):
                vb = vn[r * rblk:(r + 1) * rblk, gi * group_dim:(gi + 1) * group_dim]
                blocks.append(jnp.dot(wmat, vb, preferred_element_type=F32) + bcol)
            cols.append(blocks[0] if len(blocks) == 1 else jnp.concatenate(blocks, axis=0))
        mixed = cols[0] if len(cols) == 1 else jnp.concatenate(cols, axis=1)
        y = u * mixed
        o_ref[...] += jnp.dot(y.astype(BF16), wom_ref[...], preferred_element_type=F32)

    @pl.when(j >= 2 * nc)
    def _():
        q = jnp.dot(xn_ref[...], wq_ref[...], preferred_element_type=F32)
        y = _attention(q, k_ref, v_ref, seg, head_dim)
        o_ref[...] += jnp.dot(y.astype(BF16), woh_ref[...], preferred_element_type=F32)


def _mix_gmlp(x, g, w_in, ln_g, ln_b, ws, bias, mem_k, mem_v, w_out, *, tm, seg, batch_len, cw, cwh,
              head_dim, emit_v):
    m, d = x.shape
    dm = ln_g.shape[1]
    dq = mem_k.shape[2]
    groups, chunk = ws.shape[0], ws.shape[1]
    group_dim = dm // groups
    nseg = tm // seg
    tpb = batch_len // tm if nseg == 1 else 1
    rblk = min(chunk, seg)
    assert seg % rblk == 0 and cw % group_dim == 0
    nc, nh = dm // cw, dq // cwh
    gpc = cw // group_dim
    n_mem = mem_k.shape[1]
    uc = lambda j: jnp.clip(j - nc, 0, nc - 1)
    hc = lambda j: jnp.maximum(j - 2 * nc, 0)
    in_specs = [
        pl.BlockSpec((tm, d), lambda i, j: (i, 0)),
        pl.BlockSpec((1, d), lambda i, j: (0, 0)),
        pl.BlockSpec((d, cw), lambda i, j: (0, nc + jnp.minimum(j, nc - 1))),
        pl.BlockSpec((d, cw), lambda i, j: (0, uc(j))),
        pl.BlockSpec((d, cwh), lambda i, j: (0, 2 * dm // cwh + hc(j))),
        pl.BlockSpec((1, dm), lambda i, j: (0, 0)),
        pl.BlockSpec((1, dm), lambda i, j: (0, 0)),
        pl.BlockSpec((gpc, chunk, chunk), lambda i, j: (uc(j), 0, 0)),
        pl.BlockSpec((gpc, chunk, 1), lambda i, j: (uc(j), 0, 0)),
        pl.BlockSpec((nseg, n_mem, cwh), lambda i, j: (i // tpb, 0, hc(j))),
        pl.BlockSpec((nseg, n_mem, cwh), lambda i, j: (i // tpb, 0, hc(j))),
        pl.BlockSpec((cw, d), lambda i, j: (uc(j), 0)),
        pl.BlockSpec((cwh, d), lambda i, j: (dm // cwh + hc(j), 0)),
    ]
    out_specs = [pl.BlockSpec((tm, d), lambda i, j: (i, 0))]
    out_shape = [jax.ShapeDtypeStruct((m, d), F32)]
    if emit_v:
        out_specs.append(pl.BlockSpec((tm, dm), lambda i, j: (i, 0)))
        out_shape.append(jax.ShapeDtypeStruct((m, dm), F32))
    est = (4 * tm * d * 4 + tm * d * 2 + tm * dm * 4 * (3 if emit_v else 1)
           + 2 * (2 * d * cw + d * cwh + (cw + cwh) * d) * 2
           + 4 * nseg * n_mem * cwh * 4 + 8 * tm * cw * 4)
    outs = pl.pallas_call(
        functools.partial(_mix_gmlp_kernel, seg=seg, rblk=rblk, nc=nc, head_dim=head_dim,
                          group_dim=group_dim, emit_v=emit_v),
        grid=(m // tm, 2 * nc + nh),
        in_specs=in_specs,
        out_specs=out_specs,
        out_shape=out_shape,
        scratch_shapes=[pltpu.VMEM((tm, d), BF16), pltpu.VMEM((nc, tm, cw), F32)],
        compiler_params=pltpu.CompilerParams(
            dimension_semantics=("arbitrary", "arbitrary"), vmem_limit_bytes=_vmem_limit(est)),
        name="mix_gmlp",
    )(x, g, w_in, w_in, w_in, ln_g, ln_b, ws, bias, mem_k, mem_v, w_out, w_out)
    return (outs[0], outs[1]) if emit_v else (outs[0], None)


def _ffn_kernel(x_ref, g_ref, wup_ref, cw_ref, cb_ref, prev_ref, wdn_ref, gf_ref, o_ref, buf_ref,
                xn_ref, carry_ref, *, seg, tiles_per_batch, final_norm):
    i = pl.program_id(0)
    j = pl.program_id(1)
    fc = wdn_ref.shape[0]

    @pl.when(j == 0)
    def _():
        _init_tile(x_ref, g_ref, o_ref, xn_ref)

    up = jnp.dot(xn_ref[...], wup_ref[...], preferred_element_type=F32)
    first = True if tiles_per_batch == 1 else (i % tiles_per_batch) == 0
    z = _conv_segments(up, cw_ref[...], prev_ref, carry_ref, buf_ref, j, first, seg) + cb_ref[...]
    a = z[:, :fc]
    gate = z[:, fc:]
    h = jax.nn.silu(gate) * a
    o_ref[...] += jnp.dot(h.astype(BF16), wdn_ref[...], preferred_element_type=F32)

    if final_norm:
        @pl.when(j == pl.num_programs(1) - 1)
        def _():
            o_ref[...] = _rms(o_ref[...], gf_ref[...])


def _ffn(x, g, w_up, conv_w, conv_b, prev, w_down, g_final, *, tm, seg, batch_len, fc, final_norm):
    m, d = x.shape
    dff = w_down.shape[0]
    nj = dff // fc
    nseg = tm // seg
    tpb = batch_len // tm if nseg == 1 else 1
    nb = prev.shape[0]
    in_specs = [
        pl.BlockSpec((tm, d), lambda i, j: (i, 0)),
        pl.BlockSpec((1, d), lambda i, j: (0, 0)),
        pl.BlockSpec((d, 2 * fc), lambda i, j: (0, j)),
        pl.BlockSpec((3, 2 * fc), lambda i, j: (0, j)),
        pl.BlockSpec((1, 2 * fc), lambda i, j: (0, j)),
        pl.BlockSpec((nseg, SUBLANES, 2 * fc), lambda i, j: (i // tpb, 0, j)),
        pl.BlockSpec((fc, d), lambda i, j: (j, 0)),
        pl.BlockSpec((1, d), lambda i, j: (0, 0)),
    ]
    out_specs = [
        pl.BlockSpec((tm, d), lambda i, j: (i, 0)),
        pl.BlockSpec((nseg, nj, SUBLANES, 2 * fc), lambda i, j: (i // tpb, 0, 0, 0)),
    ]
    scratch = [pltpu.VMEM((tm, d), BF16), pltpu.VMEM((nj, nseg, SUBLANES, 2 * fc), F32)]
    est = 4 * tm * d * 4 + tm * d * 2 + 2 * (3 * d * fc) * 2 + 8 * tm * 2 * fc * 4
    x_out, buf = pl.pallas_call(
        functools.partial(_ffn_kernel, seg=seg, tiles_per_batch=tpb, final_norm=final_norm),
        grid=(m // tm, nj),
        in_specs=in_specs,
        out_specs=out_specs,
        out_shape=[jax.ShapeDtypeStruct((m, d), F32),
                   jax.ShapeDtypeStruct((nb, nj, SUBLANES, 2 * fc), F32)],
        scratch_shapes=scratch,
        compiler_params=pltpu.CompilerParams(
            dimension_semantics=("arbitrary", "arbitrary"), vmem_limit_bytes=_vmem_limit(est)),
        name="conv_ffn",
    )(x, g, w_up, conv_w, conv_b, _pad_history(prev), w_down, g_final)
    return x_out, _unpack_history(buf, prev.shape[1])


def _memkv_kernel(m_ref, g_ref, wk_ref, wv_ref, ko_ref, vo_ref):
    mn = _rms(m_ref[...], g_ref[0]).astype(BF16)
    ko_ref[0] = jnp.dot(mn, wk_ref[0], preferred_element_type=F32)
    vo_ref[0] = jnp.dot(mn, wv_ref[0], preferred_element_type=F32)


def _memkv(mem, g, wk, wv, *, tn):
    rows, d = mem.shape
    depth, _, dq = wk.shape
    shape = jax.ShapeDtypeStruct((depth, rows, dq), F32)
    return pl.pallas_call(
        _memkv_kernel,
        grid=(depth, dq // tn),
        in_specs=[
            pl.BlockSpec((rows, d), lambda l, n: (0, 0)),
            pl.BlockSpec((1, 1, d), lambda l, n: (l, 0, 0)),
            pl.BlockSpec((1, d, tn), lambda l, n: (l, 0, n)),
            pl.BlockSpec((1, d, tn), lambda l, n: (l, 0, n)),
        ],
        out_specs=[pl.BlockSpec((1, rows, tn), lambda l, n: (l, 0, n))] * 2,
        out_shape=[shape, shape],
        compiler_params=pltpu.CompilerParams(dimension_semantics=("arbitrary", "arbitrary")),
        name="memory_kv",
    )(mem, g, wk, wv)


def _group_cols(t, nj, fc):
    lead = t.shape[:-1]
    return t.reshape(lead + (2, nj, fc)).swapaxes(-3, -2).reshape(lead + (2 * nj * fc,))


def _ungroup_cols(t, nj, fc):
    lead = t.shape[:-1]
    return t.reshape(lead + (nj, 2, fc)).swapaxes(-3, -2).reshape(lead + (2 * nj * fc,))


def _trunk(x3, mem_k, mem_v, conv_a_prev, ffn_prev, p, *, tm, emit_v):
    nb, t, d = x3.shape
    seg = min(t, tm)
    x = x3.reshape(nb * t, d)
    depth = p["norm_mix_g"].shape[0]
    conv_a_new, ffn_new, gmlp_v = [], [], []
    common = dict(tm=tm, seg=seg, batch_len=t)
    for i in range(depth):
        kind, jx = i % 2, i // 2
        if kind == 0:
            x, buf = _mix_conv(x, p["norm_mix_g"][i][None], p["w_in_a"][jx], p["conv_a_w"][jx],
                               conv_a_prev[jx], mem_k[i], mem_v[i], p["w_out"][i],
                               cw=p["cw"], cwh=p["cwh"], head_dim=p["head_dim"], **common)
            conv_a_new.append(buf)
        else:
            x, v_rows = _mix_gmlp(x, p["norm_mix_g"][i][None], p["w_in_b"][jx],
                                  p["gmlp_norm_g"][jx][None], p["gmlp_norm_b"][jx][None],
                                  p["gmlp_ws"][jx], p["gmlp_bias"][jx][..., None],
                                  mem_k[i], mem_v[i], p["w_out"][i], cw=p["cw"], cwh=p["cwh"],
                                  head_dim=p["head_dim"], emit_v=emit_v, **common)
            gmlp_v.append(v_rows)
        x, buf = _ffn(x, p["norm_ffn_g"][i][None], p["w_up"][i], p["ffn_conv_w"][i],
                      p["ffn_conv_b"][i][None], ffn_prev[i], p["w_down"][i],
                      p["norm_final_g"][None], fc=p["fc"], final_norm=(i == depth - 1), **common)
        ffn_new.append(_ungroup_cols(buf, p["nj"], p["fc"]))
    y = x.reshape(nb, t, d)
    v_out = jnp.stack([v.reshape(nb, t, -1) for v in gmlp_v]) if emit_v else None
    return y, jnp.stack(conv_a_new), jnp.stack(ffn_new), v_out


def kernel(x_prompt, x_sample, mem_prompt, cache_conv_a, cache_ffn_conv, cache_mem_k, cache_mem_v,
           norm_mix_g, norm_mem_g, w_mem_k, w_mem_v, w_in_a, conv_a_w, w_in_b, gmlp_norm_g,
           gmlp_norm_b, gmlp_ws, gmlp_bias, w_out, norm_ffn_g, w_up, ffn_conv_w, ffn_conv_b, w_down,
           norm_final_g):
    b, s, d = x_prompt.shape
    nb_s, t_s, _ = x_sample.shape
    depth = norm_mix_g.shape[0]
    n_mem = mem_prompt.shape[1]
    heads, head_dim = cache_mem_k.shape[-2:]
    dq = heads * head_dim
    dff = w_down.shape[1]
    fc = 512
    assert dff % fc == 0
    nj = dff // fc

    p = dict(
        norm_mix_g=norm_mix_g, norm_ffn_g=norm_ffn_g, norm_final_g=norm_final_g,
        w_in_a=w_in_a.astype(BF16), w_in_b=w_in_b.astype(BF16), w_out=w_out.astype(BF16),
        conv_a_w=conv_a_w, gmlp_norm_g=gmlp_norm_g, gmlp_norm_b=gmlp_norm_b, gmlp_ws=gmlp_ws,
        gmlp_bias=gmlp_bias,
        w_up=_group_cols(w_up, nj, fc).astype(BF16), w_down=w_down.astype(BF16),
        ffn_conv_w=_group_cols(ffn_conv_w, nj, fc), ffn_conv_b=_group_cols(ffn_conv_b, nj, fc),
        fc=fc, nj=nj, cw=512, cwh=512, head_dim=head_dim,
    )

    mk, mv = _memkv(mem_prompt.reshape(b * n_mem, d), norm_mem_g[:, None, :],
                    w_mem_k.astype(BF16), w_mem_v.astype(BF16), tn=512)
    mk = mk.reshape(depth, b, n_mem, dq)
    mv = mv.reshape(depth, b, n_mem, dq)
    conv_a_zero = jnp.zeros((cache_conv_a.shape[0], b) + cache_conv_a.shape[2:], F32)
    ffn_zero = jnp.zeros((depth, b) + cache_ffn_conv.shape[2:], F32)
    y_prompt, conv_a_prompt, ffn_conv_prompt, _ = _trunk(
        x_prompt, mk, mv, conv_a_zero, ffn_zero, p, tm=512, emit_v=False)

    y_sample, conv_a_sample, ffn_conv_sample, gmlp_v_sample = _trunk(
        x_sample, cache_mem_k.reshape(depth, nb_s, n_mem, dq),
        cache_mem_v.reshape(depth, nb_s, n_mem, dq), cache_conv_a,
        _group_cols(cache_ffn_conv, nj, fc), p, tm=nb_s * t_s, emit_v=True)

    return (y_prompt, y_sample, conv_a_prompt, ffn_conv_prompt,
            mk.reshape(depth, b, n_mem, heads, head_dim), mv.reshape(depth, b, n_mem, heads, head_dim),
            conv_a_sample, ffn_conv_sample, gmlp_v_sample)
```

```python
import functools

import jax
import jax.numpy as jnp
from jax import lax
from jax.experimental import pallas as pl
from jax.experimental.pallas import tpu as pltpu

EPS = 1e-6
SUBLANES = 8
VMEM_REQUEST_CAP = 58 * 1024 * 1024

F32 = jnp.float32
BF16 = jnp.bfloat16


def _vmem_limit(est_bytes):
    return int(min(VMEM_REQUEST_CAP, max(32 * 1024 * 1024, est_bytes * 5 // 4)))


def _compiler_params(est_bytes):
    return pltpu.CompilerParams(dimension_semantics=("arbitrary", "arbitrary"),
                                vmem_limit_bytes=_vmem_limit(est_bytes))


def _dot(a, b):
    return jnp.dot(a, b, preferred_element_type=F32)


def _rms(x, g):
    return x * lax.rsqrt(jnp.mean(x * x, axis=-1, keepdims=True) + EPS) * g


def _gelu(x):
    return 0.5 * x * (1.0 + lax.erf(x * (2.0 ** -0.5)))


def _cat(parts, axis):
    return parts[0] if len(parts) == 1 else jnp.concatenate(parts, axis=axis)


def _segment_spans(row0, nrows, seg):
    spans, pos = [], row0
    while pos < row0 + nrows:
        s, off = divmod(pos, seg)
        n = min(seg - off, row0 + nrows - pos)
        spans.append((s, off, n))
        pos += n
    return spans


def _conv3(u, h0, h1, w):
    s, c = u.shape
    r1 = pltpu.roll(u, 1, axis=0)
    r2 = pltpu.roll(u, 2, axis=0)
    row = lax.broadcasted_iota(jnp.int32, (SUBLANES, c), 0)
    top1 = jnp.where(row == 0, h1, r1[:SUBLANES])
    top2 = jnp.where(row == 0, h0, jnp.where(row == 1, h1, r2[:SUBLANES]))
    if s > SUBLANES:
        r1 = jnp.concatenate([top1, r1[SUBLANES:]], axis=0)
        r2 = jnp.concatenate([top2, r2[SUBLANES:]], axis=0)
    else:
        r1, r2 = top1, top2
    return w[0:1] * r2 + w[1:2] * r1 + w[2:3] * u


class _RowConv:
    def __init__(self, w, prev_ref, carry_ref, buf_ref, slot, first, seg):
        self.w, self.carry_ref, self.buf_ref, self.slot, self.seg = w, carry_ref, buf_ref, slot, seg
        self.tail = None

        def load_prev():
            for s in range(prev_ref.shape[0]):
                carry_ref[slot, s] = prev_ref[s]

        if first is True:
            load_prev()
        else:
            pl.when(first)(load_prev)

    def block(self, u, row0):
        outs, pos = [], 0
        for s, off, n in _segment_spans(row0, u.shape[0], self.seg):
            us = u[pos:pos + n]
            hist = self.carry_ref[self.slot, s] if off == 0 else self.tail
            outs.append(_conv3(us, hist[SUBLANES - 2:SUBLANES - 1], hist[SUBLANES - 1:], self.w))
            self.tail = us[n - SUBLANES:]
            if off + n == self.seg:
                self.carry_ref[self.slot, s] = self.tail
                self.buf_ref[s, self.slot] = self.tail
            pos += n
        return _cat(outs, 0)


def _pad_history(prev):
    pad = [(0, 0)] * prev.ndim
    pad[-2] = (SUBLANES - prev.shape[-2], 0)
    return jnp.pad(prev, pad)


def _unpack_history(buf, rows):
    nb, slots, _, c = buf.shape
    return buf[:, :, SUBLANES - rows:, :].swapaxes(1, 2).reshape(nb, rows, slots * c)


def _attention(q, k_ref, v_ref, row0, seg, head_dim):
    width = q.shape[1]
    scale = head_dim ** -0.5
    spans = _segment_spans(row0, q.shape[0], seg)
    head_cols = [slice(h * head_dim, (h + 1) * head_dim) for h in range(width // head_dim)]
    scores, pos = [], 0
    for s, _, n in spans:
        for cs in head_cols:
            qh = q[pos:pos + n, cs].astype(BF16)
            kh = k_ref[s, :, cs].astype(BF16)
            scores.append(lax.dot_general(qh, kh, (((1,), (1,)), ((), ())),
                                          preferred_element_type=F32) * scale)
        pos += n
    rows = []
    for si, (s, _, n) in enumerate(spans):
        heads = []
        for hi, cs in enumerate(head_cols):
            sc = scores[si * len(head_cols) + hi]
            m = jnp.max(sc, axis=-1, keepdims=True)
            p = jnp.exp(sc - m)
            l = jnp.sum(p, axis=-1, keepdims=True)
            heads.append(_dot(p.astype(BF16), v_ref[s, :, cs].astype(BF16)) / l)
        rows.append(_cat(heads, 1))
    return _cat(rows, 0)


def _pipelined(starts, lead, finish):
    cur = lead(starts[0])
    for k, r0 in enumerate(starts):
        nxt = lead(starts[k + 1]) if k + 1 < len(starts) else None
        finish(r0, cur)
        cur = nxt


def _init_tile(x_ref, g_ref, o_ref, xn_ref, rb):
    for r0 in range(0, x_ref.shape[0], rb):
        x = x_ref[r0:r0 + rb, :]
        xn_ref[r0:r0 + rb, :] = _rms(x, g_ref[...]).astype(BF16)
        o_ref[r0:r0 + rb, :] = x


def _attention_step(xn_ref, wq_ref, k_ref, v_ref, woh_ref, o_ref, rb, seg, head_dim):
    def lead(r0):
        return _dot(xn_ref[r0:r0 + rb, :], wq_ref[...])

    def finish(r0, q):
        y = _attention(q, k_ref, v_ref, r0, seg, head_dim)
        o_ref[r0:r0 + rb, :] += _dot(y.astype(BF16), woh_ref[...])

    _pipelined(range(0, xn_ref.shape[0], rb), lead, finish)


def _mix_conv_kernel(x_ref, g_ref, wb_ref, wc_ref, wh_ref, wq_ref, cw_ref, prev_ref, k_ref, v_ref,
                     wom_ref, woh_ref, o_ref, buf_ref, xn_ref, carry_ref, *, seg, rb,
                     tiles_per_batch, nc, head_dim):
    i = pl.program_id(0)
    j = pl.program_id(1)
    tm = x_ref.shape[0]

    @pl.when(j == 0)
    def _():
        _init_tile(x_ref, g_ref, o_ref, xn_ref, rb)

    @pl.when(j < nc)
    def _():
        first = True if tiles_per_batch == 1 else (i % tiles_per_batch) == 0
        conv = _RowConv(cw_ref[...], prev_ref, carry_ref, buf_ref, j, first, seg)

        def lead(r0):
            xn = xn_ref[r0:r0 + rb, :]
            return _dot(xn, wc_ref[...]), _dot(xn, wh_ref[...]), _dot(xn, wb_ref[...])

        def finish(r0, chb):
            c, h, b = chb
            y = b * conv.block(c * h, r0)
            o_ref[r0:r0 + rb, :] += _dot(y.astype(BF16), wom_ref[...])

        _pipelined(range(0, tm, rb), lead, finish)

    @pl.when(j >= nc)
    def _():
        _attention_step(xn_ref, wq_ref, k_ref, v_ref, woh_ref, o_ref, rb, seg, head_dim)


def _mix_conv(x, g, w_in, conv_w, prev, mem_k, mem_v, w_out, *, layer, slot, tm, rb, seg, batch_len,
              cw, cwh, head_dim):
    m, d = x.shape
    dm = conv_w.shape[-1]
    nb, n_mem, dq = mem_k.shape[1:]
    nseg = tm // seg
    tpb = batch_len // tm if nseg == 1 else 1
    nc, nh = dm // cw, dq // cwh
    mc = lambda j: jnp.minimum(j, nc - 1)
    hc = lambda j: jnp.maximum(j - nc, 0)
    in_specs = [
        pl.BlockSpec((tm, d), lambda i, j: (i, 0)),
        pl.BlockSpec((None, 1, d), lambda i, j: (layer, 0, 0)),
        pl.BlockSpec((None, d, cw), lambda i, j: (slot, 0, mc(j))),
        pl.BlockSpec((None, d, cw), lambda i, j: (slot, 0, nc + mc(j))),
        pl.BlockSpec((None, d, cw), lambda i, j: (slot, 0, 2 * nc + mc(j))),
        pl.BlockSpec((None, d, cwh), lambda i, j: (slot, 0, 3 * dm // cwh + hc(j))),
        pl.BlockSpec((None, 3, cw), lambda i, j: (slot, 0, mc(j))),
        pl.BlockSpec((None, nseg, SUBLANES, cw), lambda i, j: (slot, i // tpb, 0, mc(j))),
        pl.BlockSpec((None, nseg, n_mem, cwh), lambda i, j: (layer, i // tpb, 0, hc(j))),
        pl.BlockSpec((None, nseg, n_mem, cwh), lambda i, j: (layer, i // tpb, 0, hc(j))),
        pl.BlockSpec((None, cw, d), lambda i, j: (layer, mc(j), 0)),
        pl.BlockSpec((None, cwh, d), lambda i, j: (layer, dm // cwh + hc(j), 0)),
    ]
    out_specs = [
        pl.BlockSpec((tm, d), lambda i, j: (i, 0)),
        pl.BlockSpec((nseg, nc, SUBLANES, cw), lambda i, j: (i // tpb, 0, 0, 0)),
    ]
    scratch = [pltpu.VMEM((tm, d), BF16), pltpu.VMEM((nc, nseg, SUBLANES, cw), F32)]
    est = (4 * tm * d * 4 + tm * d * 2 + 2 * (3 * d * cw + d * cwh + (cw + cwh) * d) * 2
           + 4 * nseg * n_mem * cwh * 4 + 8 * rb * cw * 4)
    x_out, buf = pl.pallas_call(
        functools.partial(_mix_conv_kernel, seg=seg, rb=rb, tiles_per_batch=tpb, nc=nc,
                          head_dim=head_dim),
        grid=(m // tm, nc + nh),
        in_specs=in_specs,
        out_specs=out_specs,
        out_shape=[jax.ShapeDtypeStruct((m, d), F32),
                   jax.ShapeDtypeStruct((nb, nc, SUBLANES, cw), F32)],
        scratch_shapes=scratch,
        compiler_params=_compiler_params(est),
        name="mix_conv",
    )(x, g, w_in, w_in, w_in, w_in, conv_w, _pad_history(prev), mem_k, mem_v, w_out, w_out)
    return x_out, _unpack_history(buf, prev.shape[-2])


def _mix_gmlp_kernel(x_ref, g_ref, wv_ref, wu_ref, wq_ref, lng_ref, lnb_ref, ws_ref, bias_ref, k_ref,
                     v_ref, wom_ref, woh_ref, o_ref, *rest, seg, rb, rblk, nc, head_dim, group_dim,
                     emit_v):
    if emit_v:
        vout_ref, xn_ref, vs_ref = rest
    else:
        vout_ref = None
        xn_ref, vs_ref = rest
    j = pl.program_id(1)
    tm = x_ref.shape[0]
    cw = wv_ref.shape[1]

    @pl.when(j == 0)
    def _():
        _init_tile(x_ref, g_ref, o_ref, xn_ref, rb)

    @pl.when(j < nc)
    def _():
        for r0 in range(0, tm, rb):
            vs_ref[j, r0:r0 + rb, :] = _gelu(_dot(xn_ref[r0:r0 + rb, :], wv_ref[...]))

    @pl.when(j == nc)
    def _():
        width = nc * cw
        for r0 in range(0, tm, rb):
            rows = slice(r0, r0 + rb)
            tot = vs_ref[0, rows, :].sum(axis=-1, keepdims=True)
            for c in range(1, nc):
                tot += vs_ref[c, rows, :].sum(axis=-1, keepdims=True)
            mean = tot / width
            sq = None
            for c in range(nc):
                dv = vs_ref[c, rows, :] - mean
                part = (dv * dv).sum(axis=-1, keepdims=True)
                sq = part if sq is None else sq + part
            rstd = lax.rsqrt(sq / width + EPS)
            for c in range(nc):
                cs = slice(c * cw, (c + 1) * cw)
                vn = (vs_ref[c, rows, :] - mean) * rstd * lng_ref[:, cs] + lnb_ref[:, cs]
                vs_ref[c, rows, :] = vn
                if emit_v:
                    vout_ref[rows, cs] = vn

    @pl.when(jnp.logical_and(j >= nc, j < 2 * nc))
    def _():
        tri = (lax.broadcasted_iota(jnp.int32, (rblk, rblk), 0)
               >= lax.broadcasted_iota(jnp.int32, (rblk, rblk), 1))
        wmats = [jnp.where(tri, ws_ref[gi, :rblk, :rblk], 0.0).astype(BF16)
                 for gi in range(cw // group_dim)]

        def lead(r0):
            u = _dot(xn_ref[r0:r0 + rb, :], wu_ref[...])
            vn = vs_ref[j - nc, r0:r0 + rb, :].astype(BF16)
            cols = []
            for gi, wmat in enumerate(wmats):
                blocks = [_dot(wmat, vn[r:r + rblk, gi * group_dim:(gi + 1) * group_dim])
                          for r in range(0, rb, rblk)]
                cols.append(_cat(blocks, 0))
            return u, _cat(cols, 1)

        def finish(r0, um):
            u, mixed = um
            bias = _cat([_cat([jnp.broadcast_to(bias_ref[gi, :rblk, :], (rblk, group_dim))]
                              * (rb // rblk), 0) for gi in range(len(wmats))], 1)
            y = _gelu(u) * (mixed + bias)
            o_ref[r0:r0 + rb, :] += _dot(y.astype(BF16), wom_ref[...])

        _pipelined(range(0, tm, rb), lead, finish)

    @pl.when(j >= 2 * nc)
    def _():
        _attention_step(xn_ref, wq_ref, k_ref, v_ref, woh_ref, o_ref, rb, seg, head_dim)


def _mix_gmlp(x, g, w_in, ln_g, ln_b, ws, bias, mem_k, mem_v, w_out, *, layer, slot, tm, rb, seg,
              batch_len, cw, cwh, head_dim, emit_v):
    m, d = x.shape
    dm = ln_g.shape[-1]
    nb, n_mem, dq = mem_k.shape[1:]
    groups, chunk = ws.shape[1:3]
    group_dim = dm // groups
    nseg = tm // seg
    tpb = batch_len // tm if nseg == 1 else 1
    rblk = min(chunk, seg)
    assert seg % rblk == 0 and rb % rblk == 0 and cw % group_dim == 0
    nc, nh = dm // cw, dq // cwh
    gpc = cw // group_dim
    uc = lambda j: jnp.clip(j - nc, 0, nc - 1)
    hc = lambda j: jnp.maximum(j - 2 * nc, 0)
    in_specs = [
        pl.BlockSpec((tm, d), lambda i, j: (i, 0)),
        pl.BlockSpec((None, 1, d), lambda i, j: (layer, 0, 0)),
        pl.BlockSpec((None, d, cw), lambda i, j: (slot, 0, nc + jnp.minimum(j, nc - 1))),
        pl.BlockSpec((None, d, cw), lambda i, j: (slot, 0, uc(j))),
        pl.BlockSpec((None, d, cwh), lambda i, j: (slot, 0, 2 * dm // cwh + hc(j))),
        pl.BlockSpec((None, 1, dm), lambda i, j: (slot, 0, 0)),
        pl.BlockSpec((None, 1, dm), lambda i, j: (slot, 0, 0)),
        pl.BlockSpec((None, gpc, chunk, chunk), lambda i, j: (slot, uc(j), 0, 0)),
        pl.BlockSpec((None, gpc, chunk, 1), lambda i, j: (slot, uc(j), 0, 0)),
        pl.BlockSpec((None, nseg, n_mem, cwh), lambda i, j: (layer, i // tpb, 0, hc(j))),
        pl.BlockSpec((None, nseg, n_mem, cwh), lambda i, j: (layer, i // tpb, 0, hc(j))),
        pl.BlockSpec((None, cw, d), lambda i, j: (layer, uc(j), 0)),
        pl.BlockSpec((None, cwh, d), lambda i, j: (layer, dm // cwh + hc(j), 0)),
    ]
    out_specs = [pl.BlockSpec((tm, d), lambda i, j: (i, 0))]
    out_shape = [jax.ShapeDtypeStruct((m, d), F32)]
    if emit_v:
        out_specs.append(pl.BlockSpec((tm, dm), lambda i, j: (i, 0)))
        out_shape.append(jax.ShapeDtypeStruct((m, dm), F32))
    est = (4 * tm * d * 4 + tm * d * 2 + tm * dm * 4 * (3 if emit_v else 1)
           + 2 * (2 * d * cw + d * cwh + (cw + cwh) * d) * 2
           + 4 * nseg * n_mem * cwh * 4 + 8 * rb * cw * 4)
    outs = pl.pallas_call(
        functools.partial(_mix_gmlp_kernel, seg=seg, rb=rb, rblk=rblk, nc=nc, head_dim=head_dim,
                          group_dim=group_dim, emit_v=emit_v),
        grid=(m // tm, 2 * nc + nh),
        in_specs=in_specs,
        out_specs=out_specs,
        out_shape=out_shape,
        scratch_shapes=[pltpu.VMEM((tm, d), BF16), pltpu.VMEM((nc, tm, cw), F32)],
        compiler_params=_compiler_params(est),
        name="mix_gmlp",
    )(x, g, w_in, w_in, w_in, ln_g, ln_b, ws, bias, mem_k, mem_v, w_out, w_out)
    return (outs[0], outs[1]) if emit_v else (outs[0], None)


def _ffn_kernel(x_ref, g_ref, wa_ref, wg_ref, cw_ref, cb_ref, prev_ref, wdn_ref, gf_ref, o_ref,
                buf_ref, xn_ref, carry_ref, *, seg, rb, tiles_per_batch, final_norm):
    i = pl.program_id(0)
    j = pl.program_id(1)
    tm = x_ref.shape[0]
    fc = wdn_ref.shape[0]

    @pl.when(j == 0)
    def _():
        _init_tile(x_ref, g_ref, o_ref, xn_ref, rb)

    first = True if tiles_per_batch == 1 else (i % tiles_per_batch) == 0
    conv = _RowConv(cw_ref[...], prev_ref, carry_ref, buf_ref, j, first, seg)

    def lead(r0):
        xn = xn_ref[r0:r0 + rb, :]
        return jnp.concatenate([_dot(xn, wa_ref[...]), _dot(xn, wg_ref[...])], axis=1)

    def finish(r0, up):
        z = conv.block(up, r0) + cb_ref[...]
        h = jax.nn.silu(z[:, fc:]) * z[:, :fc]
        o_ref[r0:r0 + rb, :] += _dot(h.astype(BF16), wdn_ref[...])

    _pipelined(range(0, tm, rb), lead, finish)

    if final_norm:
        @pl.when(j == pl.num_programs(1) - 1)
        def _():
            for r0 in range(0, tm, rb):
                o_ref[r0:r0 + rb, :] = _rms(o_ref[r0:r0 + rb, :], gf_ref[...])


def _ffn(x, g, w_up, conv_w, conv_b, prev, w_down, g_final, *, layer, tm, rb, seg, batch_len, fc,
         final_norm):
    m, d = x.shape
    dff = w_down.shape[1]
    nj = dff // fc
    nseg = tm // seg
    tpb = batch_len // tm if nseg == 1 else 1
    nb = prev.shape[1]
    in_specs = [
        pl.BlockSpec((tm, d), lambda i, j: (i, 0)),
        pl.BlockSpec((None, 1, d), lambda i, j: (layer, 0, 0)),
        pl.BlockSpec((None, d, fc), lambda i, j: (layer, 0, j)),
        pl.BlockSpec((None, d, fc), lambda i, j: (layer, 0, nj + j)),
        pl.BlockSpec((None, 3, 2 * fc), lambda i, j: (layer, 0, j)),
        pl.BlockSpec((None, 1, 2 * fc), lambda i, j: (layer, 0, j)),
        pl.BlockSpec((None, nseg, SUBLANES, 2 * fc), lambda i, j: (layer, i // tpb, 0, j)),
        pl.BlockSpec((None, fc, d), lambda i, j: (layer, j, 0)),
        pl.BlockSpec((1, d), lambda i, j: (0, 0)),
    ]
    out_specs = [
        pl.BlockSpec((tm, d), lambda i, j: (i, 0)),
        pl.BlockSpec((nseg, nj, SUBLANES, 2 * fc), lambda i, j: (i // tpb, 0, 0, 0)),
    ]
    scratch = [pltpu.VMEM((tm, d), BF16), pltpu.VMEM((nj, nseg, SUBLANES, 2 * fc), F32)]
    est = 4 * tm * d * 4 + tm * d * 2 + 2 * (3 * d * fc) * 2 + 8 * rb * 2 * fc * 4
    x_out, buf = pl.pallas_call(
        functools.partial(_ffn_kernel, seg=seg, rb=rb, tiles_per_batch=tpb, final_norm=final_norm),
        grid=(m // tm, nj),
        in_specs=in_specs,
        out_specs=out_specs,
        out_shape=[jax.ShapeDtypeStruct((m, d), F32),
                   jax.ShapeDtypeStruct((nb, nj, SUBLANES, 2 * fc), F32)],
        scratch_shapes=scratch,
        compiler_params=_compiler_params(est),
        name="conv_ffn",
    )(x, g, w_up, w_up, conv_w, conv_b, _pad_history(prev), w_down, g_final)
    return x_out, _unpack_history(buf, prev.shape[-2])


def _memkv_kernel(m_ref, g_ref, wk_ref, wv_ref, ko_ref, vo_ref):
    mn = _rms(m_ref[...], g_ref[0]).astype(BF16)
    ko_ref[0] = _dot(mn, wk_ref[0])
    vo_ref[0] = _dot(mn, wv_ref[0])


def _memkv(mem, g, wk, wv, *, tn):
    rows, d = mem.shape
    depth, _, dq = wk.shape
    shape = jax.ShapeDtypeStruct((depth, rows, dq), F32)
    return pl.pallas_call(
        _memkv_kernel,
        grid=(depth, dq // tn),
        in_specs=[
            pl.BlockSpec((rows, d), lambda l, n: (0, 0)),
            pl.BlockSpec((1, 1, d), lambda l, n: (l, 0, 0)),
            pl.BlockSpec((1, d, tn), lambda l, n: (l, 0, n)),
            pl.BlockSpec((1, d, tn), lambda l, n: (l, 0, n)),
        ],
        out_specs=[pl.BlockSpec((1, rows, tn), lambda l, n: (l, 0, n))] * 2,
        out_shape=[shape, shape],
        compiler_params=pltpu.CompilerParams(dimension_semantics=("arbitrary", "arbitrary")),
        name="memory_kv",
    )(mem, g, wk, wv)


def _group_cols(t, nj, fc):
    lead = t.shape[:-1]
    return t.reshape(lead + (2, nj, fc)).swapaxes(-3, -2).reshape(lead + (2 * nj * fc,))


def _ungroup_cols(t, nj, fc):
    lead = t.shape[:-1]
    return t.reshape(lead + (nj, 2, fc)).swapaxes(-3, -2).reshape(lead + (2 * nj * fc,))


def _trunk(x3, mem_k, mem_v, conv_a_prev, ffn_prev, p, *, tm, rb, emit_v):
    nb, t, d = x3.shape
    x = x3.reshape(nb * t, d)
    depth = p["norm_mix_g"].shape[0]
    conv_a_new, ffn_new, gmlp_v = [], [], []
    tiling = dict(tm=tm, rb=rb, seg=min(t, tm), batch_len=t)
    mixer = dict(cw=p["cw"], cwh=p["cwh"], head_dim=p["head_dim"], **tiling)
    for i in range(depth):
        kind, jx = i % 2, i // 2
        if kind == 0:
            x, buf = _mix_conv(x, p["norm_mix_g"], p["w_in_a"], p["conv_a_w"], conv_a_prev, mem_k,
                               mem_v, p["w_out"], layer=i, slot=jx, **mixer)
            conv_a_new.append(buf)
        else:
            x, v_rows = _mix_gmlp(x, p["norm_mix_g"], p["w_in_b"], p["gmlp_norm_g"], p["gmlp_norm_b"],
                                  p["gmlp_ws"], p["gmlp_bias"], mem_k, mem_v, p["w_out"], layer=i,
                                  slot=jx, emit_v=emit_v, **mixer)
            gmlp_v.append(v_rows)
        x, buf = _ffn(x, p["norm_ffn_g"], p["w_up"], p["ffn_conv_w"], p["ffn_conv_b"], ffn_prev,
                      p["w_down"], p["norm_final_g"], layer=i, fc=p["fc"],
                      final_norm=(i == depth - 1), **tiling)
        ffn_new.append(_ungroup_cols(buf, p["nj"], p["fc"]))
    y = x.reshape(nb, t, d)
    v_out = jnp.stack([v.reshape(nb, t, -1) for v in gmlp_v]) if emit_v else None
    return y, jnp.stack(conv_a_new), jnp.stack(ffn_new), v_out


def kernel(x_prompt, x_sample, mem_prompt, cache_conv_a, cache_ffn_conv, cache_mem_k, cache_mem_v,
           norm_mix_g, norm_mem_g, w_mem_k, w_mem_v, w_in_a, conv_a_w, w_in_b, gmlp_norm_g,
           gmlp_norm_b, gmlp_ws, gmlp_bias, w_out, norm_ffn_g, w_up, ffn_conv_w, ffn_conv_b, w_down,
           norm_final_g):
    b, s, d = x_prompt.shape
    nb_s, t_s, _ = x_sample.shape
    depth = norm_mix_g.shape[0]
    n_mem = mem_prompt.shape[1]
    heads, head_dim = cache_mem_k.shape[-2:]
    dq = heads * head_dim
    dff = w_down.shape[1]
    fc = 512
    assert dff % fc == 0
    nj = dff // fc

    p = dict(
        norm_mix_g=norm_mix_g[:, None, :], norm_ffn_g=norm_ffn_g[:, None, :],
        norm_final_g=norm_final_g[None],
        w_in_a=w_in_a.astype(BF16), w_in_b=w_in_b.astype(BF16), w_out=w_out.astype(BF16),
        conv_a_w=conv_a_w, gmlp_norm_g=gmlp_norm_g[:, None, :], gmlp_norm_b=gmlp_norm_b[:, None, :],
        gmlp_ws=gmlp_ws, gmlp_bias=gmlp_bias[..., None],
        w_up=w_up.astype(BF16), w_down=w_down.astype(BF16),
        ffn_conv_w=_group_cols(ffn_conv_w, nj, fc),
        ffn_conv_b=_group_cols(ffn_conv_b, nj, fc)[:, None, :],
        fc=fc, nj=nj, cw=512, cwh=512, head_dim=head_dim,
    )

    mk, mv = _memkv(mem_prompt.reshape(b * n_mem, d), norm_mem_g[:, None, :],
                    w_mem_k.astype(BF16), w_mem_v.astype(BF16), tn=512)
    mk = mk.reshape(depth, b, n_mem, dq)
    mv = mv.reshape(depth, b, n_mem, dq)
    conv_a_zero = jnp.zeros((cache_conv_a.shape[0], b) + cache_conv_a.shape[2:], F32)
    ffn_zero = jnp.zeros((depth, b) + cache_ffn_conv.shape[2:], F32)
    y_prompt, conv_a_prompt, ffn_conv_prompt, _ = _trunk(
        x_prompt, mk, mv, conv_a_zero, ffn_zero, p, tm=512, rb=256, emit_v=False)

    y_sample, conv_a_sample, ffn_conv_sample, gmlp_v_sample = _trunk(
        x_sample, cache_mem_k.reshape(depth, nb_s, n_mem, dq),
        cache_mem_v.reshape(depth, nb_s, n_mem, dq), cache_conv_a,
        _group_cols(cache_ffn_conv, nj, fc), p, tm=nb_s * t_s, rb=nb_s * t_s, emit_v=True)

    return (y_prompt, y_sample, conv_a_prompt, ffn_conv_prompt,
            mk.reshape(depth, b, n_mem, heads, head_dim), mv.reshape(depth, b, n_mem, heads, head_dim),
            conv_a_sample, ffn_conv_sample, gmlp_v_sample)
```

```python
import functools

import jax
import jax.numpy as jnp
from jax import lax
from jax.experimental import pallas as pl
from jax.experimental.pallas import tpu as pltpu

EPS = 1e-6
SUBLANES = 8
VMEM_REQUEST_CAP = 58 * 1024 * 1024

F32 = jnp.float32
BF16 = jnp.bfloat16


def _vmem_limit(est_bytes):
    return int(min(VMEM_REQUEST_CAP, max(32 * 1024 * 1024, est_bytes * 5 // 4)))


def _compiler_params(est_bytes):
    return pltpu.CompilerParams(dimension_semantics=("arbitrary", "arbitrary"),
                                vmem_limit_bytes=_vmem_limit(est_bytes))


def _dot(a, b):
    return jnp.dot(a, b, preferred_element_type=F32)


def _rms(x, g):
    return x * lax.rsqrt(jnp.mean(x * x, axis=-1, keepdims=True) + EPS) * g


def _gelu(x):
    return 0.5 * x * (1.0 + lax.erf(x * (2.0 ** -0.5)))


def _cat(parts, axis):
    return parts[0] if len(parts) == 1 else jnp.concatenate(parts, axis=axis)


def _segment_spans(row0, nrows, seg):
    spans, pos = [], row0
    while pos < row0 + nrows:
        s, off = divmod(pos, seg)
        n = min(seg - off, row0 + nrows - pos)
        spans.append((s, off, n))
        pos += n
    return spans


def _conv3(u, h0, h1, w):
    s, c = u.shape
    r1 = pltpu.roll(u, 1, axis=0)
    r2 = pltpu.roll(u, 2, axis=0)
    row = lax.broadcasted_iota(jnp.int32, (SUBLANES, c), 0)
    top1 = jnp.where(row == 0, h1, r1[:SUBLANES])
    top2 = jnp.where(row == 0, h0, jnp.where(row == 1, h1, r2[:SUBLANES]))
    if s > SUBLANES:
        r1 = jnp.concatenate([top1, r1[SUBLANES:]], axis=0)
        r2 = jnp.concatenate([top2, r2[SUBLANES:]], axis=0)
    else:
        r1, r2 = top1, top2
    return w[0:1] * r2 + w[1:2] * r1 + w[2:3] * u


class _RowConv:
    def __init__(self, w, prev_ref, carry_ref, buf_ref, slot, first, seg):
        self.w, self.carry_ref, self.buf_ref, self.slot, self.seg = w, carry_ref, buf_ref, slot, seg
        self.tail = None

        def load_prev():
            for s in range(prev_ref.shape[0]):
                carry_ref[slot, s] = prev_ref[s]

        if first is True:
            load_prev()
        else:
            pl.when(first)(load_prev)

    def block(self, u, row0):
        outs, pos = [], 0
        for s, off, n in _segment_spans(row0, u.shape[0], self.seg):
            us = u[pos:pos + n]
            hist = self.carry_ref[self.slot, s] if off == 0 else self.tail
            outs.append(_conv3(us, hist[SUBLANES - 2:SUBLANES - 1], hist[SUBLANES - 1:], self.w))
            self.tail = us[n - SUBLANES:]
            if off + n == self.seg:
                self.carry_ref[self.slot, s] = self.tail
                self.buf_ref[s, self.slot] = self.tail
            pos += n
        return _cat(outs, 0)


def _pad_history(prev):
    pad = [(0, 0)] * prev.ndim
    pad[-2] = (SUBLANES - prev.shape[-2], 0)
    return jnp.pad(prev, pad)


def _unpack_history(buf, rows):
    nb, slots, _, c = buf.shape
    return buf[:, :, SUBLANES - rows:, :].swapaxes(1, 2).reshape(nb, rows, slots * c)


def _attention(q, k_ref, v_ref, row0, seg, head_dim):
    width = q.shape[1]
    scale = head_dim ** -0.5
    spans = _segment_spans(row0, q.shape[0], seg)
    head_cols = [slice(h * head_dim, (h + 1) * head_dim) for h in range(width // head_dim)]
    scores, pos = [], 0
    for s, _, n in spans:
        for cs in head_cols:
            qh = q[pos:pos + n, cs].astype(BF16)
            kh = k_ref[s, :, cs].astype(BF16)
            scores.append(lax.dot_general(qh, kh, (((1,), (1,)), ((), ())),
                                          preferred_element_type=F32) * scale)
        pos += n
    rows = []
    for si, (s, _, n) in enumerate(spans):
        heads = []
        for hi, cs in enumerate(head_cols):
            sc = scores[si * len(head_cols) + hi]
            m = jnp.max(sc, axis=-1, keepdims=True)
            p = jnp.exp(sc - m)
            l = jnp.sum(p, axis=-1, keepdims=True)
            heads.append(_dot(p.astype(BF16), v_ref[s, :, cs].astype(BF16)) / l)
        rows.append(_cat(heads, 1))
    return _cat(rows, 0)


def _pipelined(starts, lead, finish):
    cur = lead(starts[0])
    for k, r0 in enumerate(starts):
        nxt = lead(starts[k + 1]) if k + 1 < len(starts) else None
        finish(r0, cur)
        cur = nxt


def _init_tile(x_ref, g_ref, o_ref, xn_ref, rb):
    for r0 in range(0, x_ref.shape[0], rb):
        x = x_ref[r0:r0 + rb, :]
        xn_ref[r0:r0 + rb, :] = _rms(x, g_ref[...]).astype(BF16)
        o_ref[r0:r0 + rb, :] = x


def _attention_step(xn_ref, wq_ref, k_ref, v_ref, woh_ref, o_ref, rb, seg, head_dim):
    def lead(r0):
        return _dot(xn_ref[r0:r0 + rb, :], wq_ref[...])

    def finish(r0, q):
        y = _attention(q, k_ref, v_ref, r0, seg, head_dim)
        o_ref[r0:r0 + rb, :] += _dot(y.astype(BF16), woh_ref[...])

    _pipelined(range(0, xn_ref.shape[0], rb), lead, finish)


def _mix_conv_kernel(x_ref, g_ref, wb_ref, wc_ref, wh_ref, wq_ref, cw_ref, prev_ref, k_ref, v_ref,
                     wom_ref, woh_ref, o_ref, buf_ref, xn_ref, carry_ref, *, seg, rb,
                     tiles_per_batch, nc, head_dim):
    i = pl.program_id(0)
    j = pl.program_id(1)
    tm = x_ref.shape[0]

    @pl.when(j == 0)
    def _():
        _init_tile(x_ref, g_ref, o_ref, xn_ref, rb)

    @pl.when(j < nc)
    def _():
        first = True if tiles_per_batch == 1 else (i % tiles_per_batch) == 0
        conv = _RowConv(cw_ref[...], prev_ref, carry_ref, buf_ref, j, first, seg)

        def lead(r0):
            xn = xn_ref[r0:r0 + rb, :]
            return _dot(xn, wc_ref[...]), _dot(xn, wh_ref[...]), _dot(xn, wb_ref[...])

        def finish(r0, chb):
            c, h, b = chb
            y = b * conv.block(c * h, r0)
            o_ref[r0:r0 + rb, :] += _dot(y.astype(BF16), wom_ref[...])

        _pipelined(range(0, tm, rb), lead, finish)

    @pl.when(j >= nc)
    def _():
        _attention_step(xn_ref, wq_ref, k_ref, v_ref, woh_ref, o_ref, rb, seg, head_dim)


def _mix_conv(x, g, w_in, conv_w, prev, mem_k, mem_v, w_out, *, layer, slot, tm, rb, seg, batch_len,
              cw, cwh, head_dim):
    m, d = x.shape
    dm = conv_w.shape[-1]
    nb, n_mem, dq = mem_k.shape[1:]
    nseg = tm // seg
    tpb = batch_len // tm if nseg == 1 else 1
    nc, nh = dm // cw, dq // cwh
    mc = lambda j: jnp.minimum(j, nc - 1)
    hc = lambda j: jnp.maximum(j - nc, 0)
    in_specs = [
        pl.BlockSpec((tm, d), lambda i, j: (i, 0)),
        pl.BlockSpec((None, 1, d), lambda i, j: (layer, 0, 0)),
        pl.BlockSpec((None, d, cw), lambda i, j: (slot, 0, mc(j))),
        pl.BlockSpec((None, d, cw), lambda i, j: (slot, 0, nc + mc(j))),
        pl.BlockSpec((None, d, cw), lambda i, j: (slot, 0, 2 * nc + mc(j))),
        pl.BlockSpec((None, d, cwh), lambda i, j: (slot, 0, 3 * dm // cwh + hc(j))),
        pl.BlockSpec((None, 3, cw), lambda i, j: (slot, 0, mc(j))),
        pl.BlockSpec((None, nseg, SUBLANES, cw), lambda i, j: (slot, i // tpb, 0, mc(j))),
        pl.BlockSpec((None, nseg, n_mem, cwh), lambda i, j: (layer, i // tpb, 0, hc(j))),
        pl.BlockSpec((None, nseg, n_mem, cwh), lambda i, j: (layer, i // tpb, 0, hc(j))),
        pl.BlockSpec((None, cw, d), lambda i, j: (layer, mc(j), 0)),
        pl.BlockSpec((None, cwh, d), lambda i, j: (layer, dm // cwh + hc(j), 0)),
    ]
    out_specs = [
        pl.BlockSpec((tm, d), lambda i, j: (i, 0)),
        pl.BlockSpec((nseg, nc, SUBLANES, cw), lambda i, j: (i // tpb, 0, 0, 0)),
    ]
    scratch = [pltpu.VMEM((tm, d), BF16), pltpu.VMEM((nc, nseg, SUBLANES, cw), F32)]
    est = (4 * tm * d * 4 + tm * d * 2 + 2 * (3 * d * cw + d * cwh + (cw + cwh) * d) * 2
           + 4 * nseg * n_mem * cwh * 4 + 8 * rb * cw * 4)
    x_out, buf = pl.pallas_call(
        functools.partial(_mix_conv_kernel, seg=seg, rb=rb, tiles_per_batch=tpb, nc=nc,
                          head_dim=head_dim),
        grid=(m // tm, nc + nh),
        in_specs=in_specs,
        out_specs=out_specs,
        out_shape=[jax.ShapeDtypeStruct((m, d), F32),
                   jax.ShapeDtypeStruct((nb, nc, SUBLANES, cw), F32)],
        scratch_shapes=scratch,
        compiler_params=_compiler_params(est),
        name="mix_conv",
    )(x, g, w_in, w_in, w_in, w_in, conv_w, _pad_history(prev), mem_k, mem_v, w_out, w_out)
    return x_out, _unpack_history(buf, prev.shape[-2])


def _mix_gmlp_kernel(x_ref, g_ref, wv_ref, wu_ref, wq_ref, lng_ref, lnb_ref, ws_ref, bias_ref, k_ref,
                     v_ref, wom_ref, woh_ref, o_ref, *rest, seg, rb, rblk, nc, head_dim, group_dim,
                     emit_v):
    if emit_v:
        vout_ref, xn_ref, vs_ref = rest
    else:
        vout_ref = None
        xn_ref, vs_ref = rest
    j = pl.program_id(1)
    tm = x_ref.shape[0]
    cw = wv_ref.shape[1]

    @pl.when(j == 0)
    def _():
        _init_tile(x_ref, g_ref, o_ref, xn_ref, rb)

    @pl.when(j < nc)
    def _():
        for r0 in range(0, tm, rb):
            vs_ref[j, r0:r0 + rb, :] = _gelu(_dot(xn_ref[r0:r0 + rb, :], wv_ref[...]))

    @pl.when(j == nc)
    def _():
        width = nc * cw
        for r0 in range(0, tm, rb):
            rows = slice(r0, r0 + rb)
            tot = vs_ref[0, rows, :].sum(axis=-1, keepdims=True)
            for c in range(1, nc):
                tot += vs_ref[c, rows, :].sum(axis=-1, keepdims=True)
            mean = tot / width
            sq = None
            for c in range(nc):
                dv = vs_ref[c, rows, :] - mean
                part = (dv * dv).sum(axis=-1, keepdims=True)
                sq = part if sq is None else sq + part
            rstd = lax.rsqrt(sq / width + EPS)
            for c in range(nc):
                cs = slice(c * cw, (c + 1) * cw)
                vn = (vs_ref[c, rows, :] - mean) * rstd * lng_ref[:, cs] + lnb_ref[:, cs]
                vs_ref[c, rows, :] = vn
                if emit_v:
                    vout_ref[rows, cs] = vn

    @pl.when(jnp.logical_and(j >= nc, j < 2 * nc))
    def _():
        tri = (lax.broadcasted_iota(jnp.int32, (rblk, rblk), 0)
               >= lax.broadcasted_iota(jnp.int32, (rblk, rblk), 1))
        wmats = [jnp.where(tri, ws_ref[gi, :rblk, :rblk], 0.0).astype(BF16)
                 for gi in range(cw // group_dim)]

        def lead(r0):
            u = _dot(xn_ref[r0:r0 + rb, :], wu_ref[...])
            vn = vs_ref[j - nc, r0:r0 + rb, :].astype(BF16)
            cols = []
            for gi, wmat in enumerate(wmats):
                blocks = [_dot(wmat, vn[r:r + rblk, gi * group_dim:(gi + 1) * group_dim])
                          for r in range(0, rb, rblk)]
                cols.append(_cat(blocks, 0))
            return u, _cat(cols, 1)

        def finish(r0, um):
            u, mixed = um
            bias = _cat([_cat([jnp.broadcast_to(bias_ref[gi, :rblk, :], (rblk, group_dim))]
                              * (rb // rblk), 0) for gi in range(len(wmats))], 1)
            y = _gelu(u) * (mixed + bias)
            o_ref[r0:r0 + rb, :] += _dot(y.astype(BF16), wom_ref[...])

        _pipelined(range(0, tm, rb), lead, finish)

    @pl.when(j >= 2 * nc)
    def _():
        _attention_step(xn_ref, wq_ref, k_ref, v_ref, woh_ref, o_ref, rb, seg, head_dim)


def _mix_gmlp(x, g, w_in, ln_g, ln_b, ws, bias, mem_k, mem_v, w_out, *, layer, slot, tm, rb, seg,
              batch_len, cw, cwh, head_dim, emit_v):
    m, d = x.shape
    dm = ln_g.shape[-1]
    nb, n_mem, dq = mem_k.shape[1:]
    groups, chunk = ws.shape[1:3]
    group_dim = dm // groups
    nseg = tm // seg
    tpb = batch_len // tm if nseg == 1 else 1
    rblk = min(chunk, seg)
    assert seg % rblk == 0 and rb % rblk == 0 and cw % group_dim == 0
    nc, nh = dm // cw, dq // cwh
    gpc = cw // group_dim
    uc = lambda j: jnp.clip(j - nc, 0, nc - 1)
    hc = lambda j: jnp.maximum(j - 2 * nc, 0)
    in_specs = [
        pl.BlockSpec((tm, d), lambda i, j: (i, 0)),
        pl.BlockSpec((None, 1, d), lambda i, j: (layer, 0, 0)),
        pl.BlockSpec((None, d, cw), lambda i, j: (slot, 0, nc + jnp.minimum(j, nc - 1))),
        pl.BlockSpec((None, d, cw), lambda i, j: (slot, 0, uc(j))),
        pl.BlockSpec((None, d, cwh), lambda i, j: (slot, 0, 2 * dm // cwh + hc(j))),
        pl.BlockSpec((None, 1, dm), lambda i, j: (slot, 0, 0)),
        pl.BlockSpec((None, 1, dm), lambda i, j: (slot, 0, 0)),
        pl.BlockSpec((None, gpc, chunk, chunk), lambda i, j: (slot, uc(j), 0, 0)),
        pl.BlockSpec((None, gpc, chunk, 1), lambda i, j: (slot, uc(j), 0, 0)),
        pl.BlockSpec((None, nseg, n_mem, cwh), lambda i, j: (layer, i // tpb, 0, hc(j))),
        pl.BlockSpec((None, nseg, n_mem, cwh), lambda i, j: (layer, i // tpb, 0, hc(j))),
        pl.BlockSpec((None, cw, d), lambda i, j: (layer, uc(j), 0)),
        pl.BlockSpec((None, cwh, d), lambda i, j: (layer, dm // cwh + hc(j), 0)),
    ]
    out_specs = [pl.BlockSpec((tm, d), lambda i, j: (i, 0))]
    out_shape = [jax.ShapeDtypeStruct((m, d), F32)]
    if emit_v:
        out_specs.append(pl.BlockSpec((tm, dm), lambda i, j: (i, 0)))
        out_shape.append(jax.ShapeDtypeStruct((m, dm), F32))
    est = (4 * tm * d * 4 + tm * d * 2 + tm * dm * 4 * (3 if emit_v else 1)
           + 2 * (2 * d * cw + d * cwh + (cw + cwh) * d) * 2
           + 4 * nseg * n_mem * cwh * 4 + 8 * rb * cw * 4)
    outs = pl.pallas_call(
        functools.partial(_mix_gmlp_kernel, seg=seg, rb=rb, rblk=rblk, nc=nc, head_dim=head_dim,
                          group_dim=group_dim, emit_v=emit_v),
        grid=(m // tm, 2 * nc + nh),
        in_specs=in_specs,
        out_specs=out_specs,
        out_shape=out_shape,
        scratch_shapes=[pltpu.VMEM((tm, d), BF16), pltpu.VMEM((nc, tm, cw), F32)],
        compiler_params=_compiler_params(est),
        name="mix_gmlp",
    )(x, g, w_in, w_in, w_in, ln_g, ln_b, ws, bias, mem_k, mem_v, w_out, w_out)
    return (outs[0], outs[1]) if emit_v else (outs[0], None)


def _ffn_kernel(x_ref, g_ref, wa_ref, wg_ref, cw_ref, cb_ref, prev_ref, wdn_ref, gf_ref, o_ref,
                buf_ref, xn_ref, carry_ref, *, seg, rb, tiles_per_batch, final_norm):
    i = pl.program_id(0)
    j = pl.program_id(1)
    tm = x_ref.shape[0]
    fc = wdn_ref.shape[0]

    @pl.when(j == 0)
    def _():
        _init_tile(x_ref, g_ref, o_ref, xn_ref, rb)

    first = True if tiles_per_batch == 1 else (i % tiles_per_batch) == 0
    conv = _RowConv(cw_ref[...], prev_ref, carry_ref, buf_ref, j, first, seg)

    def lead(r0):
        xn = xn_ref[r0:r0 + rb, :]
        return jnp.concatenate([_dot(xn, wa_ref[...]), _dot(xn, wg_ref[...])], axis=1)

    def finish(r0, up):
        z = conv.block(up, r0) + cb_ref[...]
        h = jax.nn.silu(z[:, fc:]) * z[:, :fc]
        o_ref[r0:r0 + rb, :] += _dot(h.astype(BF16), wdn_ref[...])

    _pipelined(range(0, tm, rb), lead, finish)

    if final_norm:
        @pl.when(j == pl.num_programs(1) - 1)
        def _():
            for r0 in range(0, tm, rb):
                o_ref[r0:r0 + rb, :] = _rms(o_ref[r0:r0 + rb, :], gf_ref[...])


def _ffn(x, g, w_up, conv_w, conv_b, prev, w_down, g_final, *, layer, tm, rb, seg, batch_len, fc,
         final_norm):
    m, d = x.shape
    dff = w_down.shape[1]
    nj = dff // fc
    nseg = tm // seg
    tpb = batch_len // tm if nseg == 1 else 1
    nb = prev.shape[1]
    in_specs = [
        pl.BlockSpec((tm, d), lambda i, j: (i, 0)),
        pl.BlockSpec((None, 1, d), lambda i, j: (layer, 0, 0)),
        pl.BlockSpec((None, d, fc), lambda i, j: (layer, 0, j)),
        pl.BlockSpec((None, d, fc), lambda i, j: (layer, 0, nj + j)),
        pl.BlockSpec((None, 3, 2 * fc), lambda i, j: (layer, 0, j)),
        pl.BlockSpec((None, 1, 2 * fc), lambda i, j: (layer, 0, j)),
        pl.BlockSpec((None, nseg, SUBLANES, 2 * fc), lambda i, j: (layer, i // tpb, 0, j)),
        pl.BlockSpec((None, fc, d), lambda i, j: (layer, j, 0)),
        pl.BlockSpec((1, d), lambda i, j: (0, 0)),
    ]
    out_specs = [
        pl.BlockSpec((tm, d), lambda i, j: (i, 0)),
        pl.BlockSpec((nseg, nj, SUBLANES, 2 * fc), lambda i, j: (i // tpb, 0, 0, 0)),
    ]
    scratch = [pltpu.VMEM((tm, d), BF16), pltpu.VMEM((nj, nseg, SUBLANES, 2 * fc), F32)]
    est = 4 * tm * d * 4 + tm * d * 2 + 2 * (3 * d * fc) * 2 + 8 * rb * 2 * fc * 4
    x_out, buf = pl.pallas_call(
        functools.partial(_ffn_kernel, seg=seg, rb=rb, tiles_per_batch=tpb, final_norm=final_norm),
        grid=(m // tm, nj),
        in_specs=in_specs,
        out_specs=out_specs,
        out_shape=[jax.ShapeDtypeStruct((m, d), F32),
                   jax.ShapeDtypeStruct((nb, nj, SUBLANES, 2 * fc), F32)],
        scratch_shapes=scratch,
        compiler_params=_compiler_params(est),
        name="conv_ffn",
    )(x, g, w_up, w_up, conv_w, conv_b, _pad_history(prev), w_down, g_final)
    return x_out, _unpack_history(buf, prev.shape[-2])


def _memkv_kernel(m_ref, g_ref, wk_ref, wv_ref, ko_ref, vo_ref):
    mn = _rms(m_ref[...], g_ref[0]).astype(BF16)
    ko_ref[0] = _dot(mn, wk_ref[0])
    vo_ref[0] = _dot(mn, wv_ref[0])


def _memkv(mem, g, wk, wv, *, tn):
    rows, d = mem.shape
    depth, _, dq = wk.shape
    shape = jax.ShapeDtypeStruct((depth, rows, dq), F32)
    return pl.pallas_call(
        _memkv_kernel,
        grid=(depth, dq // tn),
        in_specs=[
            pl.BlockSpec((rows, d), lambda l, n: (0, 0)),
            pl.BlockSpec((1, 1, d), lambda l, n: (l, 0, 0)),
            pl.BlockSpec((1, d, tn), lambda l, n: (l, 0, n)),
            pl.BlockSpec((1, d, tn), lambda l, n: (l, 0, n)),
        ],
        out_specs=[pl.BlockSpec((1, rows, tn), lambda l, n: (l, 0, n))] * 2,
        out_shape=[shape, shape],
        compiler_params=pltpu.CompilerParams(dimension_semantics=("arbitrary", "arbitrary")),
        name="memory_kv",
    )(mem, g, wk, wv)


def _group_cols(t, nj, fc):
    lead = t.shape[:-1]
    return t.reshape(lead + (2, nj, fc)).swapaxes(-3, -2).reshape(lead + (2 * nj * fc,))


def _ungroup_cols(t, nj, fc):
    lead = t.shape[:-1]
    return t.reshape(lead + (nj, 2, fc)).swapaxes(-3, -2).reshape(lead + (2 * nj * fc,))


def _tiling(t, tm, rb):
    return dict(tm=tm, rb=rb, seg=min(t, tm), batch_len=t)


def _trunk(x3, mem_k, mem_v, conv_a_prev, ffn_prev, p, *, tiles, emit_v):
    nb, t, d = x3.shape
    x = x3.reshape(nb * t, d)
    depth = p["norm_mix_g"].shape[0]
    conv_a_new, ffn_new, gmlp_v = [], [], []
    mixer = dict(cw=p["cw"], cwh=p["cwh"], head_dim=p["head_dim"])
    for i in range(depth):
        kind, jx = i % 2, i // 2
        if kind == 0:
            x, buf = _mix_conv(x, p["norm_mix_g"], p["w_in_a"], p["conv_a_w"], conv_a_prev, mem_k,
                               mem_v, p["w_out"], layer=i, slot=jx, **mixer,
                               **_tiling(t, *tiles["mix_conv"]))
            conv_a_new.append(buf)
        else:
            x, v_rows = _mix_gmlp(x, p["norm_mix_g"], p["w_in_b"], p["gmlp_norm_g"], p["gmlp_norm_b"],
                                  p["gmlp_ws"], p["gmlp_bias"], mem_k, mem_v, p["w_out"], layer=i,
                                  slot=jx, emit_v=emit_v, **mixer, **_tiling(t, *tiles["mix_gmlp"]))
            gmlp_v.append(v_rows)
        x, buf = _ffn(x, p["norm_ffn_g"], p["w_up"], p["ffn_conv_w"], p["ffn_conv_b"], ffn_prev,
                      p["w_down"], p["norm_final_g"], layer=i, fc=p["fc"],
                      final_norm=(i == depth - 1), **_tiling(t, *tiles["ffn"]))
        ffn_new.append(_ungroup_cols(buf, p["nj"], p["fc"]))
    y = x.reshape(nb, t, d)
    v_out = jnp.stack([v.reshape(nb, t, -1) for v in gmlp_v]) if emit_v else None
    return y, jnp.stack(conv_a_new), jnp.stack(ffn_new), v_out


def kernel(x_prompt, x_sample, mem_prompt, cache_conv_a, cache_ffn_conv, cache_mem_k, cache_mem_v,
           norm_mix_g, norm_mem_g, w_mem_k, w_mem_v, w_in_a, conv_a_w, w_in_b, gmlp_norm_g,
           gmlp_norm_b, gmlp_ws, gmlp_bias, w_out, norm_ffn_g, w_up, ffn_conv_w, ffn_conv_b, w_down,
           norm_final_g):
    b, s, d = x_prompt.shape
    nb_s, t_s, _ = x_sample.shape
    depth = norm_mix_g.shape[0]
    n_mem = mem_prompt.shape[1]
    heads, head_dim = cache_mem_k.shape[-2:]
    dq = heads * head_dim
    dff = w_down.shape[1]
    fc = 512
    assert dff % fc == 0
    nj = dff // fc

    p = dict(
        norm_mix_g=norm_mix_g[:, None, :], norm_ffn_g=norm_ffn_g[:, None, :],
        norm_final_g=norm_final_g[None],
        w_in_a=w_in_a.astype(BF16), w_in_b=w_in_b.astype(BF16), w_out=w_out.astype(BF16),
        conv_a_w=conv_a_w, gmlp_norm_g=gmlp_norm_g[:, None, :], gmlp_norm_b=gmlp_norm_b[:, None, :],
        gmlp_ws=gmlp_ws, gmlp_bias=gmlp_bias[..., None],
        w_up=w_up.astype(BF16), w_down=w_down.astype(BF16),
        ffn_conv_w=_group_cols(ffn_conv_w, nj, fc),
        ffn_conv_b=_group_cols(ffn_conv_b, nj, fc)[:, None, :],
        fc=fc, nj=nj, cw=512, cwh=512, head_dim=head_dim,
    )

    mk, mv = _memkv(mem_prompt.reshape(b * n_mem, d), norm_mem_g[:, None, :],
                    w_mem_k.astype(BF16), w_mem_v.astype(BF16), tn=512)
    mk = mk.reshape(depth, b, n_mem, dq)
    mv = mv.reshape(depth, b, n_mem, dq)
    conv_a_zero = jnp.zeros((cache_conv_a.shape[0], b) + cache_conv_a.shape[2:], F32)
    ffn_zero = jnp.zeros((depth, b) + cache_ffn_conv.shape[2:], F32)
    y_prompt, conv_a_prompt, ffn_conv_prompt, _ = _trunk(
        x_prompt, mk, mv, conv_a_zero, ffn_zero, p, emit_v=False,
        tiles=dict(mix_conv=(512, 256), mix_gmlp=(512, 256), ffn=(1024, 256)))

    y_sample, conv_a_sample, ffn_conv_sample, gmlp_v_sample = _trunk(
        x_sample, cache_mem_k.reshape(depth, nb_s, n_mem, dq),
        cache_mem_v.reshape(depth, nb_s, n_mem, dq), cache_conv_a,
        _group_cols(cache_ffn_conv, nj, fc), p, emit_v=True,
        tiles=dict.fromkeys(("mix_conv", "mix_gmlp", "ffn"), (nb_s * t_s, nb_s * t_s)))

    return (y_prompt, y_sample, conv_a_prompt, ffn_conv_prompt,
            mk.reshape(depth, b, n_mem, heads, head_dim), mv.reshape(depth, b, n_mem, heads, head_dim),
            conv_a_sample, ffn_conv_sample, gmlp_v_sample)
```

```python
import functools

import jax
import jax.numpy as jnp
from jax import lax
from jax.experimental import pallas as pl
from jax.experimental.pallas import tpu as pltpu

EPS = 1e-6
SUBLANES = 8
VMEM_REQUEST_CAP = 58 * 1024 * 1024

F32 = jnp.float32
BF16 = jnp.bfloat16


def _vmem_limit(est_bytes):
    return int(min(VMEM_REQUEST_CAP, max(32 * 1024 * 1024, est_bytes * 5 // 4)))


def _compiler_params(est_bytes):
    return pltpu.CompilerParams(dimension_semantics=("arbitrary", "arbitrary"),
                                vmem_limit_bytes=_vmem_limit(est_bytes))


def _dot(a, b):
    return jnp.dot(a, b, preferred_element_type=F32)


def _rms(x, g):
    return x * lax.rsqrt(jnp.mean(x * x, axis=-1, keepdims=True) + EPS) * g


def _gelu(x):
    return 0.5 * x * (1.0 + lax.erf(x * (2.0 ** -0.5)))


def _silu(x):
    return (0.5 * x) * (1.0 + jnp.tanh(0.5 * x))


def _cat(parts, axis):
    return parts[0] if len(parts) == 1 else jnp.concatenate(parts, axis=axis)


def _segment_spans(row0, nrows, seg):
    spans, pos = [], row0
    while pos < row0 + nrows:
        s, off = divmod(pos, seg)
        n = min(seg - off, row0 + nrows - pos)
        spans.append((s, off, n))
        pos += n
    return spans


def _conv3(u, h0, h1, w):
    s, c = u.shape
    r1 = pltpu.roll(u, 1, axis=0)
    r2 = pltpu.roll(u, 2, axis=0)
    row = lax.broadcasted_iota(jnp.int32, (SUBLANES, c), 0)
    top1 = jnp.where(row == 0, h1, r1[:SUBLANES])
    top2 = jnp.where(row == 0, h0, jnp.where(row == 1, h1, r2[:SUBLANES]))
    if s > SUBLANES:
        r1 = jnp.concatenate([top1, r1[SUBLANES:]], axis=0)
        r2 = jnp.concatenate([top2, r2[SUBLANES:]], axis=0)
    else:
        r1, r2 = top1, top2
    return w[0:1] * r2 + w[1:2] * r1 + w[2:3] * u


class _RowConv:
    def __init__(self, w, prev_ref, carry_ref, buf_ref, slot, first, seg):
        self.w, self.carry_ref, self.buf_ref, self.slot, self.seg = w, carry_ref, buf_ref, slot, seg
        self.tail = None

        def load_prev():
            for s in range(prev_ref.shape[0]):
                carry_ref[slot, s] = prev_ref[s]

        if first is True:
            load_prev()
        else:
            pl.when(first)(load_prev)

    def block(self, u, row0):
        outs, pos = [], 0
        for s, off, n in _segment_spans(row0, u.shape[0], self.seg):
            us = u[pos:pos + n]
            hist = self.carry_ref[self.slot, s] if off == 0 else self.tail
            outs.append(_conv3(us, hist[SUBLANES - 2:SUBLANES - 1], hist[SUBLANES - 1:], self.w))
            self.tail = us[n - SUBLANES:]
            if off + n == self.seg:
                self.carry_ref[self.slot, s] = self.tail
                self.buf_ref[s, self.slot] = self.tail
            pos += n
        return _cat(outs, 0)


def _pad_history(prev):
    pad = [(0, 0)] * prev.ndim
    pad[-2] = (SUBLANES - prev.shape[-2], 0)
    return jnp.pad(prev, pad)


def _unpack_history(buf, rows):
    nb, slots, _, c = buf.shape
    return buf[:, :, SUBLANES - rows:, :].swapaxes(1, 2).reshape(nb, rows, slots * c)


def _attention(q, k_ref, v_ref, row0, seg, head_dim):
    width = q.shape[1]
    scale = head_dim ** -0.5
    spans = _segment_spans(row0, q.shape[0], seg)
    head_cols = [slice(h * head_dim, (h + 1) * head_dim) for h in range(width // head_dim)]
    scores, pos = [], 0
    for s, _, n in spans:
        for cs in head_cols:
            qh = q[pos:pos + n, cs].astype(BF16)
            kh = k_ref[s, :, cs].astype(BF16)
            scores.append(lax.dot_general(qh, kh, (((1,), (1,)), ((), ())),
                                          preferred_element_type=F32) * scale)
        pos += n
    rows = []
    for si, (s, _, n) in enumerate(spans):
        heads = []
        for hi, cs in enumerate(head_cols):
            sc = scores[si * len(head_cols) + hi]
            m = jnp.max(sc, axis=-1, keepdims=True)
            p = jnp.exp(sc - m)
            l = jnp.sum(p, axis=-1, keepdims=True)
            heads.append(_dot(p.astype(BF16), v_ref[s, :, cs].astype(BF16)) / l)
        rows.append(_cat(heads, 1))
    return _cat(rows, 0)


def _pipelined(starts, lead, finish):
    cur = lead(starts[0])
    for k, r0 in enumerate(starts):
        nxt = lead(starts[k + 1]) if k + 1 < len(starts) else None
        finish(r0, cur)
        cur = nxt


def _init_tile(x_ref, g_ref, o_ref, xn_ref, rb):
    for r0 in range(0, x_ref.shape[0], rb):
        x = x_ref[r0:r0 + rb, :]
        xn_ref[r0:r0 + rb, :] = _rms(x, g_ref[...]).astype(BF16)
        o_ref[r0:r0 + rb, :] = x


def _attention_step(xn_ref, wq_ref, k_ref, v_ref, woh_ref, o_ref, rb, seg, head_dim):
    def lead(r0):
        return _dot(xn_ref[r0:r0 + rb, :], wq_ref[...])

    def finish(r0, q):
        y = _attention(q, k_ref, v_ref, r0, seg, head_dim)
        o_ref[r0:r0 + rb, :] += _dot(y.astype(BF16), woh_ref[...])

    _pipelined(range(0, xn_ref.shape[0], rb), lead, finish)


def _mix_conv_kernel(x_ref, g_ref, wb_ref, wc_ref, wh_ref, wq_ref, cw_ref, prev_ref, k_ref, v_ref,
                     wom_ref, woh_ref, o_ref, buf_ref, xn_ref, carry_ref, *, seg, rb,
                     tiles_per_batch, nc, head_dim):
    i = pl.program_id(0)
    j = pl.program_id(1)
    tm = x_ref.shape[0]

    @pl.when(j == 0)
    def _():
        _init_tile(x_ref, g_ref, o_ref, xn_ref, rb)

    @pl.when(j < nc)
    def _():
        first = True if tiles_per_batch == 1 else (i % tiles_per_batch) == 0
        conv = _RowConv(cw_ref[...], prev_ref, carry_ref, buf_ref, j, first, seg)

        def lead(r0):
            xn = xn_ref[r0:r0 + rb, :]
            return _dot(xn, wc_ref[...]), _dot(xn, wh_ref[...]), _dot(xn, wb_ref[...])

        def finish(r0, chb):
            c, h, b = chb
            y = b * conv.block(c * h, r0)
            o_ref[r0:r0 + rb, :] += _dot(y.astype(BF16), wom_ref[...])

        _pipelined(range(0, tm, rb), lead, finish)

    @pl.when(j >= nc)
    def _():
        _attention_step(xn_ref, wq_ref, k_ref, v_ref, woh_ref, o_ref, rb, seg, head_dim)


def _mix_conv(x, g, w_in, conv_w, prev, mem_k, mem_v, w_out, *, layer, slot, tm, rb, seg, batch_len,
              cw, cwh, head_dim):
    m, d = x.shape
    dm = conv_w.shape[-1]
    nb, n_mem, dq = mem_k.shape[1:]
    nseg = tm // seg
    tpb = batch_len // tm if nseg == 1 else 1
    nc, nh = dm // cw, dq // cwh
    mc = lambda j: jnp.minimum(j, nc - 1)
    hc = lambda j: jnp.maximum(j - nc, 0)
    in_specs = [
        pl.BlockSpec((tm, d), lambda i, j: (i, 0)),
        pl.BlockSpec((None, 1, d), lambda i, j: (layer, 0, 0)),
        pl.BlockSpec((None, d, cw), lambda i, j: (slot, 0, mc(j))),
        pl.BlockSpec((None, d, cw), lambda i, j: (slot, 0, nc + mc(j))),
        pl.BlockSpec((None, d, cw), lambda i, j: (slot, 0, 2 * nc + mc(j))),
        pl.BlockSpec((None, d, cwh), lambda i, j: (slot, 0, 3 * dm // cwh + hc(j))),
        pl.BlockSpec((None, 3, cw), lambda i, j: (slot, 0, mc(j))),
        pl.BlockSpec((None, nseg, SUBLANES, cw), lambda i, j: (slot, i // tpb, 0, mc(j))),
        pl.BlockSpec((None, nseg, n_mem, cwh), lambda i, j: (layer, i // tpb, 0, hc(j))),
        pl.BlockSpec((None, nseg, n_mem, cwh), lambda i, j: (layer, i // tpb, 0, hc(j))),
        pl.BlockSpec((None, cw, d), lambda i, j: (layer, mc(j), 0)),
        pl.BlockSpec((None, cwh, d), lambda i, j: (layer, dm // cwh + hc(j), 0)),
    ]
    out_specs = [
        pl.BlockSpec((tm, d), lambda i, j: (i, 0)),
        pl.BlockSpec((nseg, nc, SUBLANES, cw), lambda i, j: (i // tpb, 0, 0, 0)),
    ]
    scratch = [pltpu.VMEM((tm, d), BF16), pltpu.VMEM((nc, nseg, SUBLANES, cw), F32)]
    est = (4 * tm * d * 4 + tm * d * 2 + 2 * (3 * d * cw + d * cwh + (cw + cwh) * d) * 2
           + 4 * nseg * n_mem * cwh * 4 + 8 * rb * cw * 4)
    x_out, buf = pl.pallas_call(
        functools.partial(_mix_conv_kernel, seg=seg, rb=rb, tiles_per_batch=tpb, nc=nc,
                          head_dim=head_dim),
        grid=(m // tm, nc + nh),
        in_specs=in_specs,
        out_specs=out_specs,
        out_shape=[jax.ShapeDtypeStruct((m, d), F32),
                   jax.ShapeDtypeStruct((nb, nc, SUBLANES, cw), F32)],
        scratch_shapes=scratch,
        compiler_params=_compiler_params(est),
        name="mix_conv",
    )(x, g, w_in, w_in, w_in, w_in, conv_w, _pad_history(prev), mem_k, mem_v, w_out, w_out)
    return x_out, _unpack_history(buf, prev.shape[-2])


def _mix_gmlp_kernel(x_ref, g_ref, wv_ref, wu_ref, wq_ref, lng_ref, lnb_ref, ws_ref, bias_ref, k_ref,
                     v_ref, wom_ref, woh_ref, o_ref, *rest, seg, rb, rblk, nc, head_dim, group_dim,
                     emit_v):
    if emit_v:
        vout_ref, xn_ref, vs_ref = rest
    else:
        vout_ref = None
        xn_ref, vs_ref = rest
    j = pl.program_id(1)
    tm = x_ref.shape[0]
    cw = wv_ref.shape[1]

    @pl.when(j == 0)
    def _():
        _init_tile(x_ref, g_ref, o_ref, xn_ref, rb)

    @pl.when(j < nc)
    def _():
        for r0 in range(0, tm, rb):
            vs_ref[j, r0:r0 + rb, :] = _gelu(_dot(xn_ref[r0:r0 + rb, :], wv_ref[...]))

    @pl.when(j == nc)
    def _():
        width = nc * cw
        for r0 in range(0, tm, rb):
            rows = slice(r0, r0 + rb)
            tot = vs_ref[0, rows, :].sum(axis=-1, keepdims=True)
            for c in range(1, nc):
                tot += vs_ref[c, rows, :].sum(axis=-1, keepdims=True)
            mean = tot / width
            sq = None
            for c in range(nc):
                dv = vs_ref[c, rows, :] - mean
                part = (dv * dv).sum(axis=-1, keepdims=True)
                sq = part if sq is None else sq + part
            rstd = lax.rsqrt(sq / width + EPS)
            for c in range(nc):
                cs = slice(c * cw, (c + 1) * cw)
                vn = (vs_ref[c, rows, :] - mean) * rstd * lng_ref[:, cs] + lnb_ref[:, cs]
                vs_ref[c, rows, :] = vn
                if emit_v:
                    vout_ref[rows, cs] = vn

    @pl.when(jnp.logical_and(j >= nc, j < 2 * nc))
    def _():
        tri = (lax.broadcasted_iota(jnp.int32, (rblk, rblk), 0)
               >= lax.broadcasted_iota(jnp.int32, (rblk, rblk), 1))
        wmats = [jnp.where(tri, ws_ref[gi, :rblk, :rblk], 0.0).astype(BF16)
                 for gi in range(cw // group_dim)]

        def lead(r0):
            u = _dot(xn_ref[r0:r0 + rb, :], wu_ref[...])
            vn = vs_ref[j - nc, r0:r0 + rb, :].astype(BF16)
            cols = []
            for gi, wmat in enumerate(wmats):
                blocks = [_dot(wmat, vn[r:r + rblk, gi * group_dim:(gi + 1) * group_dim])
                          for r in range(0, rb, rblk)]
                cols.append(_cat(blocks, 0))
            return u, _cat(cols, 1)

        def finish(r0, um):
            u, mixed = um
            bias = _cat([_cat([jnp.broadcast_to(bias_ref[gi, :rblk, :], (rblk, group_dim))]
                              * (rb // rblk), 0) for gi in range(len(wmats))], 1)
            y = _gelu(u) * (mixed + bias)
            o_ref[r0:r0 + rb, :] += _dot(y.astype(BF16), wom_ref[...])

        _pipelined(range(0, tm, rb), lead, finish)

    @pl.when(j >= 2 * nc)
    def _():
        _attention_step(xn_ref, wq_ref, k_ref, v_ref, woh_ref, o_ref, rb, seg, head_dim)


def _mix_gmlp(x, g, w_in, ln_g, ln_b, ws, bias, mem_k, mem_v, w_out, *, layer, slot, tm, rb, seg,
              batch_len, cw, cwh, head_dim, emit_v):
    m, d = x.shape
    dm = ln_g.shape[-1]
    nb, n_mem, dq = mem_k.shape[1:]
    groups, chunk = ws.shape[1:3]
    group_dim = dm // groups
    nseg = tm // seg
    tpb = batch_len // tm if nseg == 1 else 1
    rblk = min(chunk, seg)
    assert seg % rblk == 0 and rb % rblk == 0 and cw % group_dim == 0
    nc, nh = dm // cw, dq // cwh
    gpc = cw // group_dim
    uc = lambda j: jnp.clip(j - nc, 0, nc - 1)
    hc = lambda j: jnp.maximum(j - 2 * nc, 0)
    in_specs = [
        pl.BlockSpec((tm, d), lambda i, j: (i, 0)),
        pl.BlockSpec((None, 1, d), lambda i, j: (layer, 0, 0)),
        pl.BlockSpec((None, d, cw), lambda i, j: (slot, 0, nc + jnp.minimum(j, nc - 1))),
        pl.BlockSpec((None, d, cw), lambda i, j: (slot, 0, uc(j))),
        pl.BlockSpec((None, d, cwh), lambda i, j: (slot, 0, 2 * dm // cwh + hc(j))),
        pl.BlockSpec((None, 1, dm), lambda i, j: (slot, 0, 0)),
        pl.BlockSpec((None, 1, dm), lambda i, j: (slot, 0, 0)),
        pl.BlockSpec((None, gpc, chunk, chunk), lambda i, j: (slot, uc(j), 0, 0)),
        pl.BlockSpec((None, gpc, chunk, 1), lambda i, j: (slot, uc(j), 0, 0)),
        pl.BlockSpec((None, nseg, n_mem, cwh), lambda i, j: (layer, i // tpb, 0, hc(j))),
        pl.BlockSpec((None, nseg, n_mem, cwh), lambda i, j: (layer, i // tpb, 0, hc(j))),
        pl.BlockSpec((None, cw, d), lambda i, j: (layer, uc(j), 0)),
        pl.BlockSpec((None, cwh, d), lambda i, j: (layer, dm // cwh + hc(j), 0)),
    ]
    out_specs = [pl.BlockSpec((tm, d), lambda i, j: (i, 0))]
    out_shape = [jax.ShapeDtypeStruct((m, d), F32)]
    if emit_v:
        out_specs.append(pl.BlockSpec((tm, dm), lambda i, j: (i, 0)))
        out_shape.append(jax.ShapeDtypeStruct((m, dm), F32))
    est = (4 * tm * d * 4 + tm * d * 2 + tm * dm * 4 * (3 if emit_v else 1)
           + 2 * (2 * d * cw + d * cwh + (cw + cwh) * d) * 2
           + 4 * nseg * n_mem * cwh * 4 + 8 * rb * cw * 4)
    outs = pl.pallas_call(
        functools.partial(_mix_gmlp_kernel, seg=seg, rb=rb, rblk=rblk, nc=nc, head_dim=head_dim,
                          group_dim=group_dim, emit_v=emit_v),
        grid=(m // tm, 2 * nc + nh),
        in_specs=in_specs,
        out_specs=out_specs,
        out_shape=out_shape,
        scratch_shapes=[pltpu.VMEM((tm, d), BF16), pltpu.VMEM((nc, tm, cw), F32)],
        compiler_params=_compiler_params(est),
        name="mix_gmlp",
    )(x, g, w_in, w_in, w_in, ln_g, ln_b, ws, bias, mem_k, mem_v, w_out, w_out)
    return (outs[0], outs[1]) if emit_v else (outs[0], None)


def _ffn_kernel(x_ref, g_ref, wup_hbm, wdn_hbm, cw_ref, cb_ref, prev_ref, gf_ref, o_ref, buf_ref,
                xn_ref, wa_buf, wg_buf, wdn_buf, sem, carry_ref, *, layer, seg, rb, tiles_per_batch,
                final_norm):
    i = pl.program_id(0)
    last_tile = i == pl.num_programs(0) - 1
    tm = x_ref.shape[0]
    nj, _, fc2 = cw_ref.shape
    fc = fc2 // 2
    dff = nj * fc

    def weight_copies(c, slot):
        a0 = pl.multiple_of(c * fc, fc)
        g0 = pl.multiple_of(dff + c * fc, fc)
        return (
            pltpu.make_async_copy(wup_hbm.at[layer, :, pl.ds(a0, fc)], wa_buf.at[slot], sem.at[0, slot]),
            pltpu.make_async_copy(wup_hbm.at[layer, :, pl.ds(g0, fc)], wg_buf.at[slot], sem.at[1, slot]),
            pltpu.make_async_copy(wdn_hbm.at[layer, pl.ds(a0, fc), :], wdn_buf.at[slot], sem.at[2, slot]),
        )

    @pl.when(i == 0)
    def _():
        for cp in weight_copies(0, 0):
            cp.start()

    _init_tile(x_ref, g_ref, o_ref, xn_ref, rb)
    first = True if tiles_per_batch == 1 else (i % tiles_per_batch) == 0

    def chunk(j, carry):
        slot = (i * nj + j) % 2
        for cp in weight_copies(j, slot):
            cp.wait()

        @pl.when(jnp.logical_not(jnp.logical_and(last_tile, j == nj - 1)))
        def _():
            for cp in weight_copies(jnp.where(j == nj - 1, 0, j + 1), 1 - slot):
                cp.start()

        conv = _RowConv(cw_ref[j], prev_ref.at[:, j], carry_ref, buf_ref, j, first, seg)

        def lead(r0):
            xn = xn_ref[r0:r0 + rb, :]
            return jnp.concatenate([_dot(xn, wa_buf[slot]), _dot(xn, wg_buf[slot])], axis=1)

        def finish(r0, up):
            z = conv.block(up, r0) + cb_ref[j]
            h = _silu(z[:, fc:]) * z[:, :fc]
            o_ref[r0:r0 + rb, :] += _dot(h.astype(BF16), wdn_buf[slot])

        _pipelined(range(0, tm, rb), lead, finish)
        return carry

    lax.fori_loop(0, nj, chunk, 0)

    if final_norm:
        for r0 in range(0, tm, rb):
            o_ref[r0:r0 + rb, :] = _rms(o_ref[r0:r0 + rb, :], gf_ref[...])


def _ffn(x, g, w_up, conv_w, conv_b, prev, w_down, g_final, *, layer, tm, rb, seg, batch_len, fc,
         final_norm):
    m, d = x.shape
    dff = w_down.shape[1]
    nj = dff // fc
    nseg = tm // seg
    tpb = batch_len // tm if nseg == 1 else 1
    nb = prev.shape[1]
    in_specs = [
        pl.BlockSpec((tm, d), lambda i: (i, 0)),
        pl.BlockSpec((None, 1, d), lambda i: (layer, 0, 0)),
        pl.BlockSpec(memory_space=pl.ANY),
        pl.BlockSpec(memory_space=pl.ANY),
        pl.BlockSpec((None, nj, 3, 2 * fc), lambda i: (layer, 0, 0, 0)),
        pl.BlockSpec((None, nj, 1, 2 * fc), lambda i: (layer, 0, 0, 0)),
        pl.BlockSpec((None, nseg, nj, SUBLANES, 2 * fc), lambda i: (layer, i // tpb, 0, 0, 0)),
        pl.BlockSpec((1, d), lambda i: (0, 0)),
    ]
    out_specs = [
        pl.BlockSpec((tm, d), lambda i: (i, 0)),
        pl.BlockSpec((nseg, nj, SUBLANES, 2 * fc), lambda i: (i // tpb, 0, 0, 0)),
    ]
    scratch = [pltpu.VMEM((tm, d), BF16), pltpu.VMEM((2, d, fc), BF16), pltpu.VMEM((2, d, fc), BF16),
               pltpu.VMEM((2, fc, d), BF16), pltpu.SemaphoreType.DMA((3, 2)),
               pltpu.VMEM((nj, nseg, SUBLANES, 2 * fc), F32)]
    est = 4 * tm * d * 4 + tm * d * 2 + 2 * (3 * d * fc) * 2 + 8 * rb * 2 * fc * 4
    x_out, buf = pl.pallas_call(
        functools.partial(_ffn_kernel, layer=layer, seg=seg, rb=rb, tiles_per_batch=tpb,
                          final_norm=final_norm),
        grid=(m // tm,),
        in_specs=in_specs,
        out_specs=out_specs,
        out_shape=[jax.ShapeDtypeStruct((m, d), F32),
                   jax.ShapeDtypeStruct((nb, nj, SUBLANES, 2 * fc), F32)],
        scratch_shapes=scratch,
        compiler_params=pltpu.CompilerParams(dimension_semantics=("arbitrary",),
                                             vmem_limit_bytes=_vmem_limit(est)),
        name="conv_ffn",
    )(x, g, w_up, w_down, conv_w, conv_b, _chunk_cols(_pad_history(prev), nj, fc), g_final)
    return x_out, _unpack_history(buf, prev.shape[-2])


def _memkv_kernel(m_ref, g_ref, wk_ref, wv_ref, ko_ref, vo_ref):
    mn = _rms(m_ref[...], g_ref[0]).astype(BF16)
    ko_ref[0] = _dot(mn, wk_ref[0])
    vo_ref[0] = _dot(mn, wv_ref[0])


def _memkv(mem, g, wk, wv, *, tn):
    rows, d = mem.shape
    depth, _, dq = wk.shape
    shape = jax.ShapeDtypeStruct((depth, rows, dq), F32)
    return pl.pallas_call(
        _memkv_kernel,
        grid=(depth, dq // tn),
        in_specs=[
            pl.BlockSpec((rows, d), lambda l, n: (0, 0)),
            pl.BlockSpec((1, 1, d), lambda l, n: (l, 0, 0)),
            pl.BlockSpec((1, d, tn), lambda l, n: (l, 0, n)),
            pl.BlockSpec((1, d, tn), lambda l, n: (l, 0, n)),
        ],
        out_specs=[pl.BlockSpec((1, rows, tn), lambda l, n: (l, 0, n))] * 2,
        out_shape=[shape, shape],
        compiler_params=pltpu.CompilerParams(dimension_semantics=("arbitrary", "arbitrary")),
        name="memory_kv",
    )(mem, g, wk, wv)


def _chunk_cols(t, nj, fc):
    lead, r = t.shape[:-2], t.shape[-2]
    t = jnp.moveaxis(t.reshape(lead + (r, 2, nj, fc)), -2, -4)
    return t.reshape(lead + (nj, r, 2 * fc))


def _ungroup_cols(t, nj, fc):
    lead = t.shape[:-1]
    return t.reshape(lead + (nj, 2, fc)).swapaxes(-3, -2).reshape(lead + (2 * nj * fc,))


def _tiling(t, tm, rb):
    return dict(tm=tm, rb=rb, seg=min(t, tm), batch_len=t)


def _trunk(x3, mem_k, mem_v, conv_a_prev, ffn_prev, p, *, tiles, emit_v):
    nb, t, d = x3.shape
    x = x3.reshape(nb * t, d)
    depth = p["norm_mix_g"].shape[0]
    conv_a_new, ffn_new, gmlp_v = [], [], []
    mixer = dict(cw=p["cw"], cwh=p["cwh"], head_dim=p["head_dim"])
    for i in range(depth):
        kind, jx = i % 2, i // 2
        if kind == 0:
            x, buf = _mix_conv(x, p["norm_mix_g"], p["w_in_a"], p["conv_a_w"], conv_a_prev, mem_k,
                               mem_v, p["w_out"], layer=i, slot=jx, **mixer,
                               **_tiling(t, *tiles["mix_conv"]))
            conv_a_new.append(buf)
        else:
            x, v_rows = _mix_gmlp(x, p["norm_mix_g"], p["w_in_b"], p["gmlp_norm_g"], p["gmlp_norm_b"],
                                  p["gmlp_ws"], p["gmlp_bias"], mem_k, mem_v, p["w_out"], layer=i,
                                  slot=jx, emit_v=emit_v, **mixer, **_tiling(t, *tiles["mix_gmlp"]))
            gmlp_v.append(v_rows)
        x, buf = _ffn(x, p["norm_ffn_g"], p["w_up"], p["ffn_conv_w"], p["ffn_conv_b"], ffn_prev,
                      p["w_down"], p["norm_final_g"], layer=i, fc=p["fc"],
                      final_norm=(i == depth - 1), **_tiling(t, *tiles["ffn"]))
        ffn_new.append(_ungroup_cols(buf, p["nj"], p["fc"]))
    y = x.reshape(nb, t, d)
    v_out = jnp.stack([v.reshape(nb, t, -1) for v in gmlp_v]) if emit_v else None
    return y, jnp.stack(conv_a_new), jnp.stack(ffn_new), v_out


def kernel(x_prompt, x_sample, mem_prompt, cache_conv_a, cache_ffn_conv, cache_mem_k, cache_mem_v,
           norm_mix_g, norm_mem_g, w_mem_k, w_mem_v, w_in_a, conv_a_w, w_in_b, gmlp_norm_g,
           gmlp_norm_b, gmlp_ws, gmlp_bias, w_out, norm_ffn_g, w_up, ffn_conv_w, ffn_conv_b, w_down,
           norm_final_g):
    b, s, d = x_prompt.shape
    nb_s, t_s, _ = x_sample.shape
    depth = norm_mix_g.shape[0]
    n_mem = mem_prompt.shape[1]
    heads, head_dim = cache_mem_k.shape[-2:]
    dq = heads * head_dim
    dff = w_down.shape[1]
    fc = 512
    assert dff % fc == 0
    nj = dff // fc

    p = dict(
        norm_mix_g=norm_mix_g[:, None, :], norm_ffn_g=norm_ffn_g[:, None, :],
        norm_final_g=norm_final_g[None],
        w_in_a=w_in_a.astype(BF16), w_in_b=w_in_b.astype(BF16), w_out=w_out.astype(BF16),
        conv_a_w=conv_a_w, gmlp_norm_g=gmlp_norm_g[:, None, :], gmlp_norm_b=gmlp_norm_b[:, None, :],
        gmlp_ws=gmlp_ws, gmlp_bias=gmlp_bias[..., None],
        w_up=w_up.astype(BF16), w_down=w_down.astype(BF16),
        ffn_conv_w=_chunk_cols(ffn_conv_w, nj, fc),
        ffn_conv_b=_chunk_cols(ffn_conv_b[:, None, :], nj, fc),
        fc=fc, nj=nj, cw=512, cwh=512, head_dim=head_dim,
    )

    y_sample, conv_a_sample, ffn_conv_sample, gmlp_v_sample = _trunk(
        x_sample, cache_mem_k.reshape(depth, nb_s, n_mem, dq),
        cache_mem_v.reshape(depth, nb_s, n_mem, dq), cache_conv_a,
        cache_ffn_conv, p, emit_v=True,
        tiles=dict.fromkeys(("mix_conv", "mix_gmlp", "ffn"), (nb_s * t_s, nb_s * t_s)))

    mk, mv = _memkv(mem_prompt.reshape(b * n_mem, d), norm_mem_g[:, None, :],
                    w_mem_k.astype(BF16), w_mem_v.astype(BF16), tn=512)
    mk = mk.reshape(depth, b, n_mem, dq)
    mv = mv.reshape(depth, b, n_mem, dq)
    conv_a_zero = jnp.zeros((cache_conv_a.shape[0], b) + cache_conv_a.shape[2:], F32)
    ffn_zero = jnp.zeros((depth, b) + cache_ffn_conv.shape[2:], F32)
    y_prompt, conv_a_prompt, ffn_conv_prompt, _ = _trunk(
        x_prompt, mk, mv, conv_a_zero, ffn_zero, p, emit_v=False,
        tiles=dict(mix_conv=(512, 256), mix_gmlp=(512, 256), ffn=(1024, 256)))

    return (y_prompt, y_sample, conv_a_prompt, ffn_conv_prompt,
            mk.reshape(depth, b, n_mem, heads, head_dim), mv.reshape(depth, b, n_mem, heads, head_dim),
            conv_a_sample, ffn_conv_sample, gmlp_v_sample)
```

```python
import functools

import jax
import jax.numpy as jnp
from jax import lax
from jax.experimental import pallas as pl
from jax.experimental.pallas import tpu as pltpu

EPS = 1e-6
SUBLANES = 8
VMEM_REQUEST_CAP = 58 * 1024 * 1024

F32 = jnp.float32
BF16 = jnp.bfloat16


def _vmem_limit(est_bytes):
    return int(min(VMEM_REQUEST_CAP, max(32 * 1024 * 1024, est_bytes * 5 // 4)))


def _compiler_params(est_bytes):
    return pltpu.CompilerParams(dimension_semantics=("arbitrary", "arbitrary"),
                                vmem_limit_bytes=_vmem_limit(est_bytes))


def _dot(a, b):
    return jnp.dot(a, b, preferred_element_type=F32)


def _rms(x, g):
    return x * lax.rsqrt(jnp.mean(x * x, axis=-1, keepdims=True) + EPS) * g


def _gelu(x):
    return 0.5 * x * (1.0 + lax.erf(x * (2.0 ** -0.5)))


def _silu(x):
    return (0.5 * x) * (1.0 + jnp.tanh(0.5 * x))


def _cat(parts, axis):
    return parts[0] if len(parts) == 1 else jnp.concatenate(parts, axis=axis)


def _segment_spans(row0, nrows, seg):
    spans, pos = [], row0
    while pos < row0 + nrows:
        s, off = divmod(pos, seg)
        n = min(seg - off, row0 + nrows - pos)
        spans.append((s, off, n))
        pos += n
    return spans


def _conv3(u, h0, h1, w):
    s, c = u.shape
    r1 = pltpu.roll(u, 1, axis=0)
    r2 = pltpu.roll(u, 2, axis=0)
    row = lax.broadcasted_iota(jnp.int32, (SUBLANES, c), 0)
    top1 = jnp.where(row == 0, h1, r1[:SUBLANES])
    top2 = jnp.where(row == 0, h0, jnp.where(row == 1, h1, r2[:SUBLANES]))
    if s > SUBLANES:
        r1 = jnp.concatenate([top1, r1[SUBLANES:]], axis=0)
        r2 = jnp.concatenate([top2, r2[SUBLANES:]], axis=0)
    else:
        r1, r2 = top1, top2
    return w[0:1] * r2 + w[1:2] * r1 + w[2:3] * u


class _RowConv:
    def __init__(self, w, prev_ref, carry_ref, buf_ref, slot, first, seg):
        self.w, self.carry_ref, self.buf_ref, self.slot, self.seg = w, carry_ref, buf_ref, slot, seg
        self.tail = None

        def load_prev():
            for s in range(prev_ref.shape[0]):
                carry_ref[slot, s] = prev_ref[s]

        if first is True:
            load_prev()
        else:
            pl.when(first)(load_prev)

    def block(self, u, row0):
        outs, pos = [], 0
        for s, off, n in _segment_spans(row0, u.shape[0], self.seg):
            us = u[pos:pos + n]
            hist = self.carry_ref[self.slot, s] if off == 0 else self.tail
            outs.append(_conv3(us, hist[SUBLANES - 2:SUBLANES - 1], hist[SUBLANES - 1:], self.w))
            self.tail = us[n - SUBLANES:]
            if off + n == self.seg:
                self.carry_ref[self.slot, s] = self.tail
                self.buf_ref[s, self.slot] = self.tail
            pos += n
        return _cat(outs, 0)


def _pad_history(prev):
    pad = [(0, 0)] * prev.ndim
    pad[-2] = (SUBLANES - prev.shape[-2], 0)
    return jnp.pad(prev, pad)


def _unpack_history(buf, rows):
    nb, slots, _, c = buf.shape
    return buf[:, :, SUBLANES - rows:, :].swapaxes(1, 2).reshape(nb, rows, slots * c)


def _attention(q, k_ref, v_ref, row0, seg, head_dim):
    width = q.shape[1]
    scale = head_dim ** -0.5
    spans = _segment_spans(row0, q.shape[0], seg)
    head_cols = [slice(h * head_dim, (h + 1) * head_dim) for h in range(width // head_dim)]
    scores, pos = [], 0
    for s, _, n in spans:
        for cs in head_cols:
            qh = q[pos:pos + n, cs].astype(BF16)
            kh = k_ref[s, :, cs].astype(BF16)
            scores.append(lax.dot_general(qh, kh, (((1,), (1,)), ((), ())),
                                          preferred_element_type=F32) * scale)
        pos += n
    rows = []
    for si, (s, _, n) in enumerate(spans):
        heads = []
        for hi, cs in enumerate(head_cols):
            sc = scores[si * len(head_cols) + hi]
            m = jnp.max(sc, axis=-1, keepdims=True)
            p = jnp.exp(sc - m)
            l = jnp.sum(p, axis=-1, keepdims=True)
            heads.append(_dot(p.astype(BF16), v_ref[s, :, cs].astype(BF16)) / l)
        rows.append(_cat(heads, 1))
    return _cat(rows, 0)


def _pipelined(starts, lead, finish):
    cur = lead(starts[0])
    for k, r0 in enumerate(starts):
        nxt = lead(starts[k + 1]) if k + 1 < len(starts) else None
        finish(r0, cur)
        cur = nxt


def _init_rows(x_ref, g_ref, o_ref, xn_ref, r0, rb):
    x = x_ref[r0:r0 + rb, :]
    xn = _rms(x, g_ref[...]).astype(BF16)
    xn_ref[r0:r0 + rb, :] = xn
    o_ref[r0:r0 + rb, :] = x
    return xn


def _init_tile(x_ref, g_ref, o_ref, xn_ref, rb):
    for r0 in range(0, x_ref.shape[0], rb):
        _init_rows(x_ref, g_ref, o_ref, xn_ref, r0, rb)


def _attention_step(xn_ref, wq_ref, k_ref, v_ref, woh_ref, o_ref, rb, seg, head_dim):
    def lead(r0):
        return _dot(xn_ref[r0:r0 + rb, :], wq_ref[...])

    def finish(r0, q):
        y = _attention(q, k_ref, v_ref, r0, seg, head_dim)
        o_ref[r0:r0 + rb, :] += _dot(y.astype(BF16), woh_ref[...])

    _pipelined(range(0, xn_ref.shape[0], rb), lead, finish)


def _mix_conv_kernel(x_ref, g_ref, wb_ref, wc_ref, wh_ref, wq_ref, cw_ref, prev_ref, k_ref, v_ref,
                     wom_ref, woh_ref, o_ref, buf_ref, xn_ref, carry_ref, *, seg, rb,
                     tiles_per_batch, nc, head_dim):
    i = pl.program_id(0)
    j = pl.program_id(1)
    tm = x_ref.shape[0]

    @pl.when(j == 0)
    def _():
        _init_tile(x_ref, g_ref, o_ref, xn_ref, rb)

    @pl.when(j < nc)
    def _():
        first = True if tiles_per_batch == 1 else (i % tiles_per_batch) == 0
        conv = _RowConv(cw_ref[...], prev_ref, carry_ref, buf_ref, j, first, seg)

        def lead(r0):
            xn = xn_ref[r0:r0 + rb, :]
            return _dot(xn, wc_ref[...]), _dot(xn, wh_ref[...]), _dot(xn, wb_ref[...])

        def finish(r0, chb):
            c, h, b = chb
            y = b * conv.block(c * h, r0)
            o_ref[r0:r0 + rb, :] += _dot(y.astype(BF16), wom_ref[...])

        _pipelined(range(0, tm, rb), lead, finish)

    @pl.when(j >= nc)
    def _():
        _attention_step(xn_ref, wq_ref, k_ref, v_ref, woh_ref, o_ref, rb, seg, head_dim)


def _mix_conv(x, g, w_in, conv_w, prev, mem_k, mem_v, w_out, *, layer, slot, tm, rb, seg, batch_len,
              cw, cwh, head_dim):
    m, d = x.shape
    dm = conv_w.shape[-1]
    nb, n_mem, dq = mem_k.shape[1:]
    nseg = tm // seg
    tpb = batch_len // tm if nseg == 1 else 1
    nc, nh = dm // cw, dq // cwh
    mc = lambda j: jnp.minimum(j, nc - 1)
    hc = lambda j: jnp.maximum(j - nc, 0)
    in_specs = [
        pl.BlockSpec((tm, d), lambda i, j: (i, 0)),
        pl.BlockSpec((None, 1, d), lambda i, j: (layer, 0, 0)),
        pl.BlockSpec((None, d, cw), lambda i, j: (slot, 0, mc(j))),
        pl.BlockSpec((None, d, cw), lambda i, j: (slot, 0, nc + mc(j))),
        pl.BlockSpec((None, d, cw), lambda i, j: (slot, 0, 2 * nc + mc(j))),
        pl.BlockSpec((None, d, cwh), lambda i, j: (slot, 0, 3 * dm // cwh + hc(j))),
        pl.BlockSpec((None, 3, cw), lambda i, j: (slot, 0, mc(j))),
        pl.BlockSpec((None, nseg, SUBLANES, cw), lambda i, j: (slot, i // tpb, 0, mc(j))),
        pl.BlockSpec((None, nseg, n_mem, cwh), lambda i, j: (layer, i // tpb, 0, hc(j))),
        pl.BlockSpec((None, nseg, n_mem, cwh), lambda i, j: (layer, i // tpb, 0, hc(j))),
        pl.BlockSpec((None, cw, d), lambda i, j: (layer, mc(j), 0)),
        pl.BlockSpec((None, cwh, d), lambda i, j: (layer, dm // cwh + hc(j), 0)),
    ]
    out_specs = [
        pl.BlockSpec((tm, d), lambda i, j: (i, 0)),
        pl.BlockSpec((nseg, nc, SUBLANES, cw), lambda i, j: (i // tpb, 0, 0, 0)),
    ]
    scratch = [pltpu.VMEM((tm, d), BF16), pltpu.VMEM((nc, nseg, SUBLANES, cw), F32)]
    est = (4 * tm * d * 4 + tm * d * 2 + 2 * (3 * d * cw + d * cwh + (cw + cwh) * d) * 2
           + 4 * nseg * n_mem * cwh * 4 + 8 * rb * cw * 4)
    x_out, buf = pl.pallas_call(
        functools.partial(_mix_conv_kernel, seg=seg, rb=rb, tiles_per_batch=tpb, nc=nc,
                          head_dim=head_dim),
        grid=(m // tm, nc + nh),
        in_specs=in_specs,
        out_specs=out_specs,
        out_shape=[jax.ShapeDtypeStruct((m, d), F32),
                   jax.ShapeDtypeStruct((nb, nc, SUBLANES, cw), F32)],
        scratch_shapes=scratch,
        compiler_params=_compiler_params(est),
        name="mix_conv",
    )(x, g, w_in, w_in, w_in, w_in, conv_w, _pad_history(prev), mem_k, mem_v, w_out, w_out)
    return x_out, _unpack_history(buf, prev.shape[-2])


def _mix_gmlp_kernel(x_ref, g_ref, wv_ref, wu_ref, wq_ref, lng_ref, lnb_ref, ws_ref, bias_ref, k_ref,
                     v_ref, wom_ref, woh_ref, o_ref, *rest, seg, rb, rblk, nc, head_dim, group_dim,
                     emit_v):
    if emit_v:
        vout_ref, xn_ref, vs_ref = rest
    else:
        vout_ref = None
        xn_ref, vs_ref = rest
    j = pl.program_id(1)
    tm = x_ref.shape[0]
    cw = wv_ref.shape[1]

    @pl.when(j == 0)
    def _():
        _init_tile(x_ref, g_ref, o_ref, xn_ref, rb)

    @pl.when(j < nc)
    def _():
        for r0 in range(0, tm, rb):
            vs_ref[j, r0:r0 + rb, :] = _gelu(_dot(xn_ref[r0:r0 + rb, :], wv_ref[...]))

    @pl.when(j == nc)
    def _():
        width = nc * cw
        for r0 in range(0, tm, rb):
            rows = slice(r0, r0 + rb)
            tot = vs_ref[0, rows, :].sum(axis=-1, keepdims=True)
            for c in range(1, nc):
                tot += vs_ref[c, rows, :].sum(axis=-1, keepdims=True)
            mean = tot / width
            sq = None
            for c in range(nc):
                dv = vs_ref[c, rows, :] - mean
                part = (dv * dv).sum(axis=-1, keepdims=True)
                sq = part if sq is None else sq + part
            rstd = lax.rsqrt(sq / width + EPS)
            for c in range(nc):
                cs = slice(c * cw, (c + 1) * cw)
                vn = (vs_ref[c, rows, :] - mean) * rstd * lng_ref[:, cs] + lnb_ref[:, cs]
                vs_ref[c, rows, :] = vn
                if emit_v:
                    vout_ref[rows, cs] = vn

    @pl.when(jnp.logical_and(j >= nc, j < 2 * nc))
    def _():
        tri = (lax.broadcasted_iota(jnp.int32, (rblk, rblk), 0)
               >= lax.broadcasted_iota(jnp.int32, (rblk, rblk), 1))
        wmats = [jnp.where(tri, ws_ref[gi, :rblk, :rblk], 0.0).astype(BF16)
                 for gi in range(cw // group_dim)]

        def lead(r0):
            u = _dot(xn_ref[r0:r0 + rb, :], wu_ref[...])
            vn = vs_ref[j - nc, r0:r0 + rb, :].astype(BF16)
            cols = []
            for gi, wmat in enumerate(wmats):
                blocks = [_dot(wmat, vn[r:r + rblk, gi * group_dim:(gi + 1) * group_dim])
                          for r in range(0, rb, rblk)]
                cols.append(_cat(blocks, 0))
            return u, _cat(cols, 1)

        def finish(r0, um):
            u, mixed = um
            bias = _cat([_cat([jnp.broadcast_to(bias_ref[gi, :rblk, :], (rblk, group_dim))]
                              * (rb // rblk), 0) for gi in range(len(wmats))], 1)
            y = _gelu(u) * (mixed + bias)
            o_ref[r0:r0 + rb, :] += _dot(y.astype(BF16), wom_ref[...])

        _pipelined(range(0, tm, rb), lead, finish)

    @pl.when(j >= 2 * nc)
    def _():
        _attention_step(xn_ref, wq_ref, k_ref, v_ref, woh_ref, o_ref, rb, seg, head_dim)


def _mix_gmlp(x, g, w_in, ln_g, ln_b, ws, bias, mem_k, mem_v, w_out, *, layer, slot, tm, rb, seg,
              batch_len, cw, cwh, head_dim, emit_v):
    m, d = x.shape
    dm = ln_g.shape[-1]
    nb, n_mem, dq = mem_k.shape[1:]
    groups, chunk = ws.shape[1:3]
    group_dim = dm // groups
    nseg = tm // seg
    tpb = batch_len // tm if nseg == 1 else 1
    rblk = min(chunk, seg)
    assert seg % rblk == 0 and rb % rblk == 0 and cw % group_dim == 0
    nc, nh = dm // cw, dq // cwh
    gpc = cw // group_dim
    uc = lambda j: jnp.clip(j - nc, 0, nc - 1)
    hc = lambda j: jnp.maximum(j - 2 * nc, 0)
    in_specs = [
        pl.BlockSpec((tm, d), lambda i, j: (i, 0)),
        pl.BlockSpec((None, 1, d), lambda i, j: (layer, 0, 0)),
        pl.BlockSpec((None, d, cw), lambda i, j: (slot, 0, nc + jnp.minimum(j, nc - 1))),
        pl.BlockSpec((None, d, cw), lambda i, j: (slot, 0, uc(j))),
        pl.BlockSpec((None, d, cwh), lambda i, j: (slot, 0, 2 * dm // cwh + hc(j))),
        pl.BlockSpec((None, 1, dm), lambda i, j: (slot, 0, 0)),
        pl.BlockSpec((None, 1, dm), lambda i, j: (slot, 0, 0)),
        pl.BlockSpec((None, gpc, chunk, chunk), lambda i, j: (slot, uc(j), 0, 0)),
        pl.BlockSpec((None, gpc, chunk, 1), lambda i, j: (slot, uc(j), 0, 0)),
        pl.BlockSpec((None, nseg, n_mem, cwh), lambda i, j: (layer, i // tpb, 0, hc(j))),
        pl.BlockSpec((None, nseg, n_mem, cwh), lambda i, j: (layer, i // tpb, 0, hc(j))),
        pl.BlockSpec((None, cw, d), lambda i, j: (layer, uc(j), 0)),
        pl.BlockSpec((None, cwh, d), lambda i, j: (layer, dm // cwh + hc(j), 0)),
    ]
    out_specs = [pl.BlockSpec((tm, d), lambda i, j: (i, 0))]
    out_shape = [jax.ShapeDtypeStruct((m, d), F32)]
    if emit_v:
        out_specs.append(pl.BlockSpec((tm, dm), lambda i, j: (i, 0)))
        out_shape.append(jax.ShapeDtypeStruct((m, dm), F32))
    est = (4 * tm * d * 4 + tm * d * 2 + tm * dm * 4 * (3 if emit_v else 1)
           + 2 * (2 * d * cw + d * cwh + (cw + cwh) * d) * 2
           + 4 * nseg * n_mem * cwh * 4 + 8 * rb * cw * 4)
    outs = pl.pallas_call(
        functools.partial(_mix_gmlp_kernel, seg=seg, rb=rb, rblk=rblk, nc=nc, head_dim=head_dim,
                          group_dim=group_dim, emit_v=emit_v),
        grid=(m // tm, 2 * nc + nh),
        in_specs=in_specs,
        out_specs=out_specs,
        out_shape=out_shape,
        scratch_shapes=[pltpu.VMEM((tm, d), BF16), pltpu.VMEM((nc, tm, cw), F32)],
        compiler_params=_compiler_params(est),
        name="mix_gmlp",
    )(x, g, w_in, w_in, w_in, ln_g, ln_b, ws, bias, mem_k, mem_v, w_out, w_out)
    return (outs[0], outs[1]) if emit_v else (outs[0], None)


def _ffn_kernel(x_ref, g_ref, wa_ref, wg_ref, cw_ref, cb_ref, prev_ref, wdn_ref, gf_ref, o_ref,
                buf_ref, xn_ref, carry_ref, *, seg, rb, tiles_per_batch, final_norm):
    i = pl.program_id(0)
    j = pl.program_id(1)
    tm = x_ref.shape[0]
    fc = wdn_ref.shape[0]

    first = True if tiles_per_batch == 1 else (i % tiles_per_batch) == 0

    def step(new_tile):
        conv = _RowConv(cw_ref[...], prev_ref, carry_ref, buf_ref, j, first, seg)

        def lead(r0):
            xn = (_init_rows(x_ref, g_ref, o_ref, xn_ref, r0, rb) if new_tile
                  else xn_ref[r0:r0 + rb, :])
            return jnp.concatenate([_dot(xn, wa_ref[...]), _dot(xn, wg_ref[...])], axis=1)

        def finish(r0, up):
            z = conv.block(up, r0) + cb_ref[...]
            h = _silu(z[:, fc:]) * z[:, :fc]
            o_ref[r0:r0 + rb, :] += _dot(h.astype(BF16), wdn_ref[...])

        _pipelined(range(0, tm, rb), lead, finish)

    pl.when(j == 0)(functools.partial(step, True))
    pl.when(j > 0)(functools.partial(step, False))

    if final_norm:
        @pl.when(j == pl.num_programs(1) - 1)
        def _():
            for r0 in range(0, tm, rb):
                o_ref[r0:r0 + rb, :] = _rms(o_ref[r0:r0 + rb, :], gf_ref[...])


def _ffn(x, g, w_up, conv_w, conv_b, prev, w_down, g_final, *, layer, tm, rb, seg, batch_len, fc,
         final_norm):
    m, d = x.shape
    dff = w_down.shape[1]
    nj = dff // fc
    nseg = tm // seg
    tpb = batch_len // tm if nseg == 1 else 1
    nb = prev.shape[1]
    in_specs = [
        pl.BlockSpec((tm, d), lambda i, j: (i, 0)),
        pl.BlockSpec((None, 1, d), lambda i, j: (layer, 0, 0)),
        pl.BlockSpec((None, d, fc), lambda i, j: (layer, 0, j)),
        pl.BlockSpec((None, d, fc), lambda i, j: (layer, 0, nj + j)),
        pl.BlockSpec((None, 3, 2 * fc), lambda i, j: (layer, 0, j)),
        pl.BlockSpec((None, 1, 2 * fc), lambda i, j: (layer, 0, j)),
        pl.BlockSpec((None, nseg, SUBLANES, 2 * fc), lambda i, j: (layer, i // tpb, 0, j)),
        pl.BlockSpec((None, fc, d), lambda i, j: (layer, j, 0)),
        pl.BlockSpec((1, d), lambda i, j: (0, 0)),
    ]
    out_specs = [
        pl.BlockSpec((tm, d), lambda i, j: (i, 0)),
        pl.BlockSpec((nseg, nj, SUBLANES, 2 * fc), lambda i, j: (i // tpb, 0, 0, 0)),
    ]
    scratch = [pltpu.VMEM((tm, d), BF16), pltpu.VMEM((nj, nseg, SUBLANES, 2 * fc), F32)]
    est = 4 * tm * d * 4 + tm * d * 2 + 2 * (3 * d * fc) * 2 + 8 * rb * 2 * fc * 4
    x_out, buf = pl.pallas_call(
        functools.partial(_ffn_kernel, seg=seg, rb=rb, tiles_per_batch=tpb, final_norm=final_norm),
        grid=(m // tm, nj),
        in_specs=in_specs,
        out_specs=out_specs,
        out_shape=[jax.ShapeDtypeStruct((m, d), F32),
                   jax.ShapeDtypeStruct((nb, nj, SUBLANES, 2 * fc), F32)],
        scratch_shapes=scratch,
        compiler_params=_compiler_params(est),
        name="conv_ffn",
    )(x, g, w_up, w_up, conv_w, conv_b, _pad_history(prev), w_down, g_final)
    return x_out, _unpack_history(buf, prev.shape[-2])


def _memkv_kernel(m_ref, g_ref, wk_ref, wv_ref, ko_ref, vo_ref):
    mn = _rms(m_ref[...], g_ref[0]).astype(BF16)
    ko_ref[0] = _dot(mn, wk_ref[0])
    vo_ref[0] = _dot(mn, wv_ref[0])


def _memkv(mem, g, wk, wv, *, tn):
    rows, d = mem.shape
    depth, _, dq = wk.shape
    shape = jax.ShapeDtypeStruct((depth, rows, dq), F32)
    return pl.pallas_call(
        _memkv_kernel,
        grid=(depth, dq // tn),
        in_specs=[
            pl.BlockSpec((rows, d), lambda l, n: (0, 0)),
            pl.BlockSpec((1, 1, d), lambda l, n: (l, 0, 0)),
            pl.BlockSpec((1, d, tn), lambda l, n: (l, 0, n)),
            pl.BlockSpec((1, d, tn), lambda l, n: (l, 0, n)),
        ],
        out_specs=[pl.BlockSpec((1, rows, tn), lambda l, n: (l, 0, n))] * 2,
        out_shape=[shape, shape],
        compiler_params=pltpu.CompilerParams(dimension_semantics=("arbitrary", "arbitrary")),
        name="memory_kv",
    )(mem, g, wk, wv)


def _group_cols(t, nj, fc):
    lead = t.shape[:-1]
    return t.reshape(lead + (2, nj, fc)).swapaxes(-3, -2).reshape(lead + (2 * nj * fc,))


def _ungroup_cols(t, nj, fc):
    lead = t.shape[:-1]
    return t.reshape(lead + (nj, 2, fc)).swapaxes(-3, -2).reshape(lead + (2 * nj * fc,))


def _tiling(t, tm, rb):
    return dict(tm=tm, rb=rb, seg=min(t, tm), batch_len=t)


def _trunk(x3, mem_k, mem_v, conv_a_prev, ffn_prev, p, *, tiles, emit_v):
    nb, t, d = x3.shape
    x = x3.reshape(nb * t, d)
    depth = p["norm_mix_g"].shape[0]
    conv_a_new, ffn_new, gmlp_v = [], [], []
    mixer = dict(head_dim=p["head_dim"])
    for i in range(depth):
        kind, jx = i % 2, i // 2
        if kind == 0:
            x, buf = _mix_conv(x, p["norm_mix_g"], p["w_in_a"], p["conv_a_w"], conv_a_prev, mem_k,
                               mem_v, p["w_out"], layer=i, slot=jx, cw=p["cw_conv"], cwh=p["cwh_conv"], **mixer,
                               **_tiling(t, *tiles["mix_conv"]))
            conv_a_new.append(buf)
        else:
            x, v_rows = _mix_gmlp(x, p["norm_mix_g"], p["w_in_b"], p["gmlp_norm_g"], p["gmlp_norm_b"],
                                  p["gmlp_ws"], p["gmlp_bias"], mem_k, mem_v, p["w_out"], layer=i,
                                  slot=jx, emit_v=emit_v, cw=p["cw_gmlp"], cwh=p["cwh_gmlp"], **mixer,
                                  **_tiling(t, *tiles["mix_gmlp"]))
            gmlp_v.append(v_rows)
        x, buf = _ffn(x, p["norm_ffn_g"], p["w_up"], p["ffn_conv_w"], p["ffn_conv_b"], ffn_prev,
                      p["w_down"], p["norm_final_g"], layer=i, fc=p["fc"],
                      final_norm=(i == depth - 1), **_tiling(t, *tiles["ffn"]))
        ffn_new.append(_ungroup_cols(buf, p["nj"], p["fc"]))
    y = x.reshape(nb, t, d)
    v_out = jnp.stack([v.reshape(nb, t, -1) for v in gmlp_v]) if emit_v else None
    return y, jnp.stack(conv_a_new), jnp.stack(ffn_new), v_out


def kernel(x_prompt, x_sample, mem_prompt, cache_conv_a, cache_ffn_conv, cache_mem_k, cache_mem_v,
           norm_mix_g, norm_mem_g, w_mem_k, w_mem_v, w_in_a, conv_a_w, w_in_b, gmlp_norm_g,
           gmlp_norm_b, gmlp_ws, gmlp_bias, w_out, norm_ffn_g, w_up, ffn_conv_w, ffn_conv_b, w_down,
           norm_final_g):
    b, s, d = x_prompt.shape
    nb_s, t_s, _ = x_sample.shape
    depth = norm_mix_g.shape[0]
    n_mem = mem_prompt.shape[1]
    heads, head_dim = cache_mem_k.shape[-2:]
    dq = heads * head_dim
    dff = w_down.shape[1]
    fc = 512
    assert dff % fc == 0
    nj = dff // fc

    p = dict(
        norm_mix_g=norm_mix_g[:, None, :], norm_ffn_g=norm_ffn_g[:, None, :],
        norm_final_g=norm_final_g[None],
        w_in_a=w_in_a.astype(BF16), w_in_b=w_in_b.astype(BF16), w_out=w_out.astype(BF16),
        conv_a_w=conv_a_w, gmlp_norm_g=gmlp_norm_g[:, None, :], gmlp_norm_b=gmlp_norm_b[:, None, :],
        gmlp_ws=gmlp_ws, gmlp_bias=gmlp_bias[..., None],
        w_up=w_up.astype(BF16), w_down=w_down.astype(BF16),
        ffn_conv_w=_group_cols(ffn_conv_w, nj, fc),
        ffn_conv_b=_group_cols(ffn_conv_b, nj, fc)[:, None, :],
        fc=fc, nj=nj, cw_conv=256, cw_gmlp=512, cwh_conv=512, cwh_gmlp=512, head_dim=head_dim,
    )

    mk, mv = _memkv(mem_prompt.reshape(b * n_mem, d), norm_mem_g[:, None, :],
                    w_mem_k.astype(BF16), w_mem_v.astype(BF16), tn=512)
    mk = mk.reshape(depth, b, n_mem, dq)
    mv = mv.reshape(depth, b, n_mem, dq)
    conv_a_zero = jnp.zeros((cache_conv_a.shape[0], b) + cache_conv_a.shape[2:], F32)
    ffn_zero = jnp.zeros((depth, b) + cache_ffn_conv.shape[2:], F32)
    y_prompt, conv_a_prompt, ffn_conv_prompt, _ = _trunk(
        x_prompt, mk, mv, conv_a_zero, ffn_zero, p, emit_v=False,
        tiles=dict(mix_conv=(1024, 512), mix_gmlp=(512, 256), ffn=(1024, 512)))

    y_sample, conv_a_sample, ffn_conv_sample, gmlp_v_sample = _trunk(
        x_sample, cache_mem_k.reshape(depth, nb_s, n_mem, dq),
        cache_mem_v.reshape(depth, nb_s, n_mem, dq), cache_conv_a,
        _group_cols(cache_ffn_conv, nj, fc), p, emit_v=True,
        tiles=dict.fromkeys(("mix_conv", "mix_gmlp", "ffn"), (nb_s * t_s, nb_s * t_s)))

    return (y_prompt, y_sample, conv_a_prompt, ffn_conv_prompt,
            mk.reshape(depth, b, n_mem, heads, head_dim), mv.reshape(depth, b, n_mem, heads, head_dim),
            conv_a_sample, ffn_conv_sample, gmlp_v_sample)
```

```python
import functools

import jax
import jax.numpy as jnp
from jax import lax
from jax.experimental import pallas as pl
from jax.experimental.pallas import tpu as pltpu

EPS = 1e-6
SUBLANES = 8
VMEM_REQUEST_CAP = 62 * 1024 * 1024

F32 = jnp.float32
BF16 = jnp.bfloat16


def _vmem_limit(est_bytes):
    return int(min(VMEM_REQUEST_CAP, max(32 * 1024 * 1024, est_bytes * 5 // 4)))


def _compiler_params(est_bytes):
    return pltpu.CompilerParams(dimension_semantics=("arbitrary", "arbitrary"),
                                vmem_limit_bytes=_vmem_limit(est_bytes))


def _dot(a, b):
    return jnp.dot(a, b, preferred_element_type=F32)


def _rms(x, g):
    return x * lax.rsqrt(jnp.mean(x * x, axis=-1, keepdims=True) + EPS) * g


def _gelu(x):
    return 0.5 * x * (1.0 + lax.erf(x * (2.0 ** -0.5)))


def _silu(x):
    return (0.5 * x) * (1.0 + jnp.tanh(0.5 * x))


def _cat(parts, axis):
    return parts[0] if len(parts) == 1 else jnp.concatenate(parts, axis=axis)


def _segment_spans(row0, nrows, seg):
    spans, pos = [], row0
    while pos < row0 + nrows:
        s, off = divmod(pos, seg)
        n = min(seg - off, row0 + nrows - pos)
        spans.append((s, off, n))
        pos += n
    return spans


def _conv3(u, h0, h1, w):
    s, c = u.shape
    r1 = pltpu.roll(u, 1, axis=0)
    r2 = pltpu.roll(u, 2, axis=0)
    row = lax.broadcasted_iota(jnp.int32, (SUBLANES, c), 0)
    top1 = jnp.where(row == 0, h1, r1[:SUBLANES])
    top2 = jnp.where(row == 0, h0, jnp.where(row == 1, h1, r2[:SUBLANES]))
    if s > SUBLANES:
        r1 = jnp.concatenate([top1, r1[SUBLANES:]], axis=0)
        r2 = jnp.concatenate([top2, r2[SUBLANES:]], axis=0)
    else:
        r1, r2 = top1, top2
    return w[0:1] * r2 + w[1:2] * r1 + w[2:3] * u


class _RowConv:
    def __init__(self, w, prev_ref, carry_ref, buf_ref, slot, first, seg):
        self.w, self.carry_ref, self.buf_ref, self.slot, self.seg = w, carry_ref, buf_ref, slot, seg
        self.tail = None

        def load_prev():
            for s in range(prev_ref.shape[0]):
                carry_ref[slot, s] = prev_ref[s]

        if first is True:
            load_prev()
        else:
            pl.when(first)(load_prev)

    def block(self, u, row0):
        outs, pos = [], 0
        for s, off, n in _segment_spans(row0, u.shape[0], self.seg):
            us = u[pos:pos + n]
            hist = self.carry_ref[self.slot, s] if off == 0 else self.tail
            outs.append(_conv3(us, hist[SUBLANES - 2:SUBLANES - 1], hist[SUBLANES - 1:], self.w))
            self.tail = us[n - SUBLANES:]
            if off + n == self.seg:
                self.carry_ref[self.slot, s] = self.tail
                self.buf_ref[s, self.slot] = self.tail
            pos += n
        return _cat(outs, 0)


def _pad_history(prev):
    pad = [(0, 0)] * prev.ndim
    pad[-2] = (SUBLANES - prev.shape[-2], 0)
    return jnp.pad(prev, pad)


def _unpack_history(buf, rows):
    nb, slots, _, c = buf.shape
    return buf[:, :, SUBLANES - rows:, :].swapaxes(1, 2).reshape(nb, rows, slots * c)


def _attention(q, k_ref, v_ref, row0, seg, head_dim):
    width = q.shape[1]
    scale = head_dim ** -0.5
    spans = _segment_spans(row0, q.shape[0], seg)
    head_cols = [slice(h * head_dim, (h + 1) * head_dim) for h in range(width // head_dim)]
    scores, pos = [], 0
    for s, _, n in spans:
        for cs in head_cols:
            qh = q[pos:pos + n, cs].astype(BF16)
            kh = k_ref[s, :, cs].astype(BF16)
            scores.append(lax.dot_general(qh, kh, (((1,), (1,)), ((), ())),
                                          preferred_element_type=F32) * scale)
        pos += n
    rows = []
    for si, (s, _, n) in enumerate(spans):
        heads = []
        for hi, cs in enumerate(head_cols):
            sc = scores[si * len(head_cols) + hi]
            m = jnp.max(sc, axis=-1, keepdims=True)
            p = jnp.exp(sc - m)
            l = jnp.sum(p, axis=-1, keepdims=True)
            heads.append(_dot(p.astype(BF16), v_ref[s, :, cs].astype(BF16)) / l)
        rows.append(_cat(heads, 1))
    return _cat(rows, 0)


def _pipelined(starts, lead, finish):
    cur = lead(starts[0])
    for k, r0 in enumerate(starts):
        nxt = lead(starts[k + 1]) if k + 1 < len(starts) else None
        finish(r0, cur)
        cur = nxt


def _init_rows(x_ref, g_ref, o_ref, xn_ref, r0, rb):
    x = x_ref[r0:r0 + rb, :]
    xn = _rms(x, g_ref[...]).astype(BF16)
    xn_ref[r0:r0 + rb, :] = xn
    o_ref[r0:r0 + rb, :] = x
    return xn


def _init_tile(x_ref, g_ref, o_ref, xn_ref, rb):
    for r0 in range(0, x_ref.shape[0], rb):
        _init_rows(x_ref, g_ref, o_ref, xn_ref, r0, rb)


def _attention_step(xn_ref, wq_ref, k_ref, v_ref, woh_ref, o_ref, rb, seg, head_dim):
    def lead(r0):
        return _dot(xn_ref[r0:r0 + rb, :], wq_ref[...])

    def finish(r0, q):
        y = _attention(q, k_ref, v_ref, r0, seg, head_dim)
        o_ref[r0:r0 + rb, :] += _dot(y.astype(BF16), woh_ref[...])

    _pipelined(range(0, xn_ref.shape[0], rb), lead, finish)


def _mix_conv_kernel(x_ref, g_ref, wb_ref, wc_ref, wh_ref, wq_ref, cw_ref, prev_ref, k_ref, v_ref,
                     wom_ref, woh_ref, o_ref, buf_ref, xn_ref, carry_ref, *, seg, rb,
                     tiles_per_batch, nc, head_dim):
    i = pl.program_id(0)
    j = pl.program_id(1)
    tm = x_ref.shape[0]

    @pl.when(j == 0)
    def _():
        _init_tile(x_ref, g_ref, o_ref, xn_ref, rb)

    @pl.when(j < nc)
    def _():
        first = True if tiles_per_batch == 1 else (i % tiles_per_batch) == 0
        conv = _RowConv(cw_ref[...], prev_ref, carry_ref, buf_ref, j, first, seg)

        def lead(r0):
            xn = xn_ref[r0:r0 + rb, :]
            return _dot(xn, wc_ref[...]), _dot(xn, wh_ref[...]), _dot(xn, wb_ref[...])

        def finish(r0, chb):
            c, h, b = chb
            y = b * conv.block(c * h, r0)
            o_ref[r0:r0 + rb, :] += _dot(y.astype(BF16), wom_ref[...])

        _pipelined(range(0, tm, rb), lead, finish)

    @pl.when(j >= nc)
    def _():
        _attention_step(xn_ref, wq_ref, k_ref, v_ref, woh_ref, o_ref, rb, seg, head_dim)


def _mix_conv(x, g, w_in, conv_w, prev, mem_k, mem_v, w_out, *, layer, slot, tm, rb, seg, batch_len,
              cw, cwh, head_dim):
    m, d = x.shape
    dm = conv_w.shape[-1]
    nb, n_mem, dq = mem_k.shape[1:]
    nseg = tm // seg
    tpb = batch_len // tm if nseg == 1 else 1
    nc, nh = dm // cw, dq // cwh
    mc = lambda j: jnp.minimum(j, nc - 1)
    hc = lambda j: jnp.maximum(j - nc, 0)
    in_specs = [
        pl.BlockSpec((tm, d), lambda i, j: (i, 0)),
        pl.BlockSpec((None, 1, d), lambda i, j: (layer, 0, 0)),
        pl.BlockSpec((None, d, cw), lambda i, j: (slot, 0, mc(j))),
        pl.BlockSpec((None, d, cw), lambda i, j: (slot, 0, nc + mc(j))),
        pl.BlockSpec((None, d, cw), lambda i, j: (slot, 0, 2 * nc + mc(j))),
        pl.BlockSpec((None, d, cwh), lambda i, j: (slot, 0, 3 * dm // cwh + hc(j))),
        pl.BlockSpec((None, 3, cw), lambda i, j: (slot, 0, mc(j))),
        pl.BlockSpec((None, nseg, SUBLANES, cw), lambda i, j: (slot, i // tpb, 0, mc(j))),
        pl.BlockSpec((None, nseg, n_mem, cwh), lambda i, j: (layer, i // tpb, 0, hc(j))),
        pl.BlockSpec((None, nseg, n_mem, cwh), lambda i, j: (layer, i // tpb, 0, hc(j))),
        pl.BlockSpec((None, cw, d), lambda i, j: (layer, mc(j), 0)),
        pl.BlockSpec((None, cwh, d), lambda i, j: (layer, dm // cwh + hc(j), 0)),
    ]
    out_specs = [
        pl.BlockSpec((tm, d), lambda i, j: (i, 0)),
        pl.BlockSpec((nseg, nc, SUBLANES, cw), lambda i, j: (i // tpb, 0, 0, 0)),
    ]
    scratch = [pltpu.VMEM((tm, d), BF16), pltpu.VMEM((nc, nseg, SUBLANES, cw), F32)]
    est = (4 * tm * d * 4 + tm * d * 2 + 2 * (3 * d * cw + d * cwh + (cw + cwh) * d) * 2
           + 4 * nseg * n_mem * cwh * 4 + 8 * rb * cw * 4)
    x_out, buf = pl.pallas_call(
        functools.partial(_mix_conv_kernel, seg=seg, rb=rb, tiles_per_batch=tpb, nc=nc,
                          head_dim=head_dim),
        grid=(m // tm, nc + nh),
        in_specs=in_specs,
        out_specs=out_specs,
        out_shape=[jax.ShapeDtypeStruct((m, d), F32),
                   jax.ShapeDtypeStruct((nb, nc, SUBLANES, cw), F32)],
        scratch_shapes=scratch,
        compiler_params=_compiler_params(est),
        name="mix_conv",
    )(x, g, w_in, w_in, w_in, w_in, conv_w, _pad_history(prev), mem_k, mem_v, w_out, w_out)
    return x_out, _unpack_history(buf, prev.shape[-2])


def _mix_gmlp_kernel(x_ref, g_ref, wv_ref, wu_ref, wq_ref, lng_ref, lnb_ref, ws_ref, bias_ref, k_ref,
                     v_ref, wom_ref, woh_ref, o_ref, *rest, seg, rb, rblk, nc, head_dim, group_dim,
                     emit_v):
    if emit_v:
        vout_ref, xn_ref, vs_ref = rest
    else:
        vout_ref = None
        xn_ref, vs_ref = rest
    j = pl.program_id(1)
    tm = x_ref.shape[0]
    cw = wv_ref.shape[1]

    @pl.when(j == 0)
    def _():
        _init_tile(x_ref, g_ref, o_ref, xn_ref, rb)

    @pl.when(j < nc)
    def _():
        for r0 in range(0, tm, rb):
            vs_ref[j, r0:r0 + rb, :] = _gelu(_dot(xn_ref[r0:r0 + rb, :], wv_ref[...]))

    @pl.when(j == nc)
    def _():
        width = nc * cw
        for r0 in range(0, tm, rb):
            rows = slice(r0, r0 + rb)
            tot = vs_ref[0, rows, :].sum(axis=-1, keepdims=True)
            for c in range(1, nc):
                tot += vs_ref[c, rows, :].sum(axis=-1, keepdims=True)
            mean = tot / width
            sq = None
            for c in range(nc):
                dv = vs_ref[c, rows, :] - mean
                part = (dv * dv).sum(axis=-1, keepdims=True)
                sq = part if sq is None else sq + part
            rstd = lax.rsqrt(sq / width + EPS)
            for c in range(nc):
                cs = slice(c * cw, (c + 1) * cw)
                vn = (vs_ref[c, rows, :] - mean) * rstd * lng_ref[:, cs] + lnb_ref[:, cs]
                vs_ref[c, rows, :] = vn
                if emit_v:
                    vout_ref[rows, cs] = vn

    @pl.when(jnp.logical_and(j >= nc, j < 2 * nc))
    def _():
        tri = (lax.broadcasted_iota(jnp.int32, (rblk, rblk), 0)
               >= lax.broadcasted_iota(jnp.int32, (rblk, rblk), 1))
        wmats = [jnp.where(tri, ws_ref[gi, :rblk, :rblk], 0.0).astype(BF16)
                 for gi in range(cw // group_dim)]

        def lead(r0):
            u = _dot(xn_ref[r0:r0 + rb, :], wu_ref[...])
            vn = vs_ref[j - nc, r0:r0 + rb, :].astype(BF16)
            cols = []
            for gi, wmat in enumerate(wmats):
                blocks = [_dot(wmat, vn[r:r + rblk, gi * group_dim:(gi + 1) * group_dim])
                          for r in range(0, rb, rblk)]
                cols.append(_cat(blocks, 0))
            return u, _cat(cols, 1)

        def finish(r0, um):
            u, mixed = um
            bias = _cat([_cat([jnp.broadcast_to(bias_ref[gi, :rblk, :], (rblk, group_dim))]
                              * (rb // rblk), 0) for gi in range(len(wmats))], 1)
            y = _gelu(u) * (mixed + bias)
            o_ref[r0:r0 + rb, :] += _dot(y.astype(BF16), wom_ref[...])

        _pipelined(range(0, tm, rb), lead, finish)

    @pl.when(j >= 2 * nc)
    def _():
        _attention_step(xn_ref, wq_ref, k_ref, v_ref, woh_ref, o_ref, rb, seg, head_dim)


def _mix_gmlp(x, g, w_in, ln_g, ln_b, ws, bias, mem_k, mem_v, w_out, *, layer, slot, tm, rb, seg,
              batch_len, cw, cwh, head_dim, emit_v):
    m, d = x.shape
    dm = ln_g.shape[-1]
    nb, n_mem, dq = mem_k.shape[1:]
    groups, chunk = ws.shape[1:3]
    group_dim = dm // groups
    nseg = tm // seg
    tpb = batch_len // tm if nseg == 1 else 1
    rblk = min(chunk, seg)
    assert seg % rblk == 0 and rb % rblk == 0 and cw % group_dim == 0
    nc, nh = dm // cw, dq // cwh
    gpc = cw // group_dim
    uc = lambda j: jnp.clip(j - nc, 0, nc - 1)
    hc = lambda j: jnp.maximum(j - 2 * nc, 0)
    in_specs = [
        pl.BlockSpec((tm, d), lambda i, j: (i, 0)),
        pl.BlockSpec((None, 1, d), lambda i, j: (layer, 0, 0)),
        pl.BlockSpec((None, d, cw), lambda i, j: (slot, 0, nc + jnp.minimum(j, nc - 1))),
        pl.BlockSpec((None, d, cw), lambda i, j: (slot, 0, uc(j))),
        pl.BlockSpec((None, d, cwh), lambda i, j: (slot, 0, 2 * dm // cwh + hc(j))),
        pl.BlockSpec((None, 1, dm), lambda i, j: (slot, 0, 0)),
        pl.BlockSpec((None, 1, dm), lambda i, j: (slot, 0, 0)),
        pl.BlockSpec((None, gpc, chunk, chunk), lambda i, j: (slot, uc(j), 0, 0)),
        pl.BlockSpec((None, gpc, chunk, 1), lambda i, j: (slot, uc(j), 0, 0)),
        pl.BlockSpec((None, nseg, n_mem, cwh), lambda i, j: (layer, i // tpb, 0, hc(j))),
        pl.BlockSpec((None, nseg, n_mem, cwh), lambda i, j: (layer, i // tpb, 0, hc(j))),
        pl.BlockSpec((None, cw, d), lambda i, j: (layer, uc(j), 0)),
        pl.BlockSpec((None, cwh, d), lambda i, j: (layer, dm // cwh + hc(j), 0)),
    ]
    out_specs = [pl.BlockSpec((tm, d), lambda i, j: (i, 0))]
    out_shape = [jax.ShapeDtypeStruct((m, d), F32)]
    if emit_v:
        out_specs.append(pl.BlockSpec((tm, dm), lambda i, j: (i, 0)))
        out_shape.append(jax.ShapeDtypeStruct((m, dm), F32))
    est = (4 * tm * d * 4 + tm * d * 2 + tm * dm * 4 * (3 if emit_v else 1)
           + 2 * (2 * d * cw + d * cwh + (cw + cwh) * d) * 2
           + 4 * nseg * n_mem * cwh * 4 + 8 * rb * cw * 4)
    outs = pl.pallas_call(
        functools.partial(_mix_gmlp_kernel, seg=seg, rb=rb, rblk=rblk, nc=nc, head_dim=head_dim,
                          group_dim=group_dim, emit_v=emit_v),
        grid=(m // tm, 2 * nc + nh),
        in_specs=in_specs,
        out_specs=out_specs,
        out_shape=out_shape,
        scratch_shapes=[pltpu.VMEM((tm, d), BF16), pltpu.VMEM((nc, tm, cw), F32)],
        compiler_params=_compiler_params(est),
        name="mix_gmlp",
    )(x, g, w_in, w_in, w_in, ln_g, ln_b, ws, bias, mem_k, mem_v, w_out, w_out)
    return (outs[0], outs[1]) if emit_v else (outs[0], None)


def _ffn_kernel(x_ref, g_ref, wa_ref, wg_ref, cw_ref, cb_ref, prev_ref, wdn_ref, gf_ref, o_ref,
                buf_ref, xn_ref, carry_ref, *, seg, rb, tiles_per_batch, final_norm):
    i = pl.program_id(0)
    j = pl.program_id(1)
    tm = x_ref.shape[0]
    fc = wdn_ref.shape[0]

    first = True if tiles_per_batch == 1 else (i % tiles_per_batch) == 0

    def step(new_tile):
        conv = _RowConv(cw_ref[...], prev_ref, carry_ref, buf_ref, j, first, seg)

        def lead(r0):
            xn = (_init_rows(x_ref, g_ref, o_ref, xn_ref, r0, rb) if new_tile
                  else xn_ref[r0:r0 + rb, :])
            return jnp.concatenate([_dot(xn, wa_ref[...]), _dot(xn, wg_ref[...])], axis=1)

        def finish(r0, up):
            z = conv.block(up, r0) + cb_ref[...]
            h = _silu(z[:, fc:]) * z[:, :fc]
            o_ref[r0:r0 + rb, :] += _dot(h.astype(BF16), wdn_ref[...])

        _pipelined(range(0, tm, rb), lead, finish)

    pl.when(j == 0)(functools.partial(step, True))
    pl.when(j > 0)(functools.partial(step, False))

    if final_norm:
        @pl.when(j == pl.num_programs(1) - 1)
        def _():
            for r0 in range(0, tm, rb):
                o_ref[r0:r0 + rb, :] = _rms(o_ref[r0:r0 + rb, :], gf_ref[...])


def _ffn(x, g, w_up, conv_w, conv_b, prev, w_down, g_final, *, layer, tm, rb, seg, batch_len, fc,
         final_norm):
    m, d = x.shape
    dff = w_down.shape[1]
    nj = dff // fc
    nseg = tm // seg
    tpb = batch_len // tm if nseg == 1 else 1
    nb = prev.shape[1]
    in_specs = [
        pl.BlockSpec((tm, d), lambda i, j: (i, 0)),
        pl.BlockSpec((None, 1, d), lambda i, j: (layer, 0, 0)),
        pl.BlockSpec((None, d, fc), lambda i, j: (layer, 0, j)),
        pl.BlockSpec((None, d, fc), lambda i, j: (layer, 0, nj + j)),
        pl.BlockSpec((None, 3, 2 * fc), lambda i, j: (layer, 0, j)),
        pl.BlockSpec((None, 1, 2 * fc), lambda i, j: (layer, 0, j)),
        pl.BlockSpec((None, nseg, SUBLANES, 2 * fc), lambda i, j: (layer, i // tpb, 0, j)),
        pl.BlockSpec((None, fc, d), lambda i, j: (layer, j, 0)),
        pl.BlockSpec((1, d), lambda i, j: (0, 0)),
    ]
    out_specs = [
        pl.BlockSpec((tm, d), lambda i, j: (i, 0)),
        pl.BlockSpec((nseg, nj, SUBLANES, 2 * fc), lambda i, j: (i // tpb, 0, 0, 0)),
    ]
    scratch = [pltpu.VMEM((tm, d), BF16), pltpu.VMEM((nj, nseg, SUBLANES, 2 * fc), F32)]
    est = 4 * tm * d * 4 + tm * d * 2 + 2 * (3 * d * fc) * 2 + 8 * rb * 2 * fc * 4
    x_out, buf = pl.pallas_call(
        functools.partial(_ffn_kernel, seg=seg, rb=rb, tiles_per_batch=tpb, final_norm=final_norm),
        grid=(m // tm, nj),
        in_specs=in_specs,
        out_specs=out_specs,
        out_shape=[jax.ShapeDtypeStruct((m, d), F32),
                   jax.ShapeDtypeStruct((nb, nj, SUBLANES, 2 * fc), F32)],
        scratch_shapes=scratch,
        compiler_params=_compiler_params(est),
        name="conv_ffn",
    )(x, g, w_up, w_up, conv_w, conv_b, _pad_history(prev), w_down, g_final)
    return x_out, _unpack_history(buf, prev.shape[-2])


def _memkv_kernel(m_ref, g_ref, wk_ref, wv_ref, ko_ref, vo_ref):
    mn = _rms(m_ref[...], g_ref[0]).astype(BF16)
    ko_ref[0] = _dot(mn, wk_ref[0])
    vo_ref[0] = _dot(mn, wv_ref[0])


def _memkv(mem, g, wk, wv, *, tn):
    rows, d = mem.shape
    depth, _, dq = wk.shape
    shape = jax.ShapeDtypeStruct((depth, rows, dq), F32)
    return pl.pallas_call(
        _memkv_kernel,
        grid=(depth, dq // tn),
        in_specs=[
            pl.BlockSpec((rows, d), lambda l, n: (0, 0)),
            pl.BlockSpec((1, 1, d), lambda l, n: (l, 0, 0)),
            pl.BlockSpec((1, d, tn), lambda l, n: (l, 0, n)),
            pl.BlockSpec((1, d, tn), lambda l, n: (l, 0, n)),
        ],
        out_specs=[pl.BlockSpec((1, rows, tn), lambda l, n: (l, 0, n))] * 2,
        out_shape=[shape, shape],
        compiler_params=pltpu.CompilerParams(dimension_semantics=("arbitrary", "arbitrary")),
        name="memory_kv",
    )(mem, g, wk, wv)


def _group_cols(t, nj, fc):
    lead = t.shape[:-1]
    return t.reshape(lead + (2, nj, fc)).swapaxes(-3, -2).reshape(lead + (2 * nj * fc,))


def _ungroup_cols(t, nj, fc):
    lead = t.shape[:-1]
    return t.reshape(lead + (nj, 2, fc)).swapaxes(-3, -2).reshape(lead + (2 * nj * fc,))


def _tiling(t, tm, rb):
    return dict(tm=tm, rb=rb, seg=min(t, tm), batch_len=t)


def _trunk(x3, mem_k, mem_v, conv_a_prev, ffn_prev, p, *, tiles, emit_v):
    nb, t, d = x3.shape
    x = x3.reshape(nb * t, d)
    depth = p["norm_mix_g"].shape[0]
    conv_a_new, ffn_new, gmlp_v = [], [], []
    mixer = dict(head_dim=p["head_dim"])
    for i in range(depth):
        kind, jx = i % 2, i // 2
        if kind == 0:
            x, buf = _mix_conv(x, p["norm_mix_g"], p["w_in_a"], p["conv_a_w"], conv_a_prev, mem_k,
                               mem_v, p["w_out"], layer=i, slot=jx, cw=p["cw_conv"], cwh=p["cwh_conv"], **mixer,
                               **_tiling(t, *tiles["mix_conv"]))
            conv_a_new.append(buf)
        else:
            x, v_rows = _mix_gmlp(x, p["norm_mix_g"], p["w_in_b"], p["gmlp_norm_g"], p["gmlp_norm_b"],
                                  p["gmlp_ws"], p["gmlp_bias"], mem_k, mem_v, p["w_out"], layer=i,
                                  slot=jx, emit_v=emit_v, cw=p["cw_gmlp"], cwh=p["cwh_gmlp"], **mixer,
                                  **_tiling(t, *tiles["mix_gmlp"]))
            gmlp_v.append(v_rows)
        x, buf = _ffn(x, p["norm_ffn_g"], p["w_up"], p["ffn_conv_w"], p["ffn_conv_b"], ffn_prev,
                      p["w_down"], p["norm_final_g"], layer=i, fc=p["fc"],
                      final_norm=(i == depth - 1), **_tiling(t, *tiles["ffn"]))
        ffn_new.append(_ungroup_cols(buf, p["nj"], p["fc"]))
    y = x.reshape(nb, t, d)
    v_out = jnp.stack([v.reshape(nb, t, -1) for v in gmlp_v]) if emit_v else None
    return y, jnp.stack(conv_a_new), jnp.stack(ffn_new), v_out


def kernel(x_prompt, x_sample, mem_prompt, cache_conv_a, cache_ffn_conv, cache_mem_k, cache_mem_v,
           norm_mix_g, norm_mem_g, w_mem_k, w_mem_v, w_in_a, conv_a_w, w_in_b, gmlp_norm_g,
           gmlp_norm_b, gmlp_ws, gmlp_bias, w_out, norm_ffn_g, w_up, ffn_conv_w, ffn_conv_b, w_down,
           norm_final_g):
    b, s, d = x_prompt.shape
    nb_s, t_s, _ = x_sample.shape
    depth = norm_mix_g.shape[0]
    n_mem = mem_prompt.shape[1]
    heads, head_dim = cache_mem_k.shape[-2:]
    dq = heads * head_dim
    dff = w_down.shape[1]
    fc = 512
    assert dff % fc == 0
    nj = dff // fc

    p = dict(
        norm_mix_g=norm_mix_g[:, None, :], norm_ffn_g=norm_ffn_g[:, None, :],
        norm_final_g=norm_final_g[None],
        w_in_a=w_in_a.astype(BF16), w_in_b=w_in_b.astype(BF16), w_out=w_out.astype(BF16),
        conv_a_w=conv_a_w, gmlp_norm_g=gmlp_norm_g[:, None, :], gmlp_norm_b=gmlp_norm_b[:, None, :],
        gmlp_ws=gmlp_ws, gmlp_bias=gmlp_bias[..., None],
        w_up=w_up.astype(BF16), w_down=w_down.astype(BF16),
        ffn_conv_w=_group_cols(ffn_conv_w, nj, fc),
        ffn_conv_b=_group_cols(ffn_conv_b, nj, fc)[:, None, :],
        fc=fc, nj=nj, cw_conv=256, cw_gmlp=256, cwh_conv=512, cwh_gmlp=256, head_dim=head_dim,
    )

    mk, mv = _memkv(mem_prompt.reshape(b * n_mem, d), norm_mem_g[:, None, :],
                    w_mem_k.astype(BF16), w_mem_v.astype(BF16), tn=512)
    mk = mk.reshape(depth, b, n_mem, dq)
    mv = mv.reshape(depth, b, n_mem, dq)
    conv_a_zero = jnp.zeros((cache_conv_a.shape[0], b) + cache_conv_a.shape[2:], F32)
    ffn_zero = jnp.zeros((depth, b) + cache_ffn_conv.shape[2:], F32)
    y_prompt, conv_a_prompt, ffn_conv_prompt, _ = _trunk(
        x_prompt, mk, mv, conv_a_zero, ffn_zero, p, emit_v=False,
        tiles=dict(mix_conv=(1024, 512), mix_gmlp=(1024, 512), ffn=(1024, 512)))

    y_sample, conv_a_sample, ffn_conv_sample, gmlp_v_sample = _trunk(
        x_sample, cache_mem_k.reshape(depth, nb_s, n_mem, dq),
        cache_mem_v.reshape(depth, nb_s, n_mem, dq), cache_conv_a,
        _group_cols(cache_ffn_conv, nj, fc), p, emit_v=True,
        tiles=dict.fromkeys(("mix_conv", "mix_gmlp", "ffn"), (nb_s * t_s, nb_s * t_s)))

    return (y_prompt, y_sample, conv_a_prompt, ffn_conv_prompt,
            mk.reshape(depth, b, n_mem, heads, head_dim), mv.reshape(depth, b, n_mem, heads, head_dim),
            conv_a_sample, ffn_conv_sample, gmlp_v_sample)
```

```python
import functools

import jax
import jax.numpy as jnp
from jax import lax
from jax.experimental import pallas as pl
from jax.experimental.pallas import tpu as pltpu

EPS = 1e-6
SUBLANES = 8
VMEM_REQUEST_CAP = 62 * 1024 * 1024

F32 = jnp.float32
BF16 = jnp.bfloat16


def _vmem_limit(est_bytes):
    return int(min(VMEM_REQUEST_CAP, max(32 * 1024 * 1024, est_bytes * 5 // 4)))


def _compiler_params(est_bytes):
    return pltpu.CompilerParams(dimension_semantics=("arbitrary", "arbitrary"),
                                vmem_limit_bytes=_vmem_limit(est_bytes))


def _dot(a, b):
    return jnp.dot(a, b, preferred_element_type=F32)


def _rms(x, g):
    return x * lax.rsqrt(jnp.mean(x * x, axis=-1, keepdims=True) + EPS) * g


def _gelu(x):
    return 0.5 * x * (1.0 + lax.erf(x * (2.0 ** -0.5)))


def _silu(x):
    return (0.5 * x) * (1.0 + jnp.tanh(0.5 * x))


def _cat(parts, axis):
    return parts[0] if len(parts) == 1 else jnp.concatenate(parts, axis=axis)


def _segment_spans(row0, nrows, seg):
    spans, pos = [], row0
    while pos < row0 + nrows:
        s, off = divmod(pos, seg)
        n = min(seg - off, row0 + nrows - pos)
        spans.append((s, off, n))
        pos += n
    return spans


def _conv3(u, h0, h1, w):
    s, c = u.shape
    r1 = pltpu.roll(u, 1, axis=0)
    r2 = pltpu.roll(u, 2, axis=0)
    row = lax.broadcasted_iota(jnp.int32, (SUBLANES, c), 0)
    top1 = jnp.where(row == 0, h1, r1[:SUBLANES])
    top2 = jnp.where(row == 0, h0, jnp.where(row == 1, h1, r2[:SUBLANES]))
    if s > SUBLANES:
        r1 = jnp.concatenate([top1, r1[SUBLANES:]], axis=0)
        r2 = jnp.concatenate([top2, r2[SUBLANES:]], axis=0)
    else:
        r1, r2 = top1, top2
    return w[0:1] * r2 + w[1:2] * r1 + w[2:3] * u


class _RowConv:
    def __init__(self, w, prev_ref, carry_ref, buf_ref, slot, first, seg):
        self.w, self.carry_ref, self.buf_ref, self.slot, self.seg = w, carry_ref, buf_ref, slot, seg
        self.tail = None

        def load_prev():
            for s in range(prev_ref.shape[0]):
                carry_ref[slot, s] = prev_ref[s]

        if first is True:
            load_prev()
        else:
            pl.when(first)(load_prev)

    def block(self, u, row0):
        outs, pos = [], 0
        for s, off, n in _segment_spans(row0, u.shape[0], self.seg):
            us = u[pos:pos + n]
            hist = self.carry_ref[self.slot, s] if off == 0 else self.tail
            outs.append(_conv3(us, hist[SUBLANES - 2:SUBLANES - 1], hist[SUBLANES - 1:], self.w))
            self.tail = us[n - SUBLANES:]
            if off + n == self.seg:
                self.carry_ref[self.slot, s] = self.tail
                self.buf_ref[s, self.slot] = self.tail
            pos += n
        return _cat(outs, 0)


def _pad_history(prev):
    pad = [(0, 0)] * prev.ndim
    pad[-2] = (SUBLANES - prev.shape[-2], 0)
    return jnp.pad(prev, pad)


def _unpack_history(buf, rows):
    nb, slots, _, c = buf.shape
    return buf[:, :, SUBLANES - rows:, :].swapaxes(1, 2).reshape(nb, rows, slots * c)


def _attention(q, k_ref, v_ref, row0, seg, head_dim):
    width = q.shape[1]
    scale = head_dim ** -0.5
    spans = _segment_spans(row0, q.shape[0], seg)
    head_cols = [slice(h * head_dim, (h + 1) * head_dim) for h in range(width // head_dim)]
    scores, pos = [], 0
    for s, _, n in spans:
        for cs in head_cols:
            qh = q[pos:pos + n, cs].astype(BF16)
            kh = k_ref[s, :, cs].astype(BF16)
            scores.append(lax.dot_general(qh, kh, (((1,), (1,)), ((), ())),
                                          preferred_element_type=F32) * scale)
        pos += n
    rows = []
    for si, (s, _, n) in enumerate(spans):
        heads = []
        for hi, cs in enumerate(head_cols):
            sc = scores[si * len(head_cols) + hi]
            m = jnp.max(sc, axis=-1, keepdims=True)
            p = jnp.exp(sc - m)
            l = jnp.sum(p, axis=-1, keepdims=True)
            heads.append(_dot(p.astype(BF16), v_ref[s, :, cs].astype(BF16)) / l)
        rows.append(_cat(heads, 1))
    return _cat(rows, 0)


def _pipelined(starts, lead, finish):
    cur = lead(starts[0])
    for k, r0 in enumerate(starts):
        nxt = lead(starts[k + 1]) if k + 1 < len(starts) else None
        finish(r0, cur)
        cur = nxt


def _init_rows(x_ref, g_ref, o_ref, xn_ref, r0, rb):
    x = x_ref[r0:r0 + rb, :]
    xn = _rms(x, g_ref[...]).astype(BF16)
    xn_ref[r0:r0 + rb, :] = xn
    o_ref[r0:r0 + rb, :] = x
    return xn


def _init_tile(x_ref, g_ref, o_ref, xn_ref, rb):
    for r0 in range(0, x_ref.shape[0], rb):
        _init_rows(x_ref, g_ref, o_ref, xn_ref, r0, rb)


def _attention_step(xn_ref, wq_ref, k_ref, v_ref, woh_ref, o_ref, rb, seg, head_dim):
    def lead(r0):
        return _dot(xn_ref[r0:r0 + rb, :], wq_ref[...])

    def finish(r0, q):
        y = _attention(q, k_ref, v_ref, r0, seg, head_dim)
        o_ref[r0:r0 + rb, :] += _dot(y.astype(BF16), woh_ref[...])

    _pipelined(range(0, xn_ref.shape[0], rb), lead, finish)


def _mix_conv_kernel(x_ref, g_ref, wb_ref, wc_ref, wh_ref, wq_ref, cw_ref, prev_ref, k_ref, v_ref,
                     wom_ref, woh_ref, o_ref, buf_ref, xn_ref, carry_ref, *, seg, rb,
                     tiles_per_batch, nc, head_dim):
    i = pl.program_id(0)
    j = pl.program_id(1)
    tm = x_ref.shape[0]

    @pl.when(j == 0)
    def _():
        _init_tile(x_ref, g_ref, o_ref, xn_ref, rb)

    @pl.when(j < nc)
    def _():
        first = True if tiles_per_batch == 1 else (i % tiles_per_batch) == 0
        conv = _RowConv(cw_ref[...], prev_ref, carry_ref, buf_ref, j, first, seg)

        def lead(r0):
            xn = xn_ref[r0:r0 + rb, :]
            return _dot(xn, wc_ref[...]), _dot(xn, wh_ref[...]), _dot(xn, wb_ref[...])

        def finish(r0, chb):
            c, h, b = chb
            y = b * conv.block(c * h, r0)
            o_ref[r0:r0 + rb, :] += _dot(y.astype(BF16), wom_ref[...])

        _pipelined(range(0, tm, rb), lead, finish)

    @pl.when(j >= nc)
    def _():
        _attention_step(xn_ref, wq_ref, k_ref, v_ref, woh_ref, o_ref, rb, seg, head_dim)


def _mix_conv(x, g, w_in, conv_w, prev, mem_k, mem_v, w_out, *, layer, slot, tm, rb, seg, batch_len,
              cw, cwh, head_dim):
    m, d = x.shape
    dm = conv_w.shape[-1]
    nb, n_mem, dq = mem_k.shape[1:]
    nseg = tm // seg
    tpb = batch_len // tm if nseg == 1 else 1
    nc, nh = dm // cw, dq // cwh
    mc = lambda j: jnp.minimum(j, nc - 1)
    hc = lambda j: jnp.maximum(j - nc, 0)
    in_specs = [
        pl.BlockSpec((tm, d), lambda i, j: (i, 0)),
        pl.BlockSpec((None, 1, d), lambda i, j: (layer, 0, 0)),
        pl.BlockSpec((None, d, cw), lambda i, j: (slot, 0, mc(j))),
        pl.BlockSpec((None, d, cw), lambda i, j: (slot, 0, nc + mc(j))),
        pl.BlockSpec((None, d, cw), lambda i, j: (slot, 0, 2 * nc + mc(j))),
        pl.BlockSpec((None, d, cwh), lambda i, j: (slot, 0, 3 * dm // cwh + hc(j))),
        pl.BlockSpec((None, 3, cw), lambda i, j: (slot, 0, mc(j))),
        pl.BlockSpec((None, nseg, SUBLANES, cw), lambda i, j: (slot, i // tpb, 0, mc(j))),
        pl.BlockSpec((None, nseg, n_mem, cwh), lambda i, j: (layer, i // tpb, 0, hc(j))),
        pl.BlockSpec((None, nseg, n_mem, cwh), lambda i, j: (layer, i // tpb, 0, hc(j))),
        pl.BlockSpec((None, cw, d), lambda i, j: (layer, mc(j), 0)),
        pl.BlockSpec((None, cwh, d), lambda i, j: (layer, dm // cwh + hc(j), 0)),
    ]
    out_specs = [
        pl.BlockSpec((tm, d), lambda i, j: (i, 0)),
        pl.BlockSpec((nseg, nc, SUBLANES, cw), lambda i, j: (i // tpb, 0, 0, 0)),
    ]
    scratch = [pltpu.VMEM((tm, d), BF16), pltpu.VMEM((nc, nseg, SUBLANES, cw), F32)]
    est = (4 * tm * d * 4 + tm * d * 2 + 2 * (3 * d * cw + d * cwh + (cw + cwh) * d) * 2
           + 4 * nseg * n_mem * cwh * 4 + 8 * rb * cw * 4)
    x_out, buf = pl.pallas_call(
        functools.partial(_mix_conv_kernel, seg=seg, rb=rb, tiles_per_batch=tpb, nc=nc,
                          head_dim=head_dim),
        grid=(m // tm, nc + nh),
        in_specs=in_specs,
        out_specs=out_specs,
        out_shape=[jax.ShapeDtypeStruct((m, d), F32),
                   jax.ShapeDtypeStruct((nb, nc, SUBLANES, cw), F32)],
        scratch_shapes=scratch,
        compiler_params=_compiler_params(est),
        name="mix_conv",
    )(x, g, w_in, w_in, w_in, w_in, conv_w, _pad_history(prev), mem_k, mem_v, w_out, w_out)
    return x_out, _unpack_history(buf, prev.shape[-2])


def _mix_gmlp_kernel(x_ref, g_ref, wv_ref, wu_ref, wq_ref, lng_ref, lnb_ref, ws_ref, bias_ref, k_ref,
                     v_ref, wom_ref, woh_ref, o_ref, *rest, seg, rb, rblk, nc, head_dim, group_dim,
                     emit_v):
    if emit_v:
        vout_ref, xn_ref, vs_ref = rest
    else:
        vout_ref = None
        xn_ref, vs_ref = rest
    j = pl.program_id(1)
    tm = x_ref.shape[0]
    cw = wv_ref.shape[1]

    @pl.when(j == 0)
    def _():
        _init_tile(x_ref, g_ref, o_ref, xn_ref, rb)

    @pl.when(j < nc)
    def _():
        for r0 in range(0, tm, rb):
            vs_ref[j, r0:r0 + rb, :] = _gelu(_dot(xn_ref[r0:r0 + rb, :], wv_ref[...]))

    @pl.when(j == nc)
    def _():
        width = nc * cw
        for r0 in range(0, tm, rb):
            rows = slice(r0, r0 + rb)
            tot = vs_ref[0, rows, :].sum(axis=-1, keepdims=True)
            for c in range(1, nc):
                tot += vs_ref[c, rows, :].sum(axis=-1, keepdims=True)
            mean = tot / width
            sq = None
            for c in range(nc):
                dv = vs_ref[c, rows, :] - mean
                part = (dv * dv).sum(axis=-1, keepdims=True)
                sq = part if sq is None else sq + part
            rstd = lax.rsqrt(sq / width + EPS)
            for c in range(nc):
                cs = slice(c * cw, (c + 1) * cw)
                vn = (vs_ref[c, rows, :] - mean) * rstd * lng_ref[:, cs] + lnb_ref[:, cs]
                vs_ref[c, rows, :] = vn
                if emit_v:
                    vout_ref[rows, cs] = vn

    @pl.when(jnp.logical_and(j >= nc, j < 2 * nc))
    def _():
        tri = (lax.broadcasted_iota(jnp.int32, (rblk, rblk), 0)
               >= lax.broadcasted_iota(jnp.int32, (rblk, rblk), 1))
        wmats = [jnp.where(tri, ws_ref[gi, :rblk, :rblk], 0.0).astype(BF16)
                 for gi in range(cw // group_dim)]

        def lead(r0):
            u = _dot(xn_ref[r0:r0 + rb, :], wu_ref[...])
            vn = vs_ref[j - nc, r0:r0 + rb, :].astype(BF16)
            cols = []
            for gi, wmat in enumerate(wmats):
                blocks = [_dot(wmat, vn[r:r + rblk, gi * group_dim:(gi + 1) * group_dim])
                          for r in range(0, rb, rblk)]
                cols.append(_cat(blocks, 0))
            return u, _cat(cols, 1)

        def finish(r0, um):
            u, mixed = um
            bias = _cat([_cat([jnp.broadcast_to(bias_ref[gi, :rblk, :], (rblk, group_dim))]
                              * (rb // rblk), 0) for gi in range(len(wmats))], 1)
            y = _gelu(u) * (mixed + bias)
            o_ref[r0:r0 + rb, :] += _dot(y.astype(BF16), wom_ref[...])

        _pipelined(range(0, tm, rb), lead, finish)

    @pl.when(j >= 2 * nc)
    def _():
        _attention_step(xn_ref, wq_ref, k_ref, v_ref, woh_ref, o_ref, rb, seg, head_dim)


def _mix_gmlp(x, g, w_in, ln_g, ln_b, ws, bias, mem_k, mem_v, w_out, *, layer, slot, tm, rb, seg,
              batch_len, cw, cwh, head_dim, emit_v):
    m, d = x.shape
    dm = ln_g.shape[-1]
    nb, n_mem, dq = mem_k.shape[1:]
    groups, chunk = ws.shape[1:3]
    group_dim = dm // groups
    nseg = tm // seg
    tpb = batch_len // tm if nseg == 1 else 1
    rblk = min(chunk, seg)
    assert seg % rblk == 0 and rb % rblk == 0 and cw % group_dim == 0
    nc, nh = dm // cw, dq // cwh
    gpc = cw // group_dim
    uc = lambda j: jnp.clip(j - nc, 0, nc - 1)
    hc = lambda j: jnp.maximum(j - 2 * nc, 0)
    in_specs = [
        pl.BlockSpec((tm, d), lambda i, j: (i, 0)),
        pl.BlockSpec((None, 1, d), lambda i, j: (layer, 0, 0)),
        pl.BlockSpec((None, d, cw), lambda i, j: (slot, 0, nc + jnp.minimum(j, nc - 1))),
        pl.BlockSpec((None, d, cw), lambda i, j: (slot, 0, uc(j))),
        pl.BlockSpec((None, d, cwh), lambda i, j: (slot, 0, 2 * dm // cwh + hc(j))),
        pl.BlockSpec((None, 1, dm), lambda i, j: (slot, 0, 0)),
        pl.BlockSpec((None, 1, dm), lambda i, j: (slot, 0, 0)),
        pl.BlockSpec((None, gpc, chunk, chunk), lambda i, j: (slot, uc(j), 0, 0)),
        pl.BlockSpec((None, gpc, chunk, 1), lambda i, j: (slot, uc(j), 0, 0)),
        pl.BlockSpec((None, nseg, n_mem, cwh), lambda i, j: (layer, i // tpb, 0, hc(j))),
        pl.BlockSpec((None, nseg, n_mem, cwh), lambda i, j: (layer, i // tpb, 0, hc(j))),
        pl.BlockSpec((None, cw, d), lambda i, j: (layer, uc(j), 0)),
        pl.BlockSpec((None, cwh, d), lambda i, j: (layer, dm // cwh + hc(j), 0)),
    ]
    out_specs = [pl.BlockSpec((tm, d), lambda i, j: (i, 0))]
    out_shape = [jax.ShapeDtypeStruct((m, d), F32)]
    if emit_v:
        out_specs.append(pl.BlockSpec((tm, dm), lambda i, j: (i, 0)))
        out_shape.append(jax.ShapeDtypeStruct((m, dm), F32))
    est = (4 * tm * d * 4 + tm * d * 2 + tm * dm * 4 * (3 if emit_v else 1)
           + 2 * (2 * d * cw + d * cwh + (cw + cwh) * d) * 2
           + 4 * nseg * n_mem * cwh * 4 + 8 * rb * cw * 4)
    outs = pl.pallas_call(
        functools.partial(_mix_gmlp_kernel, seg=seg, rb=rb, rblk=rblk, nc=nc, head_dim=head_dim,
                          group_dim=group_dim, emit_v=emit_v),
        grid=(m // tm, 2 * nc + nh),
        in_specs=in_specs,
        out_specs=out_specs,
        out_shape=out_shape,
        scratch_shapes=[pltpu.VMEM((tm, d), BF16), pltpu.VMEM((nc, tm, cw), F32)],
        compiler_params=_compiler_params(est),
        name="mix_gmlp",
    )(x, g, w_in, w_in, w_in, ln_g, ln_b, ws, bias, mem_k, mem_v, w_out, w_out)
    return (outs[0], outs[1]) if emit_v else (outs[0], None)


def _ffn_kernel(x_ref, g_ref, wa_ref, wg_ref, cw_ref, cb_ref, prev_ref, wdn_ref, gf_ref, o_ref,
                buf_ref, xn_ref, carry_ref, u0_ref, u1_ref, *, seg, rb, tiles_per_batch, final_norm):
    i = pl.program_id(0)
    j = pl.program_id(1)
    tm = x_ref.shape[0]
    fc = wdn_ref.shape[0]
    u_refs = (u0_ref, u1_ref)

    first = True if tiles_per_batch == 1 else (i % tiles_per_batch) == 0

    def step(new_tile):
        conv = _RowConv(cw_ref[...], prev_ref, carry_ref, buf_ref, j, first, seg)

        def lead(r0):
            xn = (_init_rows(x_ref, g_ref, o_ref, xn_ref, r0, rb) if new_tile
                  else xn_ref[r0:r0 + rb, :])
            u_ref = u_refs[(r0 // rb) % 2]
            u_ref[:, 0:fc] = _dot(xn, wa_ref[...])
            u_ref[:, fc:2 * fc] = _dot(xn, wg_ref[...])
            return u_ref

        def finish(r0, u_ref):
            z = conv.block(u_ref[...], r0) + cb_ref[...]
            h = _silu(z[:, fc:]) * z[:, :fc]
            o_ref[r0:r0 + rb, :] += _dot(h.astype(BF16), wdn_ref[...])

        _pipelined(range(0, tm, rb), lead, finish)

    pl.when(j == 0)(functools.partial(step, True))
    pl.when(j > 0)(functools.partial(step, False))

    if final_norm:
        @pl.when(j == pl.num_programs(1) - 1)
        def _():
            for r0 in range(0, tm, rb):
                o_ref[r0:r0 + rb, :] = _rms(o_ref[r0:r0 + rb, :], gf_ref[...])


def _ffn(x, g, w_up, conv_w, conv_b, prev, w_down, g_final, *, layer, tm, rb, seg, batch_len, fc,
         final_norm):
    m, d = x.shape
    dff = w_down.shape[1]
    nj = dff // fc
    nseg = tm // seg
    tpb = batch_len // tm if nseg == 1 else 1
    nb = prev.shape[1]
    in_specs = [
        pl.BlockSpec((tm, d), lambda i, j: (i, 0)),
        pl.BlockSpec((None, 1, d), lambda i, j: (layer, 0, 0)),
        pl.BlockSpec((None, d, fc), lambda i, j: (layer, 0, j)),
        pl.BlockSpec((None, d, fc), lambda i, j: (layer, 0, nj + j)),
        pl.BlockSpec((None, 3, 2 * fc), lambda i, j: (layer, 0, j)),
        pl.BlockSpec((None, 1, 2 * fc), lambda i, j: (layer, 0, j)),
        pl.BlockSpec((None, nseg, SUBLANES, 2 * fc), lambda i, j: (layer, i // tpb, 0, j)),
        pl.BlockSpec((None, fc, d), lambda i, j: (layer, j, 0)),
        pl.BlockSpec((1, d), lambda i, j: (0, 0)),
    ]
    out_specs = [
        pl.BlockSpec((tm, d), lambda i, j: (i, 0)),
        pl.BlockSpec((nseg, nj, SUBLANES, 2 * fc), lambda i, j: (i // tpb, 0, 0, 0)),
    ]
    scratch = [pltpu.VMEM((tm, d), BF16), pltpu.VMEM((nj, nseg, SUBLANES, 2 * fc), F32),
               pltpu.VMEM((rb, 2 * fc), F32), pltpu.VMEM((rb, 2 * fc), F32)]
    est = 4 * tm * d * 4 + tm * d * 2 + 2 * (3 * d * fc) * 2 + 10 * rb * 2 * fc * 4
    x_out, buf = pl.pallas_call(
        functools.partial(_ffn_kernel, seg=seg, rb=rb, tiles_per_batch=tpb, final_norm=final_norm),
        grid=(m // tm, nj),
        in_specs=in_specs,
        out_specs=out_specs,
        out_shape=[jax.ShapeDtypeStruct((m, d), F32),
                   jax.ShapeDtypeStruct((nb, nj, SUBLANES, 2 * fc), F32)],
        scratch_shapes=scratch,
        compiler_params=_compiler_params(est),
        name="conv_ffn",
    )(x, g, w_up, w_up, conv_w, conv_b, _pad_history(prev), w_down, g_final)
    return x_out, _unpack_history(buf, prev.shape[-2])


def _memkv_kernel(m_ref, g_ref, wk_ref, wv_ref, ko_ref, vo_ref):
    mn = _rms(m_ref[...], g_ref[0]).astype(BF16)
    ko_ref[0] = _dot(mn, wk_ref[0])
    vo_ref[0] = _dot(mn, wv_ref[0])


def _memkv(mem, g, wk, wv, *, tn):
    rows, d = mem.shape
    depth, _, dq = wk.shape
    shape = jax.ShapeDtypeStruct((depth, rows, dq), F32)
    return pl.pallas_call(
        _memkv_kernel,
        grid=(depth, dq // tn),
        in_specs=[
            pl.BlockSpec((rows, d), lambda l, n: (0, 0)),
            pl.BlockSpec((1, 1, d), lambda l, n: (l, 0, 0)),
            pl.BlockSpec((1, d, tn), lambda l, n: (l, 0, n)),
            pl.BlockSpec((1, d, tn), lambda l, n: (l, 0, n)),
        ],
        out_specs=[pl.BlockSpec((1, rows, tn), lambda l, n: (l, 0, n))] * 2,
        out_shape=[shape, shape],
        compiler_params=pltpu.CompilerParams(dimension_semantics=("arbitrary", "arbitrary")),
        name="memory_kv",
    )(mem, g, wk, wv)


def _group_cols(t, nj, fc):
    lead = t.shape[:-1]
    return t.reshape(lead + (2, nj, fc)).swapaxes(-3, -2).reshape(lead + (2 * nj * fc,))


def _ungroup_cols(t, nj, fc):
    lead = t.shape[:-1]
    return t.reshape(lead + (nj, 2, fc)).swapaxes(-3, -2).reshape(lead + (2 * nj * fc,))


def _tiling(t, tm, rb):
    return dict(tm=tm, rb=rb, seg=min(t, tm), batch_len=t)


def _trunk(x3, mem_k, mem_v, conv_a_prev, ffn_prev, p, *, tiles, emit_v):
    nb, t, d = x3.shape
    x = x3.reshape(nb * t, d)
    depth = p["norm_mix_g"].shape[0]
    conv_a_new, ffn_new, gmlp_v = [], [], []
    mixer = dict(head_dim=p["head_dim"])
    for i in range(depth):
        kind, jx = i % 2, i // 2
        if kind == 0:
            x, buf = _mix_conv(x, p["norm_mix_g"], p["w_in_a"], p["conv_a_w"], conv_a_prev, mem_k,
                               mem_v, p["w_out"], layer=i, slot=jx, cw=p["cw_conv"], cwh=p["cwh_conv"], **mixer,
                               **_tiling(t, *tiles["mix_conv"]))
            conv_a_new.append(buf)
        else:
            x, v_rows = _mix_gmlp(x, p["norm_mix_g"], p["w_in_b"], p["gmlp_norm_g"], p["gmlp_norm_b"],
                                  p["gmlp_ws"], p["gmlp_bias"], mem_k, mem_v, p["w_out"], layer=i,
                                  slot=jx, emit_v=emit_v, cw=p["cw_gmlp"], cwh=p["cwh_gmlp"], **mixer,
                                  **_tiling(t, *tiles["mix_gmlp"]))
            gmlp_v.append(v_rows)
        x, buf = _ffn(x, p["norm_ffn_g"], p["w_up"], p["ffn_conv_w"], p["ffn_conv_b"], ffn_prev,
                      p["w_down"], p["norm_final_g"], layer=i, fc=p["fc"],
                      final_norm=(i == depth - 1), **_tiling(t, *tiles["ffn"]))
        ffn_new.append(_ungroup_cols(buf, p["nj"], p["fc"]))
    y = x.reshape(nb, t, d)
    v_out = jnp.stack([v.reshape(nb, t, -1) for v in gmlp_v]) if emit_v else None
    return y, jnp.stack(conv_a_new), jnp.stack(ffn_new), v_out


def kernel(x_prompt, x_sample, mem_prompt, cache_conv_a, cache_ffn_conv, cache_mem_k, cache_mem_v,
           norm_mix_g, norm_mem_g, w_mem_k, w_mem_v, w_in_a, conv_a_w, w_in_b, gmlp_norm_g,
           gmlp_norm_b, gmlp_ws, gmlp_bias, w_out, norm_ffn_g, w_up, ffn_conv_w, ffn_conv_b, w_down,
           norm_final_g):
    b, s, d = x_prompt.shape
    nb_s, t_s, _ = x_sample.shape
    depth = norm_mix_g.shape[0]
    n_mem = mem_prompt.shape[1]
    heads, head_dim = cache_mem_k.shape[-2:]
    dq = heads * head_dim
    dff = w_down.shape[1]
    fc = 512
    assert dff % fc == 0
    nj = dff // fc

    p = dict(
        norm_mix_g=norm_mix_g[:, None, :], norm_ffn_g=norm_ffn_g[:, None, :],
        norm_final_g=norm_final_g[None],
        w_in_a=w_in_a.astype(BF16), w_in_b=w_in_b.astype(BF16), w_out=w_out.astype(BF16),
        conv_a_w=conv_a_w, gmlp_norm_g=gmlp_norm_g[:, None, :], gmlp_norm_b=gmlp_norm_b[:, None, :],
        gmlp_ws=gmlp_ws, gmlp_bias=gmlp_bias[..., None],
        w_up=w_up.astype(BF16), w_down=w_down.astype(BF16),
        ffn_conv_w=_group_cols(ffn_conv_w, nj, fc),
        ffn_conv_b=_group_cols(ffn_conv_b, nj, fc)[:, None, :],
        fc=fc, nj=nj, cw_conv=256, cw_gmlp=256, cwh_conv=512, cwh_gmlp=256, head_dim=head_dim,
    )

    mk, mv = _memkv(mem_prompt.reshape(b * n_mem, d), norm_mem_g[:, None, :],
                    w_mem_k.astype(BF16), w_mem_v.astype(BF16), tn=512)
    mk = mk.reshape(depth, b, n_mem, dq)
    mv = mv.reshape(depth, b, n_mem, dq)
    conv_a_zero = jnp.zeros((cache_conv_a.shape[0], b) + cache_conv_a.shape[2:], F32)
    ffn_zero = jnp.zeros((depth, b) + cache_ffn_conv.shape[2:], F32)
    y_prompt, conv_a_prompt, ffn_conv_prompt, _ = _trunk(
        x_prompt, mk, mv, conv_a_zero, ffn_zero, p, emit_v=False,
        tiles=dict(mix_conv=(1024, 512), mix_gmlp=(1024, 512), ffn=(1024, 512)))

    y_sample, conv_a_sample, ffn_conv_sample, gmlp_v_sample = _trunk(
        x_sample, cache_mem_k.reshape(depth, nb_s, n_mem, dq),
        cache_mem_v.reshape(depth, nb_s, n_mem, dq), cache_conv_a,
        _group_cols(cache_ffn_conv, nj, fc), p, emit_v=True,
        tiles=dict.fromkeys(("mix_conv", "mix_gmlp", "ffn"), (nb_s * t_s, nb_s * t_s)))

    return (y_prompt, y_sample, conv_a_prompt, ffn_conv_prompt,
            mk.reshape(depth, b, n_mem, heads, head_dim), mv.reshape(depth, b, n_mem, heads, head_dim),
            conv_a_sample, ffn_conv_sample, gmlp_v_sample)
```

```python
import functools

import jax
import jax.numpy as jnp
from jax import lax
from jax.experimental import pallas as pl
from jax.experimental.pallas import tpu as pltpu

EPS = 1e-6
SUBLANES = 8
VMEM_REQUEST_CAP = 62 * 1024 * 1024

F32 = jnp.float32
BF16 = jnp.bfloat16


def _vmem_limit(est_bytes):
    return int(min(VMEM_REQUEST_CAP, max(32 * 1024 * 1024, est_bytes * 5 // 4)))


def _compiler_params(est_bytes):
    return pltpu.CompilerParams(dimension_semantics=("arbitrary", "arbitrary"),
                                vmem_limit_bytes=_vmem_limit(est_bytes))


def _dot(a, b):
    return jnp.dot(a, b, preferred_element_type=F32)


def _rms(x, g):
    return x * lax.rsqrt(jnp.mean(x * x, axis=-1, keepdims=True) + EPS) * g


def _gelu(x):
    return 0.5 * x * (1.0 + lax.erf(x * (2.0 ** -0.5)))


def _silu(x):
    return (0.5 * x) * (1.0 + jnp.tanh(0.5 * x))


def _cat(parts, axis):
    return parts[0] if len(parts) == 1 else jnp.concatenate(parts, axis=axis)


def _segment_spans(row0, nrows, seg):
    spans, pos = [], row0
    while pos < row0 + nrows:
        s, off = divmod(pos, seg)
        n = min(seg - off, row0 + nrows - pos)
        spans.append((s, off, n))
        pos += n
    return spans


def _conv3(u, h0, h1, w):
    s, c = u.shape
    r1 = pltpu.roll(u, 1, axis=0)
    r2 = pltpu.roll(u, 2, axis=0)
    row = lax.broadcasted_iota(jnp.int32, (SUBLANES, c), 0)
    top1 = jnp.where(row == 0, h1, r1[:SUBLANES])
    top2 = jnp.where(row == 0, h0, jnp.where(row == 1, h1, r2[:SUBLANES]))
    if s > SUBLANES:
        r1 = jnp.concatenate([top1, r1[SUBLANES:]], axis=0)
        r2 = jnp.concatenate([top2, r2[SUBLANES:]], axis=0)
    else:
        r1, r2 = top1, top2
    return w[0:1] * r2 + w[1:2] * r1 + w[2:3] * u


class _RowConv:
    def __init__(self, w, prev_ref, carry_ref, buf_ref, slot, first, seg):
        self.w, self.carry_ref, self.buf_ref, self.slot, self.seg = w, carry_ref, buf_ref, slot, seg
        self.tail = None

        def load_prev():
            for s in range(prev_ref.shape[0]):
                carry_ref[slot, s] = prev_ref[s]

        if first is True:
            load_prev()
        else:
            pl.when(first)(load_prev)

    def block(self, u, row0):
        outs, pos = [], 0
        for s, off, n in _segment_spans(row0, u.shape[0], self.seg):
            us = u[pos:pos + n]
            hist = self.carry_ref[self.slot, s] if off == 0 else self.tail
            outs.append(_conv3(us, hist[SUBLANES - 2:SUBLANES - 1], hist[SUBLANES - 1:], self.w))
            self.tail = us[n - SUBLANES:]
            if off + n == self.seg:
                self.carry_ref[self.slot, s] = self.tail
                self.buf_ref[s, self.slot] = self.tail
            pos += n
        return _cat(outs, 0)


def _pad_history(prev):
    pad = [(0, 0)] * prev.ndim
    pad[-2] = (SUBLANES - prev.shape[-2], 0)
    return jnp.pad(prev, pad)


def _unpack_history(buf, rows):
    nb, slots, _, c = buf.shape
    return buf[:, :, SUBLANES - rows:, :].swapaxes(1, 2).reshape(nb, rows, slots * c)


def _attention(q, k_ref, v_ref, row0, seg, head_dim):
    width = q.shape[1]
    scale = head_dim ** -0.5
    spans = _segment_spans(row0, q.shape[0], seg)
    head_cols = [slice(h * head_dim, (h + 1) * head_dim) for h in range(width // head_dim)]
    scores, pos = [], 0
    for s, _, n in spans:
        for cs in head_cols:
            qh = q[pos:pos + n, cs].astype(BF16)
            kh = k_ref[s, :, cs].astype(BF16)
            scores.append(lax.dot_general(qh, kh, (((1,), (1,)), ((), ())),
                                          preferred_element_type=F32) * scale)
        pos += n
    rows = []
    for si, (s, _, n) in enumerate(spans):
        heads = []
        for hi, cs in enumerate(head_cols):
            sc = scores[si * len(head_cols) + hi]
            m = jnp.max(sc, axis=-1, keepdims=True)
            p = jnp.exp(sc - m)
            l = jnp.sum(p, axis=-1, keepdims=True)
            heads.append(_dot(p.astype(BF16), v_ref[s, :, cs].astype(BF16)) / l)
        rows.append(_cat(heads, 1))
    return _cat(rows, 0)


def _pipelined(starts, lead, finish):
    cur = lead(starts[0])
    for k, r0 in enumerate(starts):
        nxt = lead(starts[k + 1]) if k + 1 < len(starts) else None
        finish(r0, cur)
        cur = nxt


def _init_rows(x_ref, g_ref, o_ref, xn_ref, r0, rb):
    x = x_ref[r0:r0 + rb, :]
    xn = _rms(x, g_ref[...]).astype(BF16)
    xn_ref[r0:r0 + rb, :] = xn
    o_ref[r0:r0 + rb, :] = x
    return xn


def _init_tile(x_ref, g_ref, o_ref, xn_ref, rb):
    for r0 in range(0, x_ref.shape[0], rb):
        _init_rows(x_ref, g_ref, o_ref, xn_ref, r0, rb)


def _stream_x_tile(x_hbm, xbuf_ref, sem, i, j):
    tm = xbuf_ref.shape[0]

    def copy(tile):
        return pltpu.make_async_copy(x_hbm.at[pl.ds(pl.multiple_of(tile * tm, tm), tm), :], xbuf_ref, sem)

    @pl.when(jnp.logical_and(i == 0, j == 0))
    def _():
        copy(0).start()

    @pl.when(j == 0)
    def _():
        copy(i).wait()

    @pl.when(jnp.logical_and(j == 1, i + 1 < pl.num_programs(0)))
    def _():
        copy(i + 1).start()


def _attention_step(xn_ref, wq_ref, k_ref, v_ref, woh_ref, o_ref, rb, seg, head_dim):
    def lead(r0):
        return _dot(xn_ref[r0:r0 + rb, :], wq_ref[...])

    def finish(r0, q):
        y = _attention(q, k_ref, v_ref, r0, seg, head_dim)
        o_ref[r0:r0 + rb, :] += _dot(y.astype(BF16), woh_ref[...])

    _pipelined(range(0, xn_ref.shape[0], rb), lead, finish)


def _mix_conv_kernel(x_hbm, g_ref, wb_ref, wc_ref, wh_ref, wq_ref, cw_ref, prev_ref, k_ref, v_ref,
                     wom_ref, woh_ref, o_ref, buf_ref, xn_ref, carry_ref, x_ref, x_sem, *, seg, rb,
                     tiles_per_batch, nc, head_dim):
    i = pl.program_id(0)
    j = pl.program_id(1)
    tm = x_ref.shape[0]
    _stream_x_tile(x_hbm, x_ref, x_sem, i, j)

    @pl.when(j == 0)
    def _():
        _init_tile(x_ref, g_ref, o_ref, xn_ref, rb)

    @pl.when(j < nc)
    def _():
        first = True if tiles_per_batch == 1 else (i % tiles_per_batch) == 0
        conv = _RowConv(cw_ref[...], prev_ref, carry_ref, buf_ref, j, first, seg)

        def lead(r0):
            xn = xn_ref[r0:r0 + rb, :]
            return _dot(xn, wc_ref[...]), _dot(xn, wh_ref[...]), _dot(xn, wb_ref[...])

        def finish(r0, chb):
            c, h, b = chb
            y = b * conv.block(c * h, r0)
            o_ref[r0:r0 + rb, :] += _dot(y.astype(BF16), wom_ref[...])

        _pipelined(range(0, tm, rb), lead, finish)

    @pl.when(j >= nc)
    def _():
        _attention_step(xn_ref, wq_ref, k_ref, v_ref, woh_ref, o_ref, rb, seg, head_dim)


def _mix_conv(x, g, w_in, conv_w, prev, mem_k, mem_v, w_out, *, layer, slot, tm, rb, seg, batch_len,
              cw, cwh, head_dim):
    m, d = x.shape
    dm = conv_w.shape[-1]
    nb, n_mem, dq = mem_k.shape[1:]
    nseg = tm // seg
    tpb = batch_len // tm if nseg == 1 else 1
    nc, nh = dm // cw, dq // cwh
    mc = lambda j: jnp.minimum(j, nc - 1)
    hc = lambda j: jnp.maximum(j - nc, 0)
    in_specs = [
        pl.BlockSpec(memory_space=pl.ANY),
        pl.BlockSpec((None, 1, d), lambda i, j: (layer, 0, 0)),
        pl.BlockSpec((None, d, cw), lambda i, j: (slot, 0, mc(j))),
        pl.BlockSpec((None, d, cw), lambda i, j: (slot, 0, nc + mc(j))),
        pl.BlockSpec((None, d, cw), lambda i, j: (slot, 0, 2 * nc + mc(j))),
        pl.BlockSpec((None, d, cwh), lambda i, j: (slot, 0, 3 * dm // cwh + hc(j))),
        pl.BlockSpec((None, 3, cw), lambda i, j: (slot, 0, mc(j))),
        pl.BlockSpec((None, nseg, SUBLANES, cw), lambda i, j: (slot, i // tpb, 0, mc(j))),
        pl.BlockSpec((None, nseg, n_mem, cwh), lambda i, j: (layer, i // tpb, 0, hc(j))),
        pl.BlockSpec((None, nseg, n_mem, cwh), lambda i, j: (layer, i // tpb, 0, hc(j))),
        pl.BlockSpec((None, cw, d), lambda i, j: (layer, mc(j), 0)),
        pl.BlockSpec((None, cwh, d), lambda i, j: (layer, dm // cwh + hc(j), 0)),
    ]
    out_specs = [
        pl.BlockSpec((tm, d), lambda i, j: (i, 0)),
        pl.BlockSpec((nseg, nc, SUBLANES, cw), lambda i, j: (i // tpb, 0, 0, 0)),
    ]
    scratch = [pltpu.VMEM((tm, d), BF16), pltpu.VMEM((nc, nseg, SUBLANES, cw), F32),
               pltpu.VMEM((tm, d), F32), pltpu.SemaphoreType.DMA(())]
    est = (3 * tm * d * 4 + tm * d * 2 + 2 * (3 * d * cw + d * cwh + (cw + cwh) * d) * 2
           + 4 * nseg * n_mem * cwh * 4 + 8 * rb * cw * 4)
    x_out, buf = pl.pallas_call(
        functools.partial(_mix_conv_kernel, seg=seg, rb=rb, tiles_per_batch=tpb, nc=nc,
                          head_dim=head_dim),
        grid=(m // tm, nc + nh),
        in_specs=in_specs,
        out_specs=out_specs,
        out_shape=[jax.ShapeDtypeStruct((m, d), F32),
                   jax.ShapeDtypeStruct((nb, nc, SUBLANES, cw), F32)],
        scratch_shapes=scratch,
        compiler_params=_compiler_params(est),
        name="mix_conv",
    )(x, g, w_in, w_in, w_in, w_in, conv_w, _pad_history(prev), mem_k, mem_v, w_out, w_out)
    return x_out, _unpack_history(buf, prev.shape[-2])


def _mix_gmlp_kernel(x_hbm, g_ref, wv_ref, wu_ref, wq_ref, lng_ref, lnb_ref, ws_ref, bias_ref, k_ref,
                     v_ref, wom_ref, woh_ref, o_ref, *rest, seg, rb, rblk, nc, head_dim, group_dim,
                     emit_v):
    if emit_v:
        vout_ref, xn_ref, vs_ref, x_ref, x_sem = rest
    else:
        vout_ref = None
        xn_ref, vs_ref, x_ref, x_sem = rest
    j = pl.program_id(1)
    tm = x_ref.shape[0]
    cw = wv_ref.shape[1]
    _stream_x_tile(x_hbm, x_ref, x_sem, pl.program_id(0), j)

    @pl.when(j == 0)
    def _():
        _init_tile(x_ref, g_ref, o_ref, xn_ref, rb)

    @pl.when(j < nc)
    def _():
        for r0 in range(0, tm, rb):
            vs_ref[j, r0:r0 + rb, :] = _gelu(_dot(xn_ref[r0:r0 + rb, :], wv_ref[...]))

    @pl.when(j == nc)
    def _():
        width = nc * cw
        for r0 in range(0, tm, rb):
            rows = slice(r0, r0 + rb)
            tot = vs_ref[0, rows, :].sum(axis=-1, keepdims=True)
            for c in range(1, nc):
                tot += vs_ref[c, rows, :].sum(axis=-1, keepdims=True)
            mean = tot / width
            sq = None
            for c in range(nc):
                dv = vs_ref[c, rows, :] - mean
                part = (dv * dv).sum(axis=-1, keepdims=True)
                sq = part if sq is None else sq + part
            rstd = lax.rsqrt(sq / width + EPS)
            for c in range(nc):
                cs = slice(c * cw, (c + 1) * cw)
                vn = (vs_ref[c, rows, :] - mean) * rstd * lng_ref[:, cs] + lnb_ref[:, cs]
                vs_ref[c, rows, :] = vn
                if emit_v:
                    vout_ref[rows, cs] = vn

    @pl.when(jnp.logical_and(j >= nc, j < 2 * nc))
    def _():
        tri = (lax.broadcasted_iota(jnp.int32, (rblk, rblk), 0)
               >= lax.broadcasted_iota(jnp.int32, (rblk, rblk), 1))
        wmats = [jnp.where(tri, ws_ref[gi, :rblk, :rblk], 0.0).astype(BF16)
                 for gi in range(cw // group_dim)]

        def lead(r0):
            u = _dot(xn_ref[r0:r0 + rb, :], wu_ref[...])
            vn = vs_ref[j - nc, r0:r0 + rb, :].astype(BF16)
            cols = []
            for gi, wmat in enumerate(wmats):
                blocks = [_dot(wmat, vn[r:r + rblk, gi * group_dim:(gi + 1) * group_dim])
                          for r in range(0, rb, rblk)]
                cols.append(_cat(blocks, 0))
            return u, _cat(cols, 1)

        def finish(r0, um):
            u, mixed = um
            bias = _cat([_cat([jnp.broadcast_to(bias_ref[gi, :rblk, :], (rblk, group_dim))]
                              * (rb // rblk), 0) for gi in range(len(wmats))], 1)
            y = _gelu(u) * (mixed + bias)
            o_ref[r0:r0 + rb, :] += _dot(y.astype(BF16), wom_ref[...])

        _pipelined(range(0, tm, rb), lead, finish)

    @pl.when(j >= 2 * nc)
    def _():
        _attention_step(xn_ref, wq_ref, k_ref, v_ref, woh_ref, o_ref, rb, seg, head_dim)


def _mix_gmlp(x, g, w_in, ln_g, ln_b, ws, bias, mem_k, mem_v, w_out, *, layer, slot, tm, rb, seg,
              batch_len, cw, cwh, head_dim, emit_v):
    m, d = x.shape
    dm = ln_g.shape[-1]
    nb, n_mem, dq = mem_k.shape[1:]
    groups, chunk = ws.shape[1:3]
    group_dim = dm // groups
    nseg = tm // seg
    tpb = batch_len // tm if nseg == 1 else 1
    rblk = min(chunk, seg)
    assert seg % rblk == 0 and rb % rblk == 0 and cw % group_dim == 0
    nc, nh = dm // cw, dq // cwh
    gpc = cw // group_dim
    uc = lambda j: jnp.clip(j - nc, 0, nc - 1)
    hc = lambda j: jnp.maximum(j - 2 * nc, 0)
    in_specs = [
        pl.BlockSpec(memory_space=pl.ANY),
        pl.BlockSpec((None, 1, d), lambda i, j: (layer, 0, 0)),
        pl.BlockSpec((None, d, cw), lambda i, j: (slot, 0, nc + jnp.minimum(j, nc - 1))),
        pl.BlockSpec((None, d, cw), lambda i, j: (slot, 0, uc(j))),
        pl.BlockSpec((None, d, cwh), lambda i, j: (slot, 0, 2 * dm // cwh + hc(j))),
        pl.BlockSpec((None, 1, dm), lambda i, j: (slot, 0, 0)),
        pl.BlockSpec((None, 1, dm), lambda i, j: (slot, 0, 0)),
        pl.BlockSpec((None, gpc, chunk, chunk), lambda i, j: (slot, uc(j), 0, 0)),
        pl.BlockSpec((None, gpc, chunk, 1), lambda i, j: (slot, uc(j), 0, 0)),
        pl.BlockSpec((None, nseg, n_mem, cwh), lambda i, j: (layer, i // tpb, 0, hc(j))),
        pl.BlockSpec((None, nseg, n_mem, cwh), lambda i, j: (layer, i // tpb, 0, hc(j))),
        pl.BlockSpec((None, cw, d), lambda i, j: (layer, uc(j), 0)),
        pl.BlockSpec((None, cwh, d), lambda i, j: (layer, dm // cwh + hc(j), 0)),
    ]
    out_specs = [pl.BlockSpec((tm, d), lambda i, j: (i, 0))]
    out_shape = [jax.ShapeDtypeStruct((m, d), F32)]
    if emit_v:
        out_specs.append(pl.BlockSpec((tm, dm), lambda i, j: (i, 0)))
        out_shape.append(jax.ShapeDtypeStruct((m, dm), F32))
    est = (3 * tm * d * 4 + tm * d * 2 + tm * dm * 4 * (3 if emit_v else 1)
           + 2 * (2 * d * cw + d * cwh + (cw + cwh) * d) * 2
           + 4 * nseg * n_mem * cwh * 4 + 8 * rb * cw * 4)
    outs = pl.pallas_call(
        functools.partial(_mix_gmlp_kernel, seg=seg, rb=rb, rblk=rblk, nc=nc, head_dim=head_dim,
                          group_dim=group_dim, emit_v=emit_v),
        grid=(m // tm, 2 * nc + nh),
        in_specs=in_specs,
        out_specs=out_specs,
        out_shape=out_shape,
        scratch_shapes=[pltpu.VMEM((tm, d), BF16), pltpu.VMEM((nc, tm, cw), F32),
                        pltpu.VMEM((tm, d), F32), pltpu.SemaphoreType.DMA(())],
        compiler_params=_compiler_params(est),
        name="mix_gmlp",
    )(x, g, w_in, w_in, w_in, ln_g, ln_b, ws, bias, mem_k, mem_v, w_out, w_out)
    return (outs[0], outs[1]) if emit_v else (outs[0], None)


def _ffn_kernel(x_ref, g_ref, wa_ref, wg_ref, cw_ref, cb_ref, prev_ref, wdn_ref, gf_ref, o_ref,
                buf_ref, xn_ref, carry_ref, u0_ref, u1_ref, *, seg, rb, tiles_per_batch, final_norm):
    i = pl.program_id(0)
    j = pl.program_id(1)
    tm = x_ref.shape[0]
    fc = wdn_ref.shape[0]
    u_refs = (u0_ref, u1_ref)

    first = True if tiles_per_batch == 1 else (i % tiles_per_batch) == 0

    def step(new_tile):
        conv = _RowConv(cw_ref[...], prev_ref, carry_ref, buf_ref, j, first, seg)

        def lead(r0):
            xn = (_init_rows(x_ref, g_ref, o_ref, xn_ref, r0, rb) if new_tile
                  else xn_ref[r0:r0 + rb, :])
            u_ref = u_refs[(r0 // rb) % 2]
            u_ref[:, 0:fc] = _dot(xn, wa_ref[...])
            u_ref[:, fc:2 * fc] = _dot(xn, wg_ref[...])
            return u_ref

        def finish(r0, u_ref):
            z = conv.block(u_ref[...], r0) + cb_ref[...]
            h = _silu(z[:, fc:]) * z[:, :fc]
            o_ref[r0:r0 + rb, :] += _dot(h.astype(BF16), wdn_ref[...])

        _pipelined(range(0, tm, rb), lead, finish)

    pl.when(j == 0)(functools.partial(step, True))
    pl.when(j > 0)(functools.partial(step, False))

    if final_norm:
        @pl.when(j == pl.num_programs(1) - 1)
        def _():
            for r0 in range(0, tm, rb):
                o_ref[r0:r0 + rb, :] = _rms(o_ref[r0:r0 + rb, :], gf_ref[...])


def _ffn(x, g, w_up, conv_w, conv_b, prev, w_down, g_final, *, layer, tm, rb, seg, batch_len, fc,
         final_norm):
    m, d = x.shape
    dff = w_down.shape[1]
    nj = dff // fc
    nseg = tm // seg
    tpb = batch_len // tm if nseg == 1 else 1
    nb = prev.shape[1]
    in_specs = [
        pl.BlockSpec((tm, d), lambda i, j: (i, 0)),
        pl.BlockSpec((None, 1, d), lambda i, j: (layer, 0, 0)),
        pl.BlockSpec((None, d, fc), lambda i, j: (layer, 0, j)),
        pl.BlockSpec((None, d, fc), lambda i, j: (layer, 0, nj + j)),
        pl.BlockSpec((None, 3, 2 * fc), lambda i, j: (layer, 0, j)),
        pl.BlockSpec((None, 1, 2 * fc), lambda i, j: (layer, 0, j)),
        pl.BlockSpec((None, nseg, SUBLANES, 2 * fc), lambda i, j: (layer, i // tpb, 0, j)),
        pl.BlockSpec((None, fc, d), lambda i, j: (layer, j, 0)),
        pl.BlockSpec((1, d), lambda i, j: (0, 0)),
    ]
    out_specs = [
        pl.BlockSpec((tm, d), lambda i, j: (i, 0)),
        pl.BlockSpec((nseg, nj, SUBLANES, 2 * fc), lambda i, j: (i // tpb, 0, 0, 0)),
    ]
    scratch = [pltpu.VMEM((tm, d), BF16), pltpu.VMEM((nj, nseg, SUBLANES, 2 * fc), F32),
               pltpu.VMEM((rb, 2 * fc), F32), pltpu.VMEM((rb, 2 * fc), F32)]
    est = 4 * tm * d * 4 + tm * d * 2 + 2 * (3 * d * fc) * 2 + 10 * rb * 2 * fc * 4
    x_out, buf = pl.pallas_call(
        functools.partial(_ffn_kernel, seg=seg, rb=rb, tiles_per_batch=tpb, final_norm=final_norm),
        grid=(m // tm, nj),
        in_specs=in_specs,
        out_specs=out_specs,
        out_shape=[jax.ShapeDtypeStruct((m, d), F32),
                   jax.ShapeDtypeStruct((nb, nj, SUBLANES, 2 * fc), F32)],
        scratch_shapes=scratch,
        compiler_params=_compiler_params(est),
        name="conv_ffn",
    )(x, g, w_up, w_up, conv_w, conv_b, _pad_history(prev), w_down, g_final)
    return x_out, _unpack_history(buf, prev.shape[-2])


def _memkv_kernel(m_ref, g_ref, wk_ref, wv_ref, ko_ref, vo_ref):
    mn = _rms(m_ref[...], g_ref[0]).astype(BF16)
    ko_ref[0] = _dot(mn, wk_ref[0])
    vo_ref[0] = _dot(mn, wv_ref[0])


def _memkv(mem, g, wk, wv, *, tn):
    rows, d = mem.shape
    depth, _, dq = wk.shape
    shape = jax.ShapeDtypeStruct((depth, rows, dq), F32)
    return pl.pallas_call(
        _memkv_kernel,
        grid=(depth, dq // tn),
        in_specs=[
            pl.BlockSpec((rows, d), lambda l, n: (0, 0)),
            pl.BlockSpec((1, 1, d), lambda l, n: (l, 0, 0)),
            pl.BlockSpec((1, d, tn), lambda l, n: (l, 0, n)),
            pl.BlockSpec((1, d, tn), lambda l, n: (l, 0, n)),
        ],
        out_specs=[pl.BlockSpec((1, rows, tn), lambda l, n: (l, 0, n))] * 2,
        out_shape=[shape, shape],
        compiler_params=pltpu.CompilerParams(dimension_semantics=("arbitrary", "arbitrary")),
        name="memory_kv",
    )(mem, g, wk, wv)


def _group_cols(t, nj, fc):
    lead = t.shape[:-1]
    return t.reshape(lead + (2, nj, fc)).swapaxes(-3, -2).reshape(lead + (2 * nj * fc,))


def _ungroup_cols(t, nj, fc):
    lead = t.shape[:-1]
    return t.reshape(lead + (nj, 2, fc)).swapaxes(-3, -2).reshape(lead + (2 * nj * fc,))


def _tiling(t, tm, rb):
    return dict(tm=tm, rb=rb, seg=min(t, tm), batch_len=t)


def _trunk(x3, mem_k, mem_v, conv_a_prev, ffn_prev, p, *, tiles, emit_v):
    nb, t, d = x3.shape
    x = x3.reshape(nb * t, d)
    depth = p["norm_mix_g"].shape[0]
    conv_a_new, ffn_new, gmlp_v = [], [], []
    mixer = dict(head_dim=p["head_dim"])
    for i in range(depth):
        kind, jx = i % 2, i // 2
        if kind == 0:
            x, buf = _mix_conv(x, p["norm_mix_g"], p["w_in_a"], p["conv_a_w"], conv_a_prev, mem_k,
                               mem_v, p["w_out"], layer=i, slot=jx, cw=p["cw_conv"], cwh=p["cwh_conv"], **mixer,
                               **_tiling(t, *tiles["mix_conv"]))
            conv_a_new.append(buf)
        else:
            x, v_rows = _mix_gmlp(x, p["norm_mix_g"], p["w_in_b"], p["gmlp_norm_g"], p["gmlp_norm_b"],
                                  p["gmlp_ws"], p["gmlp_bias"], mem_k, mem_v, p["w_out"], layer=i,
                                  slot=jx, emit_v=emit_v, cw=p["cw_gmlp"], cwh=p["cwh_gmlp"], **mixer,
                                  **_tiling(t, *tiles["mix_gmlp"]))
            gmlp_v.append(v_rows)
        x, buf = _ffn(x, p["norm_ffn_g"], p["w_up"], p["ffn_conv_w"], p["ffn_conv_b"], ffn_prev,
                      p["w_down"], p["norm_final_g"], layer=i, fc=p["fc"],
                      final_norm=(i == depth - 1), **_tiling(t, *tiles["ffn"]))
        ffn_new.append(_ungroup_cols(buf, p["nj"], p["fc"]))
    y = x.reshape(nb, t, d)
    v_out = jnp.stack([v.reshape(nb, t, -1) for v in gmlp_v]) if emit_v else None
    return y, jnp.stack(conv_a_new), jnp.stack(ffn_new), v_out


def kernel(x_prompt, x_sample, mem_prompt, cache_conv_a, cache_ffn_conv, cache_mem_k, cache_mem_v,
           norm_mix_g, norm_mem_g, w_mem_k, w_mem_v, w_in_a, conv_a_w, w_in_b, gmlp_norm_g,
           gmlp_norm_b, gmlp_ws, gmlp_bias, w_out, norm_ffn_g, w_up, ffn_conv_w, ffn_conv_b, w_down,
           norm_final_g):
    b, s, d = x_prompt.shape
    nb_s, t_s, _ = x_sample.shape
    depth = norm_mix_g.shape[0]
    n_mem = mem_prompt.shape[1]
    heads, head_dim = cache_mem_k.shape[-2:]
    dq = heads * head_dim
    dff = w_down.shape[1]
    fc = 512
    assert dff % fc == 0
    nj = dff // fc

    p = dict(
        norm_mix_g=norm_mix_g[:, None, :], norm_ffn_g=norm_ffn_g[:, None, :],
        norm_final_g=norm_final_g[None],
        w_in_a=w_in_a.astype(BF16), w_in_b=w_in_b.astype(BF16), w_out=w_out.astype(BF16),
        conv_a_w=conv_a_w, gmlp_norm_g=gmlp_norm_g[:, None, :], gmlp_norm_b=gmlp_norm_b[:, None, :],
        gmlp_ws=gmlp_ws, gmlp_bias=gmlp_bias[..., None],
        w_up=w_up.astype(BF16), w_down=w_down.astype(BF16),
        ffn_conv_w=_group_cols(ffn_conv_w, nj, fc),
        ffn_conv_b=_group_cols(ffn_conv_b, nj, fc)[:, None, :],
        fc=fc, nj=nj, cw_conv=512, cw_gmlp=512, cwh_conv=512, cwh_gmlp=256, head_dim=head_dim,
    )

    mk, mv = _memkv(mem_prompt.reshape(b * n_mem, d), norm_mem_g[:, None, :],
                    w_mem_k.astype(BF16), w_mem_v.astype(BF16), tn=512)
    mk = mk.reshape(depth, b, n_mem, dq)
    mv = mv.reshape(depth, b, n_mem, dq)
    conv_a_zero = jnp.zeros((cache_conv_a.shape[0], b) + cache_conv_a.shape[2:], F32)
    ffn_zero = jnp.zeros((depth, b) + cache_ffn_conv.shape[2:], F32)
    y_prompt, conv_a_prompt, ffn_conv_prompt, _ = _trunk(
        x_prompt, mk, mv, conv_a_zero, ffn_zero, p, emit_v=False,
        tiles=dict(mix_conv=(1024, 512), mix_gmlp=(1024, 512), ffn=(1024, 512)))

    y_sample, conv_a_sample, ffn_conv_sample, gmlp_v_sample = _trunk(
        x_sample, cache_mem_k.reshape(depth, nb_s, n_mem, dq),
        cache_mem_v.reshape(depth, nb_s, n_mem, dq), cache_conv_a,
        _group_cols(cache_ffn_conv, nj, fc), p, emit_v=True,
        tiles=dict.fromkeys(("mix_conv", "mix_gmlp", "ffn"), (nb_s * t_s, nb_s * t_s)))

    return (y_prompt, y_sample, conv_a_prompt, ffn_conv_prompt,
            mk.reshape(depth, b, n_mem, heads, head_dim), mv.reshape(depth, b, n_mem, heads, head_dim),
            conv_a_sample, ffn_conv_sample, gmlp_v_sample)
```

```python
import functools

import jax
import jax.numpy as jnp
from jax import lax
from jax.experimental import pallas as pl
from jax.experimental.pallas import tpu as pltpu

EPS = 1e-6
SUBLANES = 8
VMEM_REQUEST_CAP = 62 * 1024 * 1024

F32 = jnp.float32
BF16 = jnp.bfloat16


def _vmem_limit(est_bytes):
    return int(min(VMEM_REQUEST_CAP, max(32 * 1024 * 1024, est_bytes * 5 // 4)))


def _compiler_params(est_bytes):
    return pltpu.CompilerParams(dimension_semantics=("arbitrary", "arbitrary"),
                                vmem_limit_bytes=_vmem_limit(est_bytes))


def _dot(a, b):
    return jnp.dot(a, b, preferred_element_type=F32)


def _rms(x, g):
    return x * lax.rsqrt(jnp.mean(x * x, axis=-1, keepdims=True) + EPS) * g


def _gelu(x):
    return 0.5 * x * (1.0 + lax.erf(x * (2.0 ** -0.5)))


def _silu(x):
    return (0.5 * x) * (1.0 + jnp.tanh(0.5 * x))


def _cat(parts, axis):
    return parts[0] if len(parts) == 1 else jnp.concatenate(parts, axis=axis)


def _segment_spans(row0, nrows, seg):
    spans, pos = [], row0
    while pos < row0 + nrows:
        s, off = divmod(pos, seg)
        n = min(seg - off, row0 + nrows - pos)
        spans.append((s, off, n))
        pos += n
    return spans


def _conv3(u, h0, h1, w):
    s, c = u.shape
    r1 = pltpu.roll(u, 1, axis=0)
    r2 = pltpu.roll(u, 2, axis=0)
    row = lax.broadcasted_iota(jnp.int32, (SUBLANES, c), 0)
    top1 = jnp.where(row == 0, h1, r1[:SUBLANES])
    top2 = jnp.where(row == 0, h0, jnp.where(row == 1, h1, r2[:SUBLANES]))
    if s > SUBLANES:
        r1 = jnp.concatenate([top1, r1[SUBLANES:]], axis=0)
        r2 = jnp.concatenate([top2, r2[SUBLANES:]], axis=0)
    else:
        r1, r2 = top1, top2
    return w[0:1] * r2 + w[1:2] * r1 + w[2:3] * u


class _RowConv:
    def __init__(self, w, prev_ref, carry_ref, buf_ref, slot, first, seg):
        self.w, self.carry_ref, self.buf_ref, self.slot, self.seg = w, carry_ref, buf_ref, slot, seg
        self.tail = None

        def load_prev():
            for s in range(prev_ref.shape[0]):
                carry_ref[slot, s] = prev_ref[s]

        if first is True:
            load_prev()
        else:
            pl.when(first)(load_prev)

    def block(self, u, row0):
        outs, pos = [], 0
        for s, off, n in _segment_spans(row0, u.shape[0], self.seg):
            us = u[pos:pos + n]
            hist = self.carry_ref[self.slot, s] if off == 0 else self.tail
            outs.append(_conv3(us, hist[SUBLANES - 2:SUBLANES - 1], hist[SUBLANES - 1:], self.w))
            self.tail = us[n - SUBLANES:]
            if off + n == self.seg:
                self.carry_ref[self.slot, s] = self.tail
                self.buf_ref[s, self.slot] = self.tail
            pos += n
        return _cat(outs, 0)


def _pad_history(prev):
    pad = [(0, 0)] * prev.ndim
    pad[-2] = (SUBLANES - prev.shape[-2], 0)
    return jnp.pad(prev, pad)


def _unpack_history(buf, rows):
    nb, slots, _, c = buf.shape
    return buf[:, :, SUBLANES - rows:, :].swapaxes(1, 2).reshape(nb, rows, slots * c)


def _attention(q, k_ref, v_ref, row0, seg, head_dim):
    width = q.shape[1]
    scale = head_dim ** -0.5
    spans = _segment_spans(row0, q.shape[0], seg)
    head_cols = [slice(h * head_dim, (h + 1) * head_dim) for h in range(width // head_dim)]
    scores, pos = [], 0
    for s, _, n in spans:
        for cs in head_cols:
            qh = q[pos:pos + n, cs].astype(BF16)
            kh = k_ref[s, :, cs].astype(BF16)
            scores.append(lax.dot_general(qh, kh, (((1,), (1,)), ((), ())),
                                          preferred_element_type=F32) * scale)
        pos += n
    rows = []
    for si, (s, _, n) in enumerate(spans):
        heads = []
        for hi, cs in enumerate(head_cols):
            sc = scores[si * len(head_cols) + hi]
            m = jnp.max(sc, axis=-1, keepdims=True)
            p = jnp.exp(sc - m)
            l = jnp.sum(p, axis=-1, keepdims=True)
            heads.append(_dot(p.astype(BF16), v_ref[s, :, cs].astype(BF16)) / l)
        rows.append(_cat(heads, 1))
    return _cat(rows, 0)


def _pipelined(starts, lead, finish):
    cur = lead(starts[0])
    for k, r0 in enumerate(starts):
        nxt = lead(starts[k + 1]) if k + 1 < len(starts) else None
        finish(r0, cur)
        cur = nxt


def _init_rows(x_ref, g_ref, o_ref, xn_ref, r0, rb):
    x = x_ref[r0:r0 + rb, :]
    xn = _rms(x, g_ref[...]).astype(BF16)
    xn_ref[r0:r0 + rb, :] = xn
    o_ref[r0:r0 + rb, :] = x
    return xn


def _init_tile(x_ref, g_ref, o_ref, xn_ref, rb):
    for r0 in range(0, x_ref.shape[0], rb):
        _init_rows(x_ref, g_ref, o_ref, xn_ref, r0, rb)


def _stream_x_tile(x_hbm, xbuf_ref, sem, i, j):
    tm = xbuf_ref.shape[0]

    def copy(tile):
        return pltpu.make_async_copy(x_hbm.at[pl.ds(pl.multiple_of(tile * tm, tm), tm), :], xbuf_ref, sem)

    @pl.when(jnp.logical_and(i == 0, j == 0))
    def _():
        copy(0).start()

    @pl.when(j == 0)
    def _():
        copy(i).wait()

    @pl.when(jnp.logical_and(j == 1, i + 1 < pl.num_programs(0)))
    def _():
        copy(i + 1).start()


def _attention_step(xn_ref, wq_ref, k_ref, v_ref, woh_ref, o_ref, rb, seg, head_dim):
    def lead(r0):
        return _dot(xn_ref[r0:r0 + rb, :], wq_ref[...])

    def finish(r0, q):
        y = _attention(q, k_ref, v_ref, r0, seg, head_dim)
        o_ref[r0:r0 + rb, :] += _dot(y.astype(BF16), woh_ref[...])

    _pipelined(range(0, xn_ref.shape[0], rb), lead, finish)


def _mix_conv_kernel(x_hbm, g_ref, wb_ref, wc_ref, wh_ref, wq_ref, cw_ref, prev_ref, k_ref, v_ref,
                     wom_ref, woh_ref, o_ref, buf_ref, xn_ref, carry_ref, x_ref, x_sem, *, seg, rb,
                     tiles_per_batch, nc, head_dim):
    i = pl.program_id(0)
    j = pl.program_id(1)
    tm = x_ref.shape[0]
    _stream_x_tile(x_hbm, x_ref, x_sem, i, j)

    def mixer_step(new_tile):
        first = True if tiles_per_batch == 1 else (i % tiles_per_batch) == 0
        conv = _RowConv(cw_ref[...], prev_ref, carry_ref, buf_ref, j, first, seg)

        def lead(r0):
            xn = (_init_rows(x_ref, g_ref, o_ref, xn_ref, r0, rb) if new_tile
                  else xn_ref[r0:r0 + rb, :])
            return _dot(xn, wc_ref[...]), _dot(xn, wh_ref[...]), _dot(xn, wb_ref[...])

        def finish(r0, chb):
            c, h, b = chb
            y = b * conv.block(c * h, r0)
            o_ref[r0:r0 + rb, :] += _dot(y.astype(BF16), wom_ref[...])

        _pipelined(range(0, tm, rb), lead, finish)

    pl.when(j == 0)(functools.partial(mixer_step, True))
    pl.when(jnp.logical_and(j > 0, j < nc))(functools.partial(mixer_step, False))

    @pl.when(j >= nc)
    def _():
        _attention_step(xn_ref, wq_ref, k_ref, v_ref, woh_ref, o_ref, rb, seg, head_dim)


def _mix_conv(x, g, w_in, conv_w, prev, mem_k, mem_v, w_out, *, layer, slot, tm, rb, seg, batch_len,
              cw, cwh, head_dim):
    m, d = x.shape
    dm = conv_w.shape[-1]
    nb, n_mem, dq = mem_k.shape[1:]
    nseg = tm // seg
    tpb = batch_len // tm if nseg == 1 else 1
    nc, nh = dm // cw, dq // cwh
    mc = lambda j: jnp.minimum(j, nc - 1)
    hc = lambda j: jnp.maximum(j - nc, 0)
    in_specs = [
        pl.BlockSpec(memory_space=pl.ANY),
        pl.BlockSpec((None, 1, d), lambda i, j: (layer, 0, 0)),
        pl.BlockSpec((None, d, cw), lambda i, j: (slot, 0, mc(j))),
        pl.BlockSpec((None, d, cw), lambda i, j: (slot, 0, nc + mc(j))),
        pl.BlockSpec((None, d, cw), lambda i, j: (slot, 0, 2 * nc + mc(j))),
        pl.BlockSpec((None, d, cwh), lambda i, j: (slot, 0, 3 * dm // cwh + hc(j))),
        pl.BlockSpec((None, 3, cw), lambda i, j: (slot, 0, mc(j))),
        pl.BlockSpec((None, nseg, SUBLANES, cw), lambda i, j: (slot, i // tpb, 0, mc(j))),
        pl.BlockSpec((None, nseg, n_mem, cwh), lambda i, j: (layer, i // tpb, 0, hc(j))),
        pl.BlockSpec((None, nseg, n_mem, cwh), lambda i, j: (layer, i // tpb, 0, hc(j))),
        pl.BlockSpec((None, cw, d), lambda i, j: (layer, mc(j), 0)),
        pl.BlockSpec((None, cwh, d), lambda i, j: (layer, dm // cwh + hc(j), 0)),
    ]
    out_specs = [
        pl.BlockSpec((tm, d), lambda i, j: (i, 0)),
        pl.BlockSpec((nseg, nc, SUBLANES, cw), lambda i, j: (i // tpb, 0, 0, 0)),
    ]
    scratch = [pltpu.VMEM((tm, d), BF16), pltpu.VMEM((nc, nseg, SUBLANES, cw), F32),
               pltpu.VMEM((tm, d), F32), pltpu.SemaphoreType.DMA(())]
    est = (3 * tm * d * 4 + tm * d * 2 + 2 * (3 * d * cw + d * cwh + (cw + cwh) * d) * 2
           + 4 * nseg * n_mem * cwh * 4 + 8 * rb * cw * 4)
    x_out, buf = pl.pallas_call(
        functools.partial(_mix_conv_kernel, seg=seg, rb=rb, tiles_per_batch=tpb, nc=nc,
                          head_dim=head_dim),
        grid=(m // tm, nc + nh),
        in_specs=in_specs,
        out_specs=out_specs,
        out_shape=[jax.ShapeDtypeStruct((m, d), F32),
                   jax.ShapeDtypeStruct((nb, nc, SUBLANES, cw), F32)],
        scratch_shapes=scratch,
        compiler_params=_compiler_params(est),
        name="mix_conv",
    )(x, g, w_in, w_in, w_in, w_in, conv_w, _pad_history(prev), mem_k, mem_v, w_out, w_out)
    return x_out, _unpack_history(buf, prev.shape[-2])


def _mix_gmlp_kernel(x_hbm, g_ref, wv_ref, wu_ref, wq_ref, lng_ref, lnb_ref, ws_ref, bias_ref, k_ref,
                     v_ref, wom_ref, woh_ref, o_ref, *rest, seg, rb, rblk, nc, head_dim, group_dim,
                     emit_v):
    if emit_v:
        vout_ref, xn_ref, vs_ref, x_ref, x_sem = rest
    else:
        vout_ref = None
        xn_ref, vs_ref, x_ref, x_sem = rest
    j = pl.program_id(1)
    tm = x_ref.shape[0]
    cw = wv_ref.shape[1]
    _stream_x_tile(x_hbm, x_ref, x_sem, pl.program_id(0), j)

    def v_step(new_tile):
        for r0 in range(0, tm, rb):
            xn = (_init_rows(x_ref, g_ref, o_ref, xn_ref, r0, rb) if new_tile
                  else xn_ref[r0:r0 + rb, :])
            vs_ref[j, r0:r0 + rb, :] = _gelu(_dot(xn, wv_ref[...]))

    pl.when(j == 0)(functools.partial(v_step, True))
    pl.when(jnp.logical_and(j > 0, j < nc))(functools.partial(v_step, False))

    def layer_norm_rows(r0):
        width = nc * cw
        rows = slice(r0, r0 + rb)
        tot = vs_ref[0, rows, :].sum(axis=-1, keepdims=True)
        for c in range(1, nc):
            tot += vs_ref[c, rows, :].sum(axis=-1, keepdims=True)
        mean = tot / width
        sq = None
        for c in range(nc):
            dv = vs_ref[c, rows, :] - mean
            part = (dv * dv).sum(axis=-1, keepdims=True)
            sq = part if sq is None else sq + part
        rstd = lax.rsqrt(sq / width + EPS)
        for c in range(nc):
            cs = slice(c * cw, (c + 1) * cw)
            vn = (vs_ref[c, rows, :] - mean) * rstd * lng_ref[:, cs] + lnb_ref[:, cs]
            vs_ref[c, rows, :] = vn
            if emit_v:
                vout_ref[rows, cs] = vn

    def u_step(first_u):
        tri = (lax.broadcasted_iota(jnp.int32, (rblk, rblk), 0)
               >= lax.broadcasted_iota(jnp.int32, (rblk, rblk), 1))
        wmats = [jnp.where(tri, ws_ref[gi, :rblk, :rblk], 0.0).astype(BF16)
                 for gi in range(cw // group_dim)]

        def lead(r0):
            u = _dot(xn_ref[r0:r0 + rb, :], wu_ref[...])
            if first_u:
                layer_norm_rows(r0)
            vn = vs_ref[j - nc, r0:r0 + rb, :].astype(BF16)
            cols = []
            for gi, wmat in enumerate(wmats):
                blocks = [_dot(wmat, vn[r:r + rblk, gi * group_dim:(gi + 1) * group_dim])
                          for r in range(0, rb, rblk)]
                cols.append(_cat(blocks, 0))
            return u, _cat(cols, 1)

        def finish(r0, um):
            u, mixed = um
            bias = _cat([_cat([jnp.broadcast_to(bias_ref[gi, :rblk, :], (rblk, group_dim))]
                              * (rb // rblk), 0) for gi in range(len(wmats))], 1)
            y = _gelu(u) * (mixed + bias)
            o_ref[r0:r0 + rb, :] += _dot(y.astype(BF16), wom_ref[...])

        _pipelined(range(0, tm, rb), lead, finish)

    pl.when(j == nc)(functools.partial(u_step, True))
    pl.when(jnp.logical_and(j > nc, j < 2 * nc))(functools.partial(u_step, False))

    @pl.when(j >= 2 * nc)
    def _():
        _attention_step(xn_ref, wq_ref, k_ref, v_ref, woh_ref, o_ref, rb, seg, head_dim)


def _mix_gmlp(x, g, w_in, ln_g, ln_b, ws, bias, mem_k, mem_v, w_out, *, layer, slot, tm, rb, seg,
              batch_len, cw, cwh, head_dim, emit_v):
    m, d = x.shape
    dm = ln_g.shape[-1]
    nb, n_mem, dq = mem_k.shape[1:]
    groups, chunk = ws.shape[1:3]
    group_dim = dm // groups
    nseg = tm // seg
    tpb = batch_len // tm if nseg == 1 else 1
    rblk = min(chunk, seg)
    assert seg % rblk == 0 and rb % rblk == 0 and cw % group_dim == 0
    nc, nh = dm // cw, dq // cwh
    gpc = cw // group_dim
    uc = lambda j: jnp.clip(j - nc, 0, nc - 1)
    hc = lambda j: jnp.maximum(j - 2 * nc, 0)
    in_specs = [
        pl.BlockSpec(memory_space=pl.ANY),
        pl.BlockSpec((None, 1, d), lambda i, j: (layer, 0, 0)),
        pl.BlockSpec((None, d, cw), lambda i, j: (slot, 0, nc + jnp.minimum(j, nc - 1))),
        pl.BlockSpec((None, d, cw), lambda i, j: (slot, 0, uc(j))),
        pl.BlockSpec((None, d, cwh), lambda i, j: (slot, 0, 2 * dm // cwh + hc(j))),
        pl.BlockSpec((None, 1, dm), lambda i, j: (slot, 0, 0)),
        pl.BlockSpec((None, 1, dm), lambda i, j: (slot, 0, 0)),
        pl.BlockSpec((None, gpc, chunk, chunk), lambda i, j: (slot, uc(j), 0, 0)),
        pl.BlockSpec((None, gpc, chunk, 1), lambda i, j: (slot, uc(j), 0, 0)),
        pl.BlockSpec((None, nseg, n_mem, cwh), lambda i, j: (layer, i // tpb, 0, hc(j))),
        pl.BlockSpec((None, nseg, n_mem, cwh), lambda i, j: (layer, i // tpb, 0, hc(j))),
        pl.BlockSpec((None, cw, d), lambda i, j: (layer, uc(j), 0)),
        pl.BlockSpec((None, cwh, d), lambda i, j: (layer, dm // cwh + hc(j), 0)),
    ]
    out_specs = [pl.BlockSpec((tm, d), lambda i, j: (i, 0))]
    out_shape = [jax.ShapeDtypeStruct((m, d), F32)]
    if emit_v:
        out_specs.append(pl.BlockSpec((tm, dm), lambda i, j: (i, 0)))
        out_shape.append(jax.ShapeDtypeStruct((m, dm), F32))
    est = (3 * tm * d * 4 + tm * d * 2 + tm * dm * 4 * (3 if emit_v else 1)
           + 2 * (2 * d * cw + d * cwh + (cw + cwh) * d) * 2
           + 4 * nseg * n_mem * cwh * 4 + 8 * rb * cw * 4)
    outs = pl.pallas_call(
        functools.partial(_mix_gmlp_kernel, seg=seg, rb=rb, rblk=rblk, nc=nc, head_dim=head_dim,
                          group_dim=group_dim, emit_v=emit_v),
        grid=(m // tm, 2 * nc + nh),
        in_specs=in_specs,
        out_specs=out_specs,
        out_shape=out_shape,
        scratch_shapes=[pltpu.VMEM((tm, d), BF16), pltpu.VMEM((nc, tm, cw), F32),
                        pltpu.VMEM((tm, d), F32), pltpu.SemaphoreType.DMA(())],
        compiler_params=_compiler_params(est),
        name="mix_gmlp",
    )(x, g, w_in, w_in, w_in, ln_g, ln_b, ws, bias, mem_k, mem_v, w_out, w_out)
    return (outs[0], outs[1]) if emit_v else (outs[0], None)


def _ffn_kernel(x_ref, g_ref, wa_ref, wg_ref, cw_ref, cb_ref, prev_ref, wdn_ref, gf_ref, o_ref,
                buf_ref, xn_ref, carry_ref, u0_ref, u1_ref, *, seg, rb, tiles_per_batch, final_norm):
    i = pl.program_id(0)
    j = pl.program_id(1)
    tm = x_ref.shape[0]
    fc = wdn_ref.shape[0]
    u_refs = (u0_ref, u1_ref)

    first = True if tiles_per_batch == 1 else (i % tiles_per_batch) == 0

    def step(new_tile):
        conv = _RowConv(cw_ref[...], prev_ref, carry_ref, buf_ref, j, first, seg)

        def lead(r0):
            xn = (_init_rows(x_ref, g_ref, o_ref, xn_ref, r0, rb) if new_tile
                  else xn_ref[r0:r0 + rb, :])
            u_ref = u_refs[(r0 // rb) % 2]
            u_ref[:, 0:fc] = _dot(xn, wa_ref[...])
            u_ref[:, fc:2 * fc] = _dot(xn, wg_ref[...])
            return u_ref

        def finish(r0, u_ref):
            z = conv.block(u_ref[...], r0) + cb_ref[...]
            h = _silu(z[:, fc:]) * z[:, :fc]
            o_ref[r0:r0 + rb, :] += _dot(h.astype(BF16), wdn_ref[...])

        _pipelined(range(0, tm, rb), lead, finish)

    pl.when(j == 0)(functools.partial(step, True))
    pl.when(j > 0)(functools.partial(step, False))

    if final_norm:
        @pl.when(j == pl.num_programs(1) - 1)
        def _():
            for r0 in range(0, tm, rb):
                o_ref[r0:r0 + rb, :] = _rms(o_ref[r0:r0 + rb, :], gf_ref[...])


def _ffn(x, g, w_up, conv_w, conv_b, prev, w_down, g_final, *, layer, tm, rb, seg, batch_len, fc,
         final_norm):
    m, d = x.shape
    dff = w_down.shape[1]
    nj = dff // fc
    nseg = tm // seg
    tpb = batch_len // tm if nseg == 1 else 1
    nb = prev.shape[1]
    in_specs = [
        pl.BlockSpec((tm, d), lambda i, j: (i, 0)),
        pl.BlockSpec((None, 1, d), lambda i, j: (layer, 0, 0)),
        pl.BlockSpec((None, d, fc), lambda i, j: (layer, 0, j)),
        pl.BlockSpec((None, d, fc), lambda i, j: (layer, 0, nj + j)),
        pl.BlockSpec((None, 3, 2 * fc), lambda i, j: (layer, 0, j)),
        pl.BlockSpec((None, 1, 2 * fc), lambda i, j: (layer, 0, j)),
        pl.BlockSpec((None, nseg, SUBLANES, 2 * fc), lambda i, j: (layer, i // tpb, 0, j)),
        pl.BlockSpec((None, fc, d), lambda i, j: (layer, j, 0)),
        pl.BlockSpec((1, d), lambda i, j: (0, 0)),
    ]
    out_specs = [
        pl.BlockSpec((tm, d), lambda i, j: (i, 0)),
        pl.BlockSpec((nseg, nj, SUBLANES, 2 * fc), lambda i, j: (i // tpb, 0, 0, 0)),
    ]
    scratch = [pltpu.VMEM((tm, d), BF16), pltpu.VMEM((nj, nseg, SUBLANES, 2 * fc), F32),
               pltpu.VMEM((rb, 2 * fc), F32), pltpu.VMEM((rb, 2 * fc), F32)]
    est = 4 * tm * d * 4 + tm * d * 2 + 2 * (3 * d * fc) * 2 + 10 * rb * 2 * fc * 4
    x_out, buf = pl.pallas_call(
        functools.partial(_ffn_kernel, seg=seg, rb=rb, tiles_per_batch=tpb, final_norm=final_norm),
        grid=(m // tm, nj),
        in_specs=in_specs,
        out_specs=out_specs,
        out_shape=[jax.ShapeDtypeStruct((m, d), F32),
                   jax.ShapeDtypeStruct((nb, nj, SUBLANES, 2 * fc), F32)],
        scratch_shapes=scratch,
        compiler_params=_compiler_params(est),
        name="conv_ffn",
    )(x, g, w_up, w_up, conv_w, conv_b, _pad_history(prev), w_down, g_final)
    return x_out, _unpack_history(buf, prev.shape[-2])


def _memkv_kernel(m_ref, g_ref, wk_ref, wv_ref, ko_ref, vo_ref):
    mn = _rms(m_ref[...], g_ref[0]).astype(BF16)
    ko_ref[0] = _dot(mn, wk_ref[0])
    vo_ref[0] = _dot(mn, wv_ref[0])


def _memkv(mem, g, wk, wv, *, tn):
    rows, d = mem.shape
    depth, _, dq = wk.shape
    shape = jax.ShapeDtypeStruct((depth, rows, dq), F32)
    return pl.pallas_call(
        _memkv_kernel,
        grid=(depth, dq // tn),
        in_specs=[
            pl.BlockSpec((rows, d), lambda l, n: (0, 0)),
            pl.BlockSpec((1, 1, d), lambda l, n: (l, 0, 0)),
            pl.BlockSpec((1, d, tn), lambda l, n: (l, 0, n)),
            pl.BlockSpec((1, d, tn), lambda l, n: (l, 0, n)),
        ],
        out_specs=[pl.BlockSpec((1, rows, tn), lambda l, n: (l, 0, n))] * 2,
        out_shape=[shape, shape],
        compiler_params=pltpu.CompilerParams(dimension_semantics=("arbitrary", "arbitrary")),
        name="memory_kv",
    )(mem, g, wk, wv)


def _group_cols(t, nj, fc):
    lead = t.shape[:-1]
    return t.reshape(lead + (2, nj, fc)).swapaxes(-3, -2).reshape(lead + (2 * nj * fc,))


def _ungroup_cols(t, nj, fc):
    lead = t.shape[:-1]
    return t.reshape(lead + (nj, 2, fc)).swapaxes(-3, -2).reshape(lead + (2 * nj * fc,))


def _tiling(t, tm, rb):
    return dict(tm=tm, rb=rb, seg=min(t, tm), batch_len=t)


def _trunk(x3, mem_k, mem_v, conv_a_prev, ffn_prev, p, *, tiles, emit_v):
    nb, t, d = x3.shape
    x = x3.reshape(nb * t, d)
    depth = p["norm_mix_g"].shape[0]
    conv_a_new, ffn_new, gmlp_v = [], [], []
    mixer = dict(head_dim=p["head_dim"])
    for i in range(depth):
        kind, jx = i % 2, i // 2
        if kind == 0:
            x, buf = _mix_conv(x, p["norm_mix_g"], p["w_in_a"], p["conv_a_w"], conv_a_prev, mem_k,
                               mem_v, p["w_out"], layer=i, slot=jx, cw=p["cw_conv"], cwh=p["cwh_conv"], **mixer,
                               **_tiling(t, *tiles["mix_conv"]))
            conv_a_new.append(buf)
        else:
            x, v_rows = _mix_gmlp(x, p["norm_mix_g"], p["w_in_b"], p["gmlp_norm_g"], p["gmlp_norm_b"],
                                  p["gmlp_ws"], p["gmlp_bias"], mem_k, mem_v, p["w_out"], layer=i,
                                  slot=jx, emit_v=emit_v, cw=p["cw_gmlp"], cwh=p["cwh_gmlp"], **mixer,
                                  **_tiling(t, *tiles["mix_gmlp"]))
            gmlp_v.append(v_rows)
        x, buf = _ffn(x, p["norm_ffn_g"], p["w_up"], p["ffn_conv_w"], p["ffn_conv_b"], ffn_prev,
                      p["w_down"], p["norm_final_g"], layer=i, fc=p["fc"],
                      final_norm=(i == depth - 1), **_tiling(t, *tiles["ffn"]))
        ffn_new.append(_ungroup_cols(buf, p["nj"], p["fc"]))
    y = x.reshape(nb, t, d)
    v_out = jnp.stack([v.reshape(nb, t, -1) for v in gmlp_v]) if emit_v else None
    return y, jnp.stack(conv_a_new), jnp.stack(ffn_new), v_out


def kernel(x_prompt, x_sample, mem_prompt, cache_conv_a, cache_ffn_conv, cache_mem_k, cache_mem_v,
           norm_mix_g, norm_mem_g, w_mem_k, w_mem_v, w_in_a, conv_a_w, w_in_b, gmlp_norm_g,
           gmlp_norm_b, gmlp_ws, gmlp_bias, w_out, norm_ffn_g, w_up, ffn_conv_w, ffn_conv_b, w_down,
           norm_final_g):
    b, s, d = x_prompt.shape
    nb_s, t_s, _ = x_sample.shape
    depth = norm_mix_g.shape[0]
    n_mem = mem_prompt.shape[1]
    heads, head_dim = cache_mem_k.shape[-2:]
    dq = heads * head_dim
    dff = w_down.shape[1]
    fc = 512
    assert dff % fc == 0
    nj = dff // fc

    p = dict(
        norm_mix_g=norm_mix_g[:, None, :], norm_ffn_g=norm_ffn_g[:, None, :],
        norm_final_g=norm_final_g[None],
        w_in_a=w_in_a.astype(BF16), w_in_b=w_in_b.astype(BF16), w_out=w_out.astype(BF16),
        conv_a_w=conv_a_w, gmlp_norm_g=gmlp_norm_g[:, None, :], gmlp_norm_b=gmlp_norm_b[:, None, :],
        gmlp_ws=gmlp_ws, gmlp_bias=gmlp_bias[..., None],
        w_up=w_up.astype(BF16), w_down=w_down.astype(BF16),
        ffn_conv_w=_group_cols(ffn_conv_w, nj, fc),
        ffn_conv_b=_group_cols(ffn_conv_b, nj, fc)[:, None, :],
        fc=fc, nj=nj, cw_conv=512, cw_gmlp=512, cwh_conv=512, cwh_gmlp=256, head_dim=head_dim,
    )

    mk, mv = _memkv(mem_prompt.reshape(b * n_mem, d), norm_mem_g[:, None, :],
                    w_mem_k.astype(BF16), w_mem_v.astype(BF16), tn=512)
    mk = mk.reshape(depth, b, n_mem, dq)
    mv = mv.reshape(depth, b, n_mem, dq)
    conv_a_zero = jnp.zeros((cache_conv_a.shape[0], b) + cache_conv_a.shape[2:], F32)
    ffn_zero = jnp.zeros((depth, b) + cache_ffn_conv.shape[2:], F32)
    y_prompt, conv_a_prompt, ffn_conv_prompt, _ = _trunk(
        x_prompt, mk, mv, conv_a_zero, ffn_zero, p, emit_v=False,
        tiles=dict(mix_conv=(1024, 512), mix_gmlp=(1024, 512), ffn=(1024, 512)))

    y_sample, conv_a_sample, ffn_conv_sample, gmlp_v_sample = _trunk(
        x_sample, cache_mem_k.reshape(depth, nb_s, n_mem, dq),
        cache_mem_v.reshape(depth, nb_s, n_mem, dq), cache_conv_a,
        _group_cols(cache_ffn_conv, nj, fc), p, emit_v=True,
        tiles=dict.fromkeys(("mix_conv", "mix_gmlp", "ffn"), (nb_s * t_s, nb_s * t_s)))

    return (y_prompt, y_sample, conv_a_prompt, ffn_conv_prompt,
            mk.reshape(depth, b, n_mem, heads, head_dim), mv.reshape(depth, b, n_mem, heads, head_dim),
            conv_a_sample, ffn_conv_sample, gmlp_v_sample)
```

```python
import functools

import jax
import jax.numpy as jnp
from jax import lax
from jax.experimental import pallas as pl
from jax.experimental.pallas import tpu as pltpu

EPS = 1e-6
SUBLANES = 8
VMEM_REQUEST_CAP = 62 * 1024 * 1024

F32 = jnp.float32
BF16 = jnp.bfloat16


def _vmem_limit(est_bytes):
    return int(min(VMEM_REQUEST_CAP, max(32 * 1024 * 1024, est_bytes * 5 // 4)))


def _compiler_params(est_bytes):
    return pltpu.CompilerParams(dimension_semantics=("arbitrary", "arbitrary"),
                                vmem_limit_bytes=_vmem_limit(est_bytes))


def _dot(a, b):
    return jnp.dot(a, b, preferred_element_type=F32)


def _rms(x, g):
    return x * lax.rsqrt(jnp.mean(x * x, axis=-1, keepdims=True) + EPS) * g


def _gelu(x):
    return 0.5 * x * (1.0 + lax.erf(x * (2.0 ** -0.5)))


def _silu(x):
    return (0.5 * x) * (1.0 + jnp.tanh(0.5 * x))


def _cat(parts, axis):
    return parts[0] if len(parts) == 1 else jnp.concatenate(parts, axis=axis)


def _segment_spans(row0, nrows, seg):
    spans, pos = [], row0
    while pos < row0 + nrows:
        s, off = divmod(pos, seg)
        n = min(seg - off, row0 + nrows - pos)
        spans.append((s, off, n))
        pos += n
    return spans


def _conv3(u, h0, h1, w):
    s, c = u.shape
    r1 = pltpu.roll(u, 1, axis=0)
    r2 = pltpu.roll(u, 2, axis=0)
    row = lax.broadcasted_iota(jnp.int32, (SUBLANES, c), 0)
    top1 = jnp.where(row == 0, h1, r1[:SUBLANES])
    top2 = jnp.where(row == 0, h0, jnp.where(row == 1, h1, r2[:SUBLANES]))
    if s > SUBLANES:
        r1 = jnp.concatenate([top1, r1[SUBLANES:]], axis=0)
        r2 = jnp.concatenate([top2, r2[SUBLANES:]], axis=0)
    else:
        r1, r2 = top1, top2
    return w[0:1] * r2 + w[1:2] * r1 + w[2:3] * u


class _RowConv:
    def __init__(self, w, prev_ref, carry_ref, buf_ref, slot, first, seg):
        self.w, self.carry_ref, self.buf_ref, self.slot, self.seg = w, carry_ref, buf_ref, slot, seg
        self.tail = None

        def load_prev():
            for s in range(prev_ref.shape[0]):
                carry_ref[slot, s] = prev_ref[s]

        if first is True:
            load_prev()
        else:
            pl.when(first)(load_prev)

    def block(self, u, row0):
        outs, pos = [], 0
        for s, off, n in _segment_spans(row0, u.shape[0], self.seg):
            us = u[pos:pos + n]
            hist = self.carry_ref[self.slot, s] if off == 0 else self.tail
            outs.append(_conv3(us, hist[SUBLANES - 2:SUBLANES - 1], hist[SUBLANES - 1:], self.w))
            self.tail = us[n - SUBLANES:]
            if off + n == self.seg:
                self.carry_ref[self.slot, s] = self.tail
                self.buf_ref[s, self.slot] = self.tail
            pos += n
        return _cat(outs, 0)


def _pad_history(prev):
    pad = [(0, 0)] * prev.ndim
    pad[-2] = (SUBLANES - prev.shape[-2], 0)
    return jnp.pad(prev, pad)


def _unpack_history(buf, rows):
    nb, slots, _, c = buf.shape
    return buf[:, :, SUBLANES - rows:, :].swapaxes(1, 2).reshape(nb, rows, slots * c)


def _attention(q, k_ref, v_ref, row0, seg, head_dim):
    width = q.shape[1]
    scale = head_dim ** -0.5
    spans = _segment_spans(row0, q.shape[0], seg)
    head_cols = [slice(h * head_dim, (h + 1) * head_dim) for h in range(width // head_dim)]
    scores, pos = [], 0
    for s, _, n in spans:
        for cs in head_cols:
            qh = q[pos:pos + n, cs].astype(BF16)
            kh = k_ref[s, :, cs].astype(BF16)
            scores.append(lax.dot_general(qh, kh, (((1,), (1,)), ((), ())),
                                          preferred_element_type=F32) * scale)
        pos += n
    rows = []
    for si, (s, _, n) in enumerate(spans):
        heads = []
        for hi, cs in enumerate(head_cols):
            sc = scores[si * len(head_cols) + hi]
            m = jnp.max(sc, axis=-1, keepdims=True)
            p = jnp.exp(sc - m)
            l = jnp.sum(p, axis=-1, keepdims=True)
            heads.append(_dot(p.astype(BF16), v_ref[s, :, cs].astype(BF16)) / l)
        rows.append(_cat(heads, 1))
    return _cat(rows, 0)


def _pipelined(starts, lead, finish):
    cur = lead(starts[0])
    for k, r0 in enumerate(starts):
        nxt = lead(starts[k + 1]) if k + 1 < len(starts) else None
        finish(r0, cur)
        cur = nxt


def _init_rows(x, g_ref, o_ref, xn_ref, r0, rb):
    xn = _rms(x, g_ref[...]).astype(BF16)
    xn_ref[r0:r0 + rb, :] = xn
    o_ref[r0:r0 + rb, :] = x
    return xn


def _init_tile(x_ref, g_ref, o_ref, xn_ref, rb):
    for r0 in range(0, x_ref.shape[0], rb):
        _init_rows(x_ref[r0:r0 + rb, :], g_ref, o_ref, xn_ref, r0, rb)


def _stream_x_tile(x_hbm, xbuf_ref, sem, i, j):
    tm = xbuf_ref.shape[0]

    def copy(tile):
        return pltpu.make_async_copy(x_hbm.at[pl.ds(pl.multiple_of(tile * tm, tm), tm), :], xbuf_ref, sem)

    @pl.when(jnp.logical_and(i == 0, j == 0))
    def _():
        copy(0).start()

    @pl.when(j == 0)
    def _():
        copy(i).wait()

    @pl.when(jnp.logical_and(j == 1, i + 1 < pl.num_programs(0)))
    def _():
        copy(i + 1).start()


def _attention_step(xn_ref, wq_ref, k_ref, v_ref, woh_ref, o_ref, rb, seg, head_dim):
    def lead(r0):
        return _dot(xn_ref[r0:r0 + rb, :], wq_ref[...])

    def finish(r0, q):
        y = _attention(q, k_ref, v_ref, r0, seg, head_dim)
        o_ref[r0:r0 + rb, :] += _dot(y.astype(BF16), woh_ref[...])

    _pipelined(range(0, xn_ref.shape[0], rb), lead, finish)


def _mix_conv_kernel(x_hbm, g_ref, wb_ref, wc_ref, wh_ref, wq_ref, cw_ref, prev_ref, k_ref, v_ref,
                     wom_ref, woh_ref, o_ref, buf_ref, xn_ref, carry_ref, x_ref, x_sem, *, seg, rb,
                     tiles_per_batch, nc, head_dim):
    i = pl.program_id(0)
    j = pl.program_id(1)
    tm = x_ref.shape[0]
    _stream_x_tile(x_hbm, x_ref, x_sem, i, j)

    def mixer_step(new_tile):
        first = True if tiles_per_batch == 1 else (i % tiles_per_batch) == 0
        conv = _RowConv(cw_ref[...], prev_ref, carry_ref, buf_ref, j, first, seg)

        def lead(r0):
            xn = (_init_rows(x_ref[r0:r0 + rb, :], g_ref, o_ref, xn_ref, r0, rb) if new_tile
                  else xn_ref[r0:r0 + rb, :])
            return _dot(xn, wc_ref[...]), _dot(xn, wh_ref[...]), _dot(xn, wb_ref[...])

        def finish(r0, chb):
            c, h, b = chb
            y = b * conv.block(c * h, r0)
            o_ref[r0:r0 + rb, :] += _dot(y.astype(BF16), wom_ref[...])

        _pipelined(range(0, tm, rb), lead, finish)

    pl.when(j == 0)(functools.partial(mixer_step, True))
    pl.when(jnp.logical_and(j > 0, j < nc))(functools.partial(mixer_step, False))

    @pl.when(j >= nc)
    def _():
        _attention_step(xn_ref, wq_ref, k_ref, v_ref, woh_ref, o_ref, rb, seg, head_dim)


def _mix_conv(x, g, w_in, conv_w, prev, mem_k, mem_v, w_out, *, layer, slot, tm, rb, seg, batch_len,
              cw, cwh, head_dim):
    m, d = x.shape
    dm = conv_w.shape[-1]
    nb, n_mem, dq = mem_k.shape[1:]
    nseg = tm // seg
    tpb = batch_len // tm if nseg == 1 else 1
    nc, nh = dm // cw, dq // cwh
    mc = lambda j: jnp.minimum(j, nc - 1)
    hc = lambda j: jnp.maximum(j - nc, 0)
    in_specs = [
        pl.BlockSpec(memory_space=pl.ANY),
        pl.BlockSpec((None, 1, d), lambda i, j: (layer, 0, 0)),
        pl.BlockSpec((None, d, cw), lambda i, j: (slot, 0, mc(j))),
        pl.BlockSpec((None, d, cw), lambda i, j: (slot, 0, nc + mc(j))),
        pl.BlockSpec((None, d, cw), lambda i, j: (slot, 0, 2 * nc + mc(j))),
        pl.BlockSpec((None, d, cwh), lambda i, j: (slot, 0, 3 * dm // cwh + hc(j))),
        pl.BlockSpec((None, 3, cw), lambda i, j: (slot, 0, mc(j))),
        pl.BlockSpec((None, nseg, SUBLANES, cw), lambda i, j: (slot, i // tpb, 0, mc(j))),
        pl.BlockSpec((None, nseg, n_mem, cwh), lambda i, j: (layer, i // tpb, 0, hc(j))),
        pl.BlockSpec((None, nseg, n_mem, cwh), lambda i, j: (layer, i // tpb, 0, hc(j))),
        pl.BlockSpec((None, cw, d), lambda i, j: (layer, mc(j), 0)),
        pl.BlockSpec((None, cwh, d), lambda i, j: (layer, dm // cwh + hc(j), 0)),
    ]
    out_specs = [
        pl.BlockSpec((tm, d), lambda i, j: (i, 0)),
        pl.BlockSpec((nseg, nc, SUBLANES, cw), lambda i, j: (i // tpb, 0, 0, 0)),
    ]
    scratch = [pltpu.VMEM((tm, d), BF16), pltpu.VMEM((nc, nseg, SUBLANES, cw), F32),
               pltpu.VMEM((tm, d), F32), pltpu.SemaphoreType.DMA(())]
    est = (3 * tm * d * 4 + tm * d * 2 + 2 * (3 * d * cw + d * cwh + (cw + cwh) * d) * 2
           + 4 * nseg * n_mem * cwh * 4 + 8 * rb * cw * 4)
    x_out, buf = pl.pallas_call(
        functools.partial(_mix_conv_kernel, seg=seg, rb=rb, tiles_per_batch=tpb, nc=nc,
                          head_dim=head_dim),
        grid=(m // tm, nc + nh),
        in_specs=in_specs,
        out_specs=out_specs,
        out_shape=[jax.ShapeDtypeStruct((m, d), F32),
                   jax.ShapeDtypeStruct((nb, nc, SUBLANES, cw), F32)],
        scratch_shapes=scratch,
        compiler_params=_compiler_params(est),
        name="mix_conv",
    )(x, g, w_in, w_in, w_in, w_in, conv_w, _pad_history(prev), mem_k, mem_v, w_out, w_out)
    return x_out, _unpack_history(buf, prev.shape[-2])


def _mix_gmlp_kernel(x_hbm, g_ref, wv_ref, wu_ref, wq_ref, lng_ref, lnb_ref, ws_ref, bias_ref, k_ref,
                     v_ref, wom_ref, woh_ref, o_ref, *rest, seg, rb, rblk, nc, head_dim, group_dim,
                     emit_v):
    if emit_v:
        vout_ref, xn_ref, vs_ref, x_sem = rest
    else:
        vout_ref = None
        xn_ref, vs_ref, x_sem = rest
    i = pl.program_id(0)
    j = pl.program_id(1)
    tm = xn_ref.shape[0]
    cw = wv_ref.shape[1]

    def x_copies(tile):
        rows = pl.ds(pl.multiple_of(tile * tm, tm), tm)
        return [pltpu.make_async_copy(x_hbm.at[rows, pl.ds(c * cw, cw)], vs_ref.at[c], x_sem.at[c])
                for c in range(nc)]

    @pl.when(jnp.logical_and(i == 0, j == 0))
    def _():
        for cp in x_copies(0):
            cp.start()

    @pl.when(j == 0)
    def _():
        for cp in x_copies(i):
            cp.wait()

    @pl.when(jnp.logical_and(j == 2 * nc, i + 1 < pl.num_programs(0)))
    def _():
        for cp in x_copies(i + 1):
            cp.start()

    def v_step(new_tile):
        for r0 in range(0, tm, rb):
            if new_tile:
                x = _cat([vs_ref[c, r0:r0 + rb, :] for c in range(nc)], 1)
                xn = _init_rows(x, g_ref, o_ref, xn_ref, r0, rb)
                vs_ref[0, r0:r0 + rb, :] = _gelu(_dot(xn, wv_ref[...]))
            else:
                vs_ref[j, r0:r0 + rb, :] = _gelu(_dot(xn_ref[r0:r0 + rb, :], wv_ref[...]))

    pl.when(j == 0)(functools.partial(v_step, True))
    pl.when(jnp.logical_and(j > 0, j < nc))(functools.partial(v_step, False))

    def layer_norm_rows(r0):
        width = nc * cw
        rows = slice(r0, r0 + rb)
        tot = vs_ref[0, rows, :].sum(axis=-1, keepdims=True)
        for c in range(1, nc):
            tot += vs_ref[c, rows, :].sum(axis=-1, keepdims=True)
        mean = tot / width
        sq = None
        for c in range(nc):
            dv = vs_ref[c, rows, :] - mean
            part = (dv * dv).sum(axis=-1, keepdims=True)
            sq = part if sq is None else sq + part
        rstd = lax.rsqrt(sq / width + EPS)
        for c in range(nc):
            cs = slice(c * cw, (c + 1) * cw)
            vn = (vs_ref[c, rows, :] - mean) * rstd * lng_ref[:, cs] + lnb_ref[:, cs]
            vs_ref[c, rows, :] = vn
            if emit_v:
                vout_ref[rows, cs] = vn

    def u_step(first_u):
        tri = (lax.broadcasted_iota(jnp.int32, (rblk, rblk), 0)
               >= lax.broadcasted_iota(jnp.int32, (rblk, rblk), 1))
        wmats = [jnp.where(tri, ws_ref[gi, :rblk, :rblk], 0.0).astype(BF16)
                 for gi in range(cw // group_dim)]

        def lead(r0):
            u = _dot(xn_ref[r0:r0 + rb, :], wu_ref[...])
            if first_u:
                layer_norm_rows(r0)
            vn = vs_ref[j - nc, r0:r0 + rb, :].astype(BF16)
            cols = []
            for gi, wmat in enumerate(wmats):
                blocks = [_dot(wmat, vn[r:r + rblk, gi * group_dim:(gi + 1) * group_dim])
                          for r in range(0, rb, rblk)]
                cols.append(_cat(blocks, 0))
            return u, _cat(cols, 1)

        def finish(r0, um):
            u, mixed = um
            bias = _cat([_cat([jnp.broadcast_to(bias_ref[gi, :rblk, :], (rblk, group_dim))]
                              * (rb // rblk), 0) for gi in range(len(wmats))], 1)
            y = _gelu(u) * (mixed + bias)
            o_ref[r0:r0 + rb, :] += _dot(y.astype(BF16), wom_ref[...])

        _pipelined(range(0, tm, rb), lead, finish)

    pl.when(j == nc)(functools.partial(u_step, True))
    pl.when(jnp.logical_and(j > nc, j < 2 * nc))(functools.partial(u_step, False))

    @pl.when(j >= 2 * nc)
    def _():
        _attention_step(xn_ref, wq_ref, k_ref, v_ref, woh_ref, o_ref, rb, seg, head_dim)


def _mix_gmlp(x, g, w_in, ln_g, ln_b, ws, bias, mem_k, mem_v, w_out, *, layer, slot, tm, rb, seg,
              batch_len, cw, cwh, head_dim, emit_v):
    m, d = x.shape
    dm = ln_g.shape[-1]
    nb, n_mem, dq = mem_k.shape[1:]
    groups, chunk = ws.shape[1:3]
    group_dim = dm // groups
    nseg = tm // seg
    tpb = batch_len // tm if nseg == 1 else 1
    rblk = min(chunk, seg)
    assert seg % rblk == 0 and rb % rblk == 0 and cw % group_dim == 0
    nc, nh = dm // cw, dq // cwh
    gpc = cw // group_dim
    uc = lambda j: jnp.clip(j - nc, 0, nc - 1)
    hc = lambda j: jnp.maximum(j - 2 * nc, 0)
    in_specs = [
        pl.BlockSpec(memory_space=pl.ANY),
        pl.BlockSpec((None, 1, d), lambda i, j: (layer, 0, 0)),
        pl.BlockSpec((None, d, cw), lambda i, j: (slot, 0, nc + jnp.minimum(j, nc - 1))),
        pl.BlockSpec((None, d, cw), lambda i, j: (slot, 0, uc(j))),
        pl.BlockSpec((None, d, cwh), lambda i, j: (slot, 0, 2 * dm // cwh + hc(j))),
        pl.BlockSpec((None, 1, dm), lambda i, j: (slot, 0, 0)),
        pl.BlockSpec((None, 1, dm), lambda i, j: (slot, 0, 0)),
        pl.BlockSpec((None, gpc, chunk, chunk), lambda i, j: (slot, uc(j), 0, 0)),
        pl.BlockSpec((None, gpc, chunk, 1), lambda i, j: (slot, uc(j), 0, 0)),
        pl.BlockSpec((None, nseg, n_mem, cwh), lambda i, j: (layer, i // tpb, 0, hc(j))),
        pl.BlockSpec((None, nseg, n_mem, cwh), lambda i, j: (layer, i // tpb, 0, hc(j))),
        pl.BlockSpec((None, cw, d), lambda i, j: (layer, uc(j), 0)),
        pl.BlockSpec((None, cwh, d), lambda i, j: (layer, dm // cwh + hc(j), 0)),
    ]
    out_specs = [pl.BlockSpec((tm, d), lambda i, j: (i, 0))]
    out_shape = [jax.ShapeDtypeStruct((m, d), F32)]
    if emit_v:
        out_specs.append(pl.BlockSpec((tm, dm), lambda i, j: (i, 0)))
        out_shape.append(jax.ShapeDtypeStruct((m, dm), F32))
    assert dm == d and nh >= 1
    est = (2 * tm * d * 4 + tm * d * 2 + tm * dm * 4 * (3 if emit_v else 1)
           + 2 * (2 * d * cw + d * cwh + (cw + cwh) * d) * 2
           + 4 * nseg * n_mem * cwh * 4 + 8 * rb * cw * 4)
    outs = pl.pallas_call(
        functools.partial(_mix_gmlp_kernel, seg=seg, rb=rb, rblk=rblk, nc=nc, head_dim=head_dim,
                          group_dim=group_dim, emit_v=emit_v),
        grid=(m // tm, 2 * nc + nh),
        in_specs=in_specs,
        out_specs=out_specs,
        out_shape=out_shape,
        scratch_shapes=[pltpu.VMEM((tm, d), BF16), pltpu.VMEM((nc, tm, cw), F32),
                        pltpu.SemaphoreType.DMA((nc,))],
        compiler_params=_compiler_params(est),
        name="mix_gmlp",
    )(x, g, w_in, w_in, w_in, ln_g, ln_b, ws, bias, mem_k, mem_v, w_out, w_out)
    return (outs[0], outs[1]) if emit_v else (outs[0], None)


def _ffn_kernel(x_ref, g_ref, wa_ref, wg_ref, cw_ref, cb_ref, prev_ref, wdn_ref, gf_ref, o_ref,
                buf_ref, xn_ref, carry_ref, u0_ref, u1_ref, *, seg, rb, tiles_per_batch, final_norm):
    i = pl.program_id(0)
    j = pl.program_id(1)
    tm = x_ref.shape[0]
    fc = wdn_ref.shape[0]
    u_refs = (u0_ref, u1_ref)

    first = True if tiles_per_batch == 1 else (i % tiles_per_batch) == 0

    def step(new_tile):
        conv = _RowConv(cw_ref[...], prev_ref, carry_ref, buf_ref, j, first, seg)

        def lead(r0):
            xn = (_init_rows(x_ref[r0:r0 + rb, :], g_ref, o_ref, xn_ref, r0, rb) if new_tile
                  else xn_ref[r0:r0 + rb, :])
            u_ref = u_refs[(r0 // rb) % 2]
            u_ref[:, 0:fc] = _dot(xn, wa_ref[...])
            u_ref[:, fc:2 * fc] = _dot(xn, wg_ref[...])
            return u_ref

        def finish(r0, u_ref):
            z = conv.block(u_ref[...], r0) + cb_ref[...]
            h = _silu(z[:, fc:]) * z[:, :fc]
            o_ref[r0:r0 + rb, :] += _dot(h.astype(BF16), wdn_ref[...])

        _pipelined(range(0, tm, rb), lead, finish)

    pl.when(j == 0)(functools.partial(step, True))
    pl.when(j > 0)(functools.partial(step, False))

    if final_norm:
        @pl.when(j == pl.num_programs(1) - 1)
        def _():
            for r0 in range(0, tm, rb):
                o_ref[r0:r0 + rb, :] = _rms(o_ref[r0:r0 + rb, :], gf_ref[...])


def _ffn(x, g, w_up, conv_w, conv_b, prev, w_down, g_final, *, layer, tm, rb, seg, batch_len, fc,
         final_norm):
    m, d = x.shape
    dff = w_down.shape[1]
    nj = dff // fc
    nseg = tm // seg
    tpb = batch_len // tm if nseg == 1 else 1
    nb = prev.shape[1]
    in_specs = [
        pl.BlockSpec((tm, d), lambda i, j: (i, 0)),
        pl.BlockSpec((None, 1, d), lambda i, j: (layer, 0, 0)),
        pl.BlockSpec((None, d, fc), lambda i, j: (layer, 0, j)),
        pl.BlockSpec((None, d, fc), lambda i, j: (layer, 0, nj + j)),
        pl.BlockSpec((None, 3, 2 * fc), lambda i, j: (layer, 0, j)),
        pl.BlockSpec((None, 1, 2 * fc), lambda i, j: (layer, 0, j)),
        pl.BlockSpec((None, nseg, SUBLANES, 2 * fc), lambda i, j: (layer, i // tpb, 0, j)),
        pl.BlockSpec((None, fc, d), lambda i, j: (layer, j, 0)),
        pl.BlockSpec((1, d), lambda i, j: (0, 0)),
    ]
    out_specs = [
        pl.BlockSpec((tm, d), lambda i, j: (i, 0)),
        pl.BlockSpec((nseg, nj, SUBLANES, 2 * fc), lambda i, j: (i // tpb, 0, 0, 0)),
    ]
    scratch = [pltpu.VMEM((tm, d), BF16), pltpu.VMEM((nj, nseg, SUBLANES, 2 * fc), F32),
               pltpu.VMEM((rb, 2 * fc), F32), pltpu.VMEM((rb, 2 * fc), F32)]
    est = 4 * tm * d * 4 + tm * d * 2 + 2 * (3 * d * fc) * 2 + 10 * rb * 2 * fc * 4
    x_out, buf = pl.pallas_call(
        functools.partial(_ffn_kernel, seg=seg, rb=rb, tiles_per_batch=tpb, final_norm=final_norm),
        grid=(m // tm, nj),
        in_specs=in_specs,
        out_specs=out_specs,
        out_shape=[jax.ShapeDtypeStruct((m, d), F32),
                   jax.ShapeDtypeStruct((nb, nj, SUBLANES, 2 * fc), F32)],
        scratch_shapes=scratch,
        compiler_params=_compiler_params(est),
        name="conv_ffn",
    )(x, g, w_up, w_up, conv_w, conv_b, _pad_history(prev), w_down, g_final)
    return x_out, _unpack_history(buf, prev.shape[-2])


def _memkv_kernel(m_ref, g_ref, wk_ref, wv_ref, ko_ref, vo_ref):
    mn = _rms(m_ref[...], g_ref[0]).astype(BF16)
    ko_ref[0] = _dot(mn, wk_ref[0])
    vo_ref[0] = _dot(mn, wv_ref[0])


def _memkv(mem, g, wk, wv, *, tn):
    rows, d = mem.shape
    depth, _, dq = wk.shape
    shape = jax.ShapeDtypeStruct((depth, rows, dq), F32)
    return pl.pallas_call(
        _memkv_kernel,
        grid=(depth, dq // tn),
        in_specs=[
            pl.BlockSpec((rows, d), lambda l, n: (0, 0)),
            pl.BlockSpec((1, 1, d), lambda l, n: (l, 0, 0)),
            pl.BlockSpec((1, d, tn), lambda l, n: (l, 0, n)),
            pl.BlockSpec((1, d, tn), lambda l, n: (l, 0, n)),
        ],
        out_specs=[pl.BlockSpec((1, rows, tn), lambda l, n: (l, 0, n))] * 2,
        out_shape=[shape, shape],
        compiler_params=pltpu.CompilerParams(dimension_semantics=("arbitrary", "arbitrary")),
        name="memory_kv",
    )(mem, g, wk, wv)


def _group_cols(t, nj, fc):
    lead = t.shape[:-1]
    return t.reshape(lead + (2, nj, fc)).swapaxes(-3, -2).reshape(lead + (2 * nj * fc,))


def _ungroup_cols(t, nj, fc):
    lead = t.shape[:-1]
    return t.reshape(lead + (nj, 2, fc)).swapaxes(-3, -2).reshape(lead + (2 * nj * fc,))


def _tiling(t, tm, rb):
    return dict(tm=tm, rb=rb, seg=min(t, tm), batch_len=t)


def _trunk(x3, mem_k, mem_v, conv_a_prev, ffn_prev, p, *, tiles, emit_v):
    nb, t, d = x3.shape
    x = x3.reshape(nb * t, d)
    depth = p["norm_mix_g"].shape[0]
    conv_a_new, ffn_new, gmlp_v = [], [], []
    mixer = dict(head_dim=p["head_dim"])
    for i in range(depth):
        kind, jx = i % 2, i // 2
        if kind == 0:
            x, buf = _mix_conv(x, p["norm_mix_g"], p["w_in_a"], p["conv_a_w"], conv_a_prev, mem_k,
                               mem_v, p["w_out"], layer=i, slot=jx, cw=p["cw_conv"], cwh=p["cwh_conv"], **mixer,
                               **_tiling(t, *tiles["mix_conv"]))
            conv_a_new.append(buf)
        else:
            x, v_rows = _mix_gmlp(x, p["norm_mix_g"], p["w_in_b"], p["gmlp_norm_g"], p["gmlp_norm_b"],
                                  p["gmlp_ws"], p["gmlp_bias"], mem_k, mem_v, p["w_out"], layer=i,
                                  slot=jx, emit_v=emit_v, cw=p["cw_gmlp"], cwh=p["cwh_gmlp"], **mixer,
                                  **_tiling(t, *tiles["mix_gmlp"]))
            gmlp_v.append(v_rows)
        x, buf = _ffn(x, p["norm_ffn_g"], p["w_up"], p["ffn_conv_w"], p["ffn_conv_b"], ffn_prev,
                      p["w_down"], p["norm_final_g"], layer=i, fc=p["fc"],
                      final_norm=(i == depth - 1), **_tiling(t, *tiles["ffn"]))
        ffn_new.append(_ungroup_cols(buf, p["nj"], p["fc"]))
    y = x.reshape(nb, t, d)
    v_out = jnp.stack([v.reshape(nb, t, -1) for v in gmlp_v]) if emit_v else None
    return y, jnp.stack(conv_a_new), jnp.stack(ffn_new), v_out


def kernel(x_prompt, x_sample, mem_prompt, cache_conv_a, cache_ffn_conv, cache_mem_k, cache_mem_v,
           norm_mix_g, norm_mem_g, w_mem_k, w_mem_v, w_in_a, conv_a_w, w_in_b, gmlp_norm_g,
           gmlp_norm_b, gmlp_ws, gmlp_bias, w_out, norm_ffn_g, w_up, ffn_conv_w, ffn_conv_b, w_down,
           norm_final_g):
    b, s, d = x_prompt.shape
    nb_s, t_s, _ = x_sample.shape
    depth = norm_mix_g.shape[0]
    n_mem = mem_prompt.shape[1]
    heads, head_dim = cache_mem_k.shape[-2:]
    dq = heads * head_dim
    dff = w_down.shape[1]
    fc = 512
    assert dff % fc == 0
    nj = dff // fc

    p = dict(
        norm_mix_g=norm_mix_g[:, None, :], norm_ffn_g=norm_ffn_g[:, None, :],
        norm_final_g=norm_final_g[None],
        w_in_a=w_in_a.astype(BF16), w_in_b=w_in_b.astype(BF16), w_out=w_out.astype(BF16),
        conv_a_w=conv_a_w, gmlp_norm_g=gmlp_norm_g[:, None, :], gmlp_norm_b=gmlp_norm_b[:, None, :],
        gmlp_ws=gmlp_ws, gmlp_bias=gmlp_bias[..., None],
        w_up=w_up.astype(BF16), w_down=w_down.astype(BF16),
        ffn_conv_w=_group_cols(ffn_conv_w, nj, fc),
        ffn_conv_b=_group_cols(ffn_conv_b, nj, fc)[:, None, :],
        fc=fc, nj=nj, cw_conv=512, cw_gmlp=512, cwh_conv=512, cwh_gmlp=512, head_dim=head_dim,
    )

    mk, mv = _memkv(mem_prompt.reshape(b * n_mem, d), norm_mem_g[:, None, :],
                    w_mem_k.astype(BF16), w_mem_v.astype(BF16), tn=512)
    mk = mk.reshape(depth, b, n_mem, dq)
    mv = mv.reshape(depth, b, n_mem, dq)
    conv_a_zero = jnp.zeros((cache_conv_a.shape[0], b) + cache_conv_a.shape[2:], F32)
    ffn_zero = jnp.zeros((depth, b) + cache_ffn_conv.shape[2:], F32)
    y_prompt, conv_a_prompt, ffn_conv_prompt, _ = _trunk(
        x_prompt, mk, mv, conv_a_zero, ffn_zero, p, emit_v=False,
        tiles=dict(mix_conv=(1024, 512), mix_gmlp=(1024, 512), ffn=(1024, 512)))

    y_sample, conv_a_sample, ffn_conv_sample, gmlp_v_sample = _trunk(
        x_sample, cache_mem_k.reshape(depth, nb_s, n_mem, dq),
        cache_mem_v.reshape(depth, nb_s, n_mem, dq), cache_conv_a,
        _group_cols(cache_ffn_conv, nj, fc), p, emit_v=True,
        tiles=dict.fromkeys(("mix_conv", "mix_gmlp", "ffn"), (nb_s * t_s, nb_s * t_s)))

    return (y_prompt, y_sample, conv_a_prompt, ffn_conv_prompt,
            mk.reshape(depth, b, n_mem, heads, head_dim), mv.reshape(depth, b, n_mem, heads, head_dim),
            conv_a_sample, ffn_conv_sample, gmlp_v_sample)
```

```python
import functools

import jax
import jax.numpy as jnp
from jax import lax
from jax.experimental import pallas as pl
from jax.experimental.pallas import tpu as pltpu

EPS = 1e-6
SUBLANES = 8
VMEM_REQUEST_CAP = 62 * 1024 * 1024

F32 = jnp.float32
BF16 = jnp.bfloat16


def _vmem_limit(est_bytes):
    return int(min(VMEM_REQUEST_CAP, max(32 * 1024 * 1024, est_bytes * 5 // 4)))


def _compiler_params(est_bytes):
    return pltpu.CompilerParams(dimension_semantics=("arbitrary", "arbitrary"),
                                vmem_limit_bytes=_vmem_limit(est_bytes))


def _dot(a, b):
    return jnp.dot(a, b, preferred_element_type=F32)


def _rms(x, g):
    return x * lax.rsqrt(jnp.mean(x * x, axis=-1, keepdims=True) + EPS) * g


def _gelu(x):
    return 0.5 * x * (1.0 + lax.erf(x * (2.0 ** -0.5)))


def _silu(x):
    return (0.5 * x) * (1.0 + jnp.tanh(0.5 * x))


def _cat(parts, axis):
    return parts[0] if len(parts) == 1 else jnp.concatenate(parts, axis=axis)


def _segment_spans(row0, nrows, seg):
    spans, pos = [], row0
    while pos < row0 + nrows:
        s, off = divmod(pos, seg)
        n = min(seg - off, row0 + nrows - pos)
        spans.append((s, off, n))
        pos += n
    return spans


def _conv3(u, h0, h1, w):
    s, c = u.shape
    r1 = pltpu.roll(u, 1, axis=0)
    r2 = pltpu.roll(u, 2, axis=0)
    row = lax.broadcasted_iota(jnp.int32, (SUBLANES, c), 0)
    top1 = jnp.where(row == 0, h1, r1[:SUBLANES])
    top2 = jnp.where(row == 0, h0, jnp.where(row == 1, h1, r2[:SUBLANES]))
    if s > SUBLANES:
        r1 = jnp.concatenate([top1, r1[SUBLANES:]], axis=0)
        r2 = jnp.concatenate([top2, r2[SUBLANES:]], axis=0)
    else:
        r1, r2 = top1, top2
    return w[0:1] * r2 + w[1:2] * r1 + w[2:3] * u


class _RowConv:
    def __init__(self, w, prev_ref, carry_ref, buf_ref, slot, first, seg):
        self.w, self.carry_ref, self.buf_ref, self.slot, self.seg = w, carry_ref, buf_ref, slot, seg
        self.tail = None

        def load_prev():
            for s in range(prev_ref.shape[0]):
                carry_ref[slot, s] = prev_ref[s]

        if first is True:
            load_prev()
        else:
            pl.when(first)(load_prev)

    def block(self, u, row0):
        outs, pos = [], 0
        for s, off, n in _segment_spans(row0, u.shape[0], self.seg):
            us = u[pos:pos + n]
            hist = self.carry_ref[self.slot, s] if off == 0 else self.tail
            outs.append(_conv3(us, hist[SUBLANES - 2:SUBLANES - 1], hist[SUBLANES - 1:], self.w))
            self.tail = us[n - SUBLANES:]
            if off + n == self.seg:
                self.carry_ref[self.slot, s] = self.tail
                self.buf_ref[s, self.slot] = self.tail
            pos += n
        return _cat(outs, 0)


def _pad_history(prev):
    pad = [(0, 0)] * prev.ndim
    pad[-2] = (SUBLANES - prev.shape[-2], 0)
    return jnp.pad(prev, pad)


def _unpack_history(buf, rows):
    nb, slots, _, c = buf.shape
    return buf[:, :, SUBLANES - rows:, :].swapaxes(1, 2).reshape(nb, rows, slots * c)


def _attention(q, k_ref, v_ref, row0, seg, head_dim):
    width = q.shape[1]
    scale = head_dim ** -0.5
    spans = _segment_spans(row0, q.shape[0], seg)
    head_cols = [slice(h * head_dim, (h + 1) * head_dim) for h in range(width // head_dim)]
    scores, pos = [], 0
    for s, _, n in spans:
        for cs in head_cols:
            qh = q[pos:pos + n, cs].astype(BF16)
            kh = k_ref[s, :, cs].astype(BF16)
            scores.append(lax.dot_general(qh, kh, (((1,), (1,)), ((), ())),
                                          preferred_element_type=F32) * scale)
        pos += n
    rows = []
    for si, (s, _, n) in enumerate(spans):
        heads = []
        for hi, cs in enumerate(head_cols):
            sc = scores[si * len(head_cols) + hi]
            m = jnp.max(sc, axis=-1, keepdims=True)
            p = jnp.exp(sc - m)
            l = jnp.sum(p, axis=-1, keepdims=True)
            heads.append(_dot(p.astype(BF16), v_ref[s, :, cs].astype(BF16)) / l)
        rows.append(_cat(heads, 1))
    return _cat(rows, 0)


def _pipelined(starts, lead, finish):
    cur = lead(starts[0])
    for k, r0 in enumerate(starts):
        nxt = lead(starts[k + 1]) if k + 1 < len(starts) else None
        finish(r0, cur)
        cur = nxt


def _init_rows(x, g_ref, o_ref, xn_ref, r0, rb):
    xn = _rms(x, g_ref[...]).astype(BF16)
    xn_ref[r0:r0 + rb, :] = xn
    o_ref[r0:r0 + rb, :] = x
    return xn


def _stream_x_tile(x_hbm, xbuf_ref, sem, i, j):
    tm = xbuf_ref.shape[0]

    def copy(tile):
        return pltpu.make_async_copy(x_hbm.at[pl.ds(pl.multiple_of(tile * tm, tm), tm), :], xbuf_ref, sem)

    @pl.when(jnp.logical_and(i == 0, j == 0))
    def _():
        copy(0).start()

    @pl.when(j == 0)
    def _():
        copy(i).wait()

    @pl.when(jnp.logical_and(j == 1, i + 1 < pl.num_programs(0)))
    def _():
        copy(i + 1).start()


def _attention_step(xn_ref, wq_ref, k_ref, v_ref, woh_ref, o_ref, rb, seg, head_dim):
    def lead(r0):
        return _dot(xn_ref[r0:r0 + rb, :], wq_ref[...])

    def finish(r0, q):
        y = _attention(q, k_ref, v_ref, r0, seg, head_dim)
        o_ref[r0:r0 + rb, :] += _dot(y.astype(BF16), woh_ref[...])

    _pipelined(range(0, xn_ref.shape[0], rb), lead, finish)


def _mix_conv_kernel(x_hbm, g_ref, wb_ref, wc_ref, wh_ref, wq_ref, cw_ref, prev_ref, k_ref, v_ref,
                     wom_ref, woh_ref, o_ref, buf_ref, xn_ref, carry_ref, x_ref, x_sem, *, seg, rb,
                     tiles_per_batch, nc, head_dim):
    i = pl.program_id(0)
    j = pl.program_id(1)
    tm = x_ref.shape[0]
    _stream_x_tile(x_hbm, x_ref, x_sem, i, j)

    def mixer_step(new_tile):
        first = True if tiles_per_batch == 1 else (i % tiles_per_batch) == 0
        conv = _RowConv(cw_ref[...], prev_ref, carry_ref, buf_ref, j, first, seg)

        def lead(r0):
            xn = (_init_rows(x_ref[r0:r0 + rb, :], g_ref, o_ref, xn_ref, r0, rb) if new_tile
                  else xn_ref[r0:r0 + rb, :])
            return _dot(xn, wc_ref[...]), _dot(xn, wh_ref[...]), _dot(xn, wb_ref[...])

        def finish(r0, chb):
            c, h, b = chb
            y = b * conv.block(c * h, r0)
            o_ref[r0:r0 + rb, :] += _dot(y.astype(BF16), wom_ref[...])

        _pipelined(range(0, tm, rb), lead, finish)

    pl.when(j == 0)(functools.partial(mixer_step, True))
    pl.when(jnp.logical_and(j > 0, j < nc))(functools.partial(mixer_step, False))

    @pl.when(j >= nc)
    def _():
        _attention_step(xn_ref, wq_ref, k_ref, v_ref, woh_ref, o_ref, rb, seg, head_dim)


def _mix_conv(x, g, w_in, conv_w, prev, mem_k, mem_v, w_out, *, layer, slot, tm, rb, seg, batch_len,
              cw, cwh, head_dim):
    m, d = x.shape
    dm = conv_w.shape[-1]
    nb, n_mem, dq = mem_k.shape[1:]
    nseg = tm // seg
    tpb = batch_len // tm if nseg == 1 else 1
    nc, nh = dm // cw, dq // cwh
    mc = lambda j: jnp.minimum(j, nc - 1)
    hc = lambda j: jnp.maximum(j - nc, 0)
    in_specs = [
        pl.BlockSpec(memory_space=pl.ANY),
        pl.BlockSpec((None, 1, d), lambda i, j: (layer, 0, 0)),
        pl.BlockSpec((None, d, cw), lambda i, j: (slot, 0, mc(j))),
        pl.BlockSpec((None, d, cw), lambda i, j: (slot, 0, nc + mc(j))),
        pl.BlockSpec((None, d, cw), lambda i, j: (slot, 0, 2 * nc + mc(j))),
        pl.BlockSpec((None, d, cwh), lambda i, j: (slot, 0, 3 * dm // cwh + hc(j))),
        pl.BlockSpec((None, 3, cw), lambda i, j: (slot, 0, mc(j))),
        pl.BlockSpec((None, nseg, SUBLANES, cw), lambda i, j: (slot, i // tpb, 0, mc(j))),
        pl.BlockSpec((None, nseg, n_mem, cwh), lambda i, j: (layer, i // tpb, 0, hc(j))),
        pl.BlockSpec((None, nseg, n_mem, cwh), lambda i, j: (layer, i // tpb, 0, hc(j))),
        pl.BlockSpec((None, cw, d), lambda i, j: (layer, mc(j), 0)),
        pl.BlockSpec((None, cwh, d), lambda i, j: (layer, dm // cwh + hc(j), 0)),
    ]
    out_specs = [
        pl.BlockSpec((tm, d), lambda i, j: (i, 0)),
        pl.BlockSpec((nseg, nc, SUBLANES, cw), lambda i, j: (i // tpb, 0, 0, 0)),
    ]
    scratch = [pltpu.VMEM((tm, d), BF16), pltpu.VMEM((nc, nseg, SUBLANES, cw), F32),
               pltpu.VMEM((tm, d), F32), pltpu.SemaphoreType.DMA(())]
    est = (3 * tm * d * 4 + tm * d * 2 + 2 * (3 * d * cw + d * cwh + (cw + cwh) * d) * 2
           + 4 * nseg * n_mem * cwh * 4 + 8 * rb * cw * 4)
    x_out, buf = pl.pallas_call(
        functools.partial(_mix_conv_kernel, seg=seg, rb=rb, tiles_per_batch=tpb, nc=nc,
                          head_dim=head_dim),
        grid=(m // tm, nc + nh),
        in_specs=in_specs,
        out_specs=out_specs,
        out_shape=[jax.ShapeDtypeStruct((m, d), F32),
                   jax.ShapeDtypeStruct((nb, nc, SUBLANES, cw), F32)],
        scratch_shapes=scratch,
        compiler_params=_compiler_params(est),
        name="mix_conv",
    )(x, g, w_in, w_in, w_in, w_in, conv_w, _pad_history(prev), mem_k, mem_v, w_out, w_out)
    return x_out, _unpack_history(buf, prev.shape[-2])


def _mix_gmlp_kernel(x_hbm, g_ref, wv_ref, wu_ref, wq_ref, lng_ref, lnb_ref, ws_ref, bias_ref, k_ref,
                     v_ref, wom_ref, woh_ref, o_ref, *rest, seg, rb, rblk, nc, head_dim, group_dim,
                     emit_v):
    if emit_v:
        vout_ref, xn_ref, vs_ref, x_sem = rest
    else:
        vout_ref = None
        xn_ref, vs_ref, x_sem = rest
    i = pl.program_id(0)
    j = pl.program_id(1)
    tm = xn_ref.shape[0]
    cw = wv_ref.shape[1]

    def x_copies(tile):
        rows = pl.ds(pl.multiple_of(tile * tm, tm), tm)
        return [pltpu.make_async_copy(x_hbm.at[rows, pl.ds(c * cw, cw)], vs_ref.at[c], x_sem.at[c])
                for c in range(nc)]

    @pl.when(jnp.logical_and(i == 0, j == 0))
    def _():
        for cp in x_copies(0):
            cp.start()

    @pl.when(j == 0)
    def _():
        for cp in x_copies(i):
            cp.wait()

    @pl.when(jnp.logical_and(j == 2 * nc, i + 1 < pl.num_programs(0)))
    def _():
        for cp in x_copies(i + 1):
            cp.start()

    def v_step(new_tile):
        for r0 in range(0, tm, rb):
            if new_tile:
                x = _cat([vs_ref[c, r0:r0 + rb, :] for c in range(nc)], 1)
                xn = _init_rows(x, g_ref, o_ref, xn_ref, r0, rb)
                vs_ref[0, r0:r0 + rb, :] = _gelu(_dot(xn, wv_ref[...]))
            else:
                vs_ref[j, r0:r0 + rb, :] = _gelu(_dot(xn_ref[r0:r0 + rb, :], wv_ref[...]))

    pl.when(j == 0)(functools.partial(v_step, True))
    pl.when(jnp.logical_and(j > 0, j < nc))(functools.partial(v_step, False))

    def layer_norm_rows(r0):
        width = nc * cw
        rows = slice(r0, r0 + rb)
        tot = vs_ref[0, rows, :].sum(axis=-1, keepdims=True)
        for c in range(1, nc):
            tot += vs_ref[c, rows, :].sum(axis=-1, keepdims=True)
        mean = tot / width
        sq = None
        for c in range(nc):
            dv = vs_ref[c, rows, :] - mean
            part = (dv * dv).sum(axis=-1, keepdims=True)
            sq = part if sq is None else sq + part
        rstd = lax.rsqrt(sq / width + EPS)
        for c in range(nc):
            cs = slice(c * cw, (c + 1) * cw)
            vn = (vs_ref[c, rows, :] - mean) * rstd * lng_ref[:, cs] + lnb_ref[:, cs]
            vs_ref[c, rows, :] = vn
            if emit_v:
                vout_ref[rows, cs] = vn

    def u_step(first_u):
        tri = (lax.broadcasted_iota(jnp.int32, (rblk, rblk), 0)
               >= lax.broadcasted_iota(jnp.int32, (rblk, rblk), 1))
        wmats = [jnp.where(tri, ws_ref[gi, :rblk, :rblk], 0.0).astype(BF16)
                 for gi in range(cw // group_dim)]

        def lead(r0):
            u = _dot(xn_ref[r0:r0 + rb, :], wu_ref[...])
            if first_u:
                layer_norm_rows(r0)
            vn = vs_ref[j - nc, r0:r0 + rb, :].astype(BF16)
            cols = []
            for gi, wmat in enumerate(wmats):
                blocks = [_dot(wmat, vn[r:r + rblk, gi * group_dim:(gi + 1) * group_dim])
                          for r in range(0, rb, rblk)]
                cols.append(_cat(blocks, 0))
            return u, _cat(cols, 1)

        def finish(r0, um):
            u, mixed = um
            bias = _cat([_cat([jnp.broadcast_to(bias_ref[gi, :rblk, :], (rblk, group_dim))]
                              * (rb // rblk), 0) for gi in range(len(wmats))], 1)
            y = _gelu(u) * (mixed + bias)
            o_ref[r0:r0 + rb, :] += _dot(y.astype(BF16), wom_ref[...])

        _pipelined(range(0, tm, rb), lead, finish)

    pl.when(j == nc)(functools.partial(u_step, True))
    pl.when(jnp.logical_and(j > nc, j < 2 * nc))(functools.partial(u_step, False))

    @pl.when(j >= 2 * nc)
    def _():
        _attention_step(xn_ref, wq_ref, k_ref, v_ref, woh_ref, o_ref, rb, seg, head_dim)


def _mix_gmlp(x, g, w_in, ln_g, ln_b, ws, bias, mem_k, mem_v, w_out, *, layer, slot, tm, rb, seg,
              batch_len, cw, cwh, head_dim, emit_v):
    m, d = x.shape
    dm = ln_g.shape[-1]
    nb, n_mem, dq = mem_k.shape[1:]
    groups, chunk = ws.shape[1:3]
    group_dim = dm // groups
    nseg = tm // seg
    tpb = batch_len // tm if nseg == 1 else 1
    rblk = min(chunk, seg)
    assert seg % rblk == 0 and rb % rblk == 0 and cw % group_dim == 0
    nc, nh = dm // cw, dq // cwh
    gpc = cw // group_dim
    uc = lambda j: jnp.clip(j - nc, 0, nc - 1)
    hc = lambda j: jnp.maximum(j - 2 * nc, 0)
    in_specs = [
        pl.BlockSpec(memory_space=pl.ANY),
        pl.BlockSpec((None, 1, d), lambda i, j: (layer, 0, 0)),
        pl.BlockSpec((None, d, cw), lambda i, j: (slot, 0, nc + jnp.minimum(j, nc - 1))),
        pl.BlockSpec((None, d, cw), lambda i, j: (slot, 0, uc(j))),
        pl.BlockSpec((None, d, cwh), lambda i, j: (slot, 0, 2 * dm // cwh + hc(j))),
        pl.BlockSpec((None, 1, dm), lambda i, j: (slot, 0, 0)),
        pl.BlockSpec((None, 1, dm), lambda i, j: (slot, 0, 0)),
        pl.BlockSpec((None, gpc, chunk, chunk), lambda i, j: (slot, uc(j), 0, 0)),
        pl.BlockSpec((None, gpc, chunk, 1), lambda i, j: (slot, uc(j), 0, 0)),
        pl.BlockSpec((None, nseg, n_mem, cwh), lambda i, j: (layer, i // tpb, 0, hc(j))),
        pl.BlockSpec((None, nseg, n_mem, cwh), lambda i, j: (layer, i // tpb, 0, hc(j))),
        pl.BlockSpec((None, cw, d), lambda i, j: (layer, uc(j), 0)),
        pl.BlockSpec((None, cwh, d), lambda i, j: (layer, dm // cwh + hc(j), 0)),
    ]
    out_specs = [pl.BlockSpec((tm, d), lambda i, j: (i, 0))]
    out_shape = [jax.ShapeDtypeStruct((m, d), F32)]
    if emit_v:
        out_specs.append(pl.BlockSpec((tm, dm), lambda i, j: (i, 0)))
        out_shape.append(jax.ShapeDtypeStruct((m, dm), F32))
    assert dm == d and nh >= 1
    est = (2 * tm * d * 4 + tm * d * 2 + tm * dm * 4 * (3 if emit_v else 1)
           + 2 * (2 * d * cw + d * cwh + (cw + cwh) * d) * 2
           + 4 * nseg * n_mem * cwh * 4 + 8 * rb * cw * 4)
    outs = pl.pallas_call(
        functools.partial(_mix_gmlp_kernel, seg=seg, rb=rb, rblk=rblk, nc=nc, head_dim=head_dim,
                          group_dim=group_dim, emit_v=emit_v),
        grid=(m // tm, 2 * nc + nh),
        in_specs=in_specs,
        out_specs=out_specs,
        out_shape=out_shape,
        scratch_shapes=[pltpu.VMEM((tm, d), BF16), pltpu.VMEM((nc, tm, cw), F32),
                        pltpu.SemaphoreType.DMA((nc,))],
        compiler_params=_compiler_params(est),
        name="mix_gmlp",
    )(x, g, w_in, w_in, w_in, ln_g, ln_b, ws, bias, mem_k, mem_v, w_out, w_out)
    return (outs[0], outs[1]) if emit_v else (outs[0], None)


def _ffn_kernel(x_ref, g_ref, wa_ref, wg_ref, cw_ref, cb_ref, prev_ref, wdn_ref, gf_ref, o_ref,
                buf_ref, xn_ref, carry_ref, u0_ref, u1_ref, *, seg, rb, tiles_per_batch, final_norm):
    i = pl.program_id(0)
    j = pl.program_id(1)
    tm = x_ref.shape[0]
    fc = wdn_ref.shape[0]
    u_refs = (u0_ref, u1_ref)

    first = True if tiles_per_batch == 1 else (i % tiles_per_batch) == 0

    def step(new_tile, normed_out=False):
        conv = _RowConv(cw_ref[...], prev_ref, carry_ref, buf_ref, j, first, seg)

        def lead(r0):
            xn = (_init_rows(x_ref[r0:r0 + rb, :], g_ref, o_ref, xn_ref, r0, rb) if new_tile
                  else xn_ref[r0:r0 + rb, :])
            u_ref = u_refs[(r0 // rb) % 2]
            u_ref[:, 0:fc] = _dot(xn, wa_ref[...])
            u_ref[:, fc:2 * fc] = _dot(xn, wg_ref[...])
            return u_ref

        def finish(r0, u_ref):
            z = conv.block(u_ref[...], r0) + cb_ref[...]
            h = _silu(z[:, fc:]) * z[:, :fc]
            acc = o_ref[r0:r0 + rb, :] + _dot(h.astype(BF16), wdn_ref[...])
            o_ref[r0:r0 + rb, :] = _rms(acc, gf_ref[...]) if normed_out else acc

        _pipelined(range(0, tm, rb), lead, finish)

    last = pl.num_programs(1) - 1
    pl.when(j == 0)(functools.partial(step, True))
    if final_norm:
        pl.when(jnp.logical_and(j > 0, j < last))(functools.partial(step, False))
        pl.when(jnp.logical_and(j > 0, j == last))(functools.partial(step, False, True))
    else:
        pl.when(j > 0)(functools.partial(step, False))


def _ffn(x, g, w_up, conv_w, conv_b, prev, w_down, g_final, *, layer, tm, rb, seg, batch_len, fc,
         final_norm):
    m, d = x.shape
    dff = w_down.shape[1]
    nj = dff // fc
    assert nj > 1
    nseg = tm // seg
    tpb = batch_len // tm if nseg == 1 else 1
    nb = prev.shape[1]
    in_specs = [
        pl.BlockSpec((tm, d), lambda i, j: (i, 0)),
        pl.BlockSpec((None, 1, d), lambda i, j: (layer, 0, 0)),
        pl.BlockSpec((None, d, fc), lambda i, j: (layer, 0, j)),
        pl.BlockSpec((None, d, fc), lambda i, j: (layer, 0, nj + j)),
        pl.BlockSpec((None, 3, 2 * fc), lambda i, j: (layer, 0, j)),
        pl.BlockSpec((None, 1, 2 * fc), lambda i, j: (layer, 0, j)),
        pl.BlockSpec((None, nseg, SUBLANES, 2 * fc), lambda i, j: (layer, i // tpb, 0, j)),
        pl.BlockSpec((None, fc, d), lambda i, j: (layer, j, 0)),
        pl.BlockSpec((1, d), lambda i, j: (0, 0)),
    ]
    out_specs = [
        pl.BlockSpec((tm, d), lambda i, j: (i, 0)),
        pl.BlockSpec((nseg, nj, SUBLANES, 2 * fc), lambda i, j: (i // tpb, 0, 0, 0)),
    ]
    scratch = [pltpu.VMEM((tm, d), BF16), pltpu.VMEM((nj, nseg, SUBLANES, 2 * fc), F32),
               pltpu.VMEM((rb, 2 * fc), F32), pltpu.VMEM((rb, 2 * fc), F32)]
    est = 4 * tm * d * 4 + tm * d * 2 + 2 * (3 * d * fc) * 2 + 10 * rb * 2 * fc * 4
    x_out, buf = pl.pallas_call(
        functools.partial(_ffn_kernel, seg=seg, rb=rb, tiles_per_batch=tpb, final_norm=final_norm),
        grid=(m // tm, nj),
        in_specs=in_specs,
        out_specs=out_specs,
        out_shape=[jax.ShapeDtypeStruct((m, d), F32),
                   jax.ShapeDtypeStruct((nb, nj, SUBLANES, 2 * fc), F32)],
        scratch_shapes=scratch,
        compiler_params=_compiler_params(est),
        name="conv_ffn",
    )(x, g, w_up, w_up, conv_w, conv_b, _pad_history(prev), w_down, g_final)
    return x_out, _unpack_history(buf, prev.shape[-2])


def _memkv_kernel(m_ref, g_ref, wk_ref, wv_ref, ko_ref, vo_ref):
    mn = _rms(m_ref[...], g_ref[0]).astype(BF16)
    ko_ref[0] = _dot(mn, wk_ref[0])
    vo_ref[0] = _dot(mn, wv_ref[0])


def _memkv(mem, g, wk, wv, *, tn):
    rows, d = mem.shape
    depth, _, dq = wk.shape
    shape = jax.ShapeDtypeStruct((depth, rows, dq), F32)
    return pl.pallas_call(
        _memkv_kernel,
        grid=(depth, dq // tn),
        in_specs=[
            pl.BlockSpec((rows, d), lambda l, n: (0, 0)),
            pl.BlockSpec((1, 1, d), lambda l, n: (l, 0, 0)),
            pl.BlockSpec((1, d, tn), lambda l, n: (l, 0, n)),
            pl.BlockSpec((1, d, tn), lambda l, n: (l, 0, n)),
        ],
        out_specs=[pl.BlockSpec((1, rows, tn), lambda l, n: (l, 0, n))] * 2,
        out_shape=[shape, shape],
        compiler_params=pltpu.CompilerParams(dimension_semantics=("arbitrary", "arbitrary")),
        name="memory_kv",
    )(mem, g, wk, wv)


def _group_cols(t, nj, fc):
    lead = t.shape[:-1]
    return t.reshape(lead + (2, nj, fc)).swapaxes(-3, -2).reshape(lead + (2 * nj * fc,))


def _ungroup_cols(t, nj, fc):
    lead = t.shape[:-1]
    return t.reshape(lead + (nj, 2, fc)).swapaxes(-3, -2).reshape(lead + (2 * nj * fc,))


def _tiling(t, tm, rb):
    return dict(tm=tm, rb=rb, seg=min(t, tm), batch_len=t)


def _trunk(x3, mem_k, mem_v, conv_a_prev, ffn_prev, p, *, tiles, emit_v):
    nb, t, d = x3.shape
    x = x3.reshape(nb * t, d)
    depth = p["norm_mix_g"].shape[0]
    conv_a_new, ffn_new, gmlp_v = [], [], []
    mixer = dict(head_dim=p["head_dim"])
    for i in range(depth):
        kind, jx = i % 2, i // 2
        if kind == 0:
            x, buf = _mix_conv(x, p["norm_mix_g"], p["w_in_a"], p["conv_a_w"], conv_a_prev, mem_k,
                               mem_v, p["w_out"], layer=i, slot=jx, cw=p["cw_conv"], cwh=p["cwh_conv"], **mixer,
                               **_tiling(t, *tiles["mix_conv"]))
            conv_a_new.append(buf)
        else:
            x, v_rows = _mix_gmlp(x, p["norm_mix_g"], p["w_in_b"], p["gmlp_norm_g"], p["gmlp_norm_b"],
                                  p["gmlp_ws"], p["gmlp_bias"], mem_k, mem_v, p["w_out"], layer=i,
                                  slot=jx, emit_v=emit_v, cw=p["cw_gmlp"], cwh=p["cwh_gmlp"], **mixer,
                                  **_tiling(t, *tiles["mix_gmlp"]))
            gmlp_v.append(v_rows)
        x, buf = _ffn(x, p["norm_ffn_g"], p["w_up"], p["ffn_conv_w"], p["ffn_conv_b"], ffn_prev,
                      p["w_down"], p["norm_final_g"], layer=i, fc=p["fc"],
                      final_norm=(i == depth - 1), **_tiling(t, *tiles["ffn"]))
        ffn_new.append(_ungroup_cols(buf, p["nj"], p["fc"]))
    y = x.reshape(nb, t, d)
    v_out = jnp.stack([v.reshape(nb, t, -1) for v in gmlp_v]) if emit_v else None
    return y, jnp.stack(conv_a_new), jnp.stack(ffn_new), v_out


def kernel(x_prompt, x_sample, mem_prompt, cache_conv_a, cache_ffn_conv, cache_mem_k, cache_mem_v,
           norm_mix_g, norm_mem_g, w_mem_k, w_mem_v, w_in_a, conv_a_w, w_in_b, gmlp_norm_g,
           gmlp_norm_b, gmlp_ws, gmlp_bias, w_out, norm_ffn_g, w_up, ffn_conv_w, ffn_conv_b, w_down,
           norm_final_g):
    b, s, d = x_prompt.shape
    nb_s, t_s, _ = x_sample.shape
    depth = norm_mix_g.shape[0]
    n_mem = mem_prompt.shape[1]
    heads, head_dim = cache_mem_k.shape[-2:]
    dq = heads * head_dim
    dff = w_down.shape[1]
    fc = 512
    assert dff % fc == 0
    nj = dff // fc

    p = dict(
        norm_mix_g=norm_mix_g[:, None, :], norm_ffn_g=norm_ffn_g[:, None, :],
        norm_final_g=norm_final_g[None],
        w_in_a=w_in_a.astype(BF16), w_in_b=w_in_b.astype(BF16), w_out=w_out.astype(BF16),
        conv_a_w=conv_a_w, gmlp_norm_g=gmlp_norm_g[:, None, :], gmlp_norm_b=gmlp_norm_b[:, None, :],
        gmlp_ws=gmlp_ws, gmlp_bias=gmlp_bias[..., None],
        w_up=w_up.astype(BF16), w_down=w_down.astype(BF16),
        ffn_conv_w=_group_cols(ffn_conv_w, nj, fc),
        ffn_conv_b=_group_cols(ffn_conv_b, nj, fc)[:, None, :],
        fc=fc, nj=nj, cw_conv=512, cw_gmlp=512, cwh_conv=512, cwh_gmlp=512, head_dim=head_dim,
    )

    mk, mv = _memkv(mem_prompt.reshape(b * n_mem, d), norm_mem_g[:, None, :],
                    w_mem_k.astype(BF16), w_mem_v.astype(BF16), tn=512)
    mk = mk.reshape(depth, b, n_mem, dq)
    mv = mv.reshape(depth, b, n_mem, dq)
    conv_a_zero = jnp.zeros((cache_conv_a.shape[0], b) + cache_conv_a.shape[2:], F32)
    ffn_zero = jnp.zeros((depth, b) + cache_ffn_conv.shape[2:], F32)
    y_prompt, conv_a_prompt, ffn_conv_prompt, _ = _trunk(
        x_prompt, mk, mv, conv_a_zero, ffn_zero, p, emit_v=False,
        tiles=dict(mix_conv=(1024, 512), mix_gmlp=(1024, 512), ffn=(1024, 512)))

    y_sample, conv_a_sample, ffn_conv_sample, gmlp_v_sample = _trunk(
        x_sample, cache_mem_k.reshape(depth, nb_s, n_mem, dq),
        cache_mem_v.reshape(depth, nb_s, n_mem, dq), cache_conv_a,
        _group_cols(cache_ffn_conv, nj, fc), p, emit_v=True,
        tiles=dict.fromkeys(("mix_conv", "mix_gmlp", "ffn"), (nb_s * t_s, nb_s * t_s)))

    return (y_prompt, y_sample, conv_a_prompt, ffn_conv_prompt,
            mk.reshape(depth, b, n_mem, heads, head_dim), mv.reshape(depth, b, n_mem, heads, head_dim),
            conv_a_sample, ffn_conv_sample, gmlp_v_sample)
```

```python
import functools

import jax
import jax.numpy as jnp
from jax import lax
from jax.experimental import pallas as pl
from jax.experimental.pallas import tpu as pltpu

EPS = 1e-6
SUBLANES = 8
VMEM_REQUEST_CAP = 62 * 1024 * 1024

F32 = jnp.float32
BF16 = jnp.bfloat16


def _vmem_limit(est_bytes):
    return int(min(VMEM_REQUEST_CAP, max(32 * 1024 * 1024, est_bytes * 5 // 4)))


def _compiler_params(est_bytes):
    return pltpu.CompilerParams(dimension_semantics=("arbitrary", "arbitrary"),
                                vmem_limit_bytes=_vmem_limit(est_bytes))


def _dot(a, b):
    return jnp.dot(a, b, preferred_element_type=F32)


def _rms(x, g):
    return x * lax.rsqrt(jnp.mean(x * x, axis=-1, keepdims=True) + EPS) * g


def _gelu(x):
    return 0.5 * x * (1.0 + lax.erf(x * (2.0 ** -0.5)))


def _silu(x):
    return (0.5 * x) * (1.0 + jnp.tanh(0.5 * x))


def _cat(parts, axis):
    return parts[0] if len(parts) == 1 else jnp.concatenate(parts, axis=axis)


def _segment_spans(row0, nrows, seg):
    spans, pos = [], row0
    while pos < row0 + nrows:
        s, off = divmod(pos, seg)
        n = min(seg - off, row0 + nrows - pos)
        spans.append((s, off, n))
        pos += n
    return spans


def _conv3(u, h0, h1, w):
    s, c = u.shape
    r1 = pltpu.roll(u, 1, axis=0)
    r2 = pltpu.roll(u, 2, axis=0)
    row = lax.broadcasted_iota(jnp.int32, (SUBLANES, c), 0)
    top1 = jnp.where(row == 0, h1, r1[:SUBLANES])
    top2 = jnp.where(row == 0, h0, jnp.where(row == 1, h1, r2[:SUBLANES]))
    if s > SUBLANES:
        r1 = jnp.concatenate([top1, r1[SUBLANES:]], axis=0)
        r2 = jnp.concatenate([top2, r2[SUBLANES:]], axis=0)
    else:
        r1, r2 = top1, top2
    return w[0:1] * r2 + w[1:2] * r1 + w[2:3] * u


class _RowConv:
    def __init__(self, w, prev_ref, carry_ref, buf_ref, slot, first, seg):
        self.w, self.carry_ref, self.buf_ref, self.slot, self.seg = w, carry_ref, buf_ref, slot, seg
        self.tail = None

        def load_prev():
            for s in range(prev_ref.shape[0]):
                carry_ref[slot, s] = prev_ref[s]

        if first is True:
            load_prev()
        else:
            pl.when(first)(load_prev)

    def block(self, u, row0):
        outs, pos = [], 0
        for s, off, n in _segment_spans(row0, u.shape[0], self.seg):
            us = u[pos:pos + n]
            hist = self.carry_ref[self.slot, s] if off == 0 else self.tail
            outs.append(_conv3(us, hist[SUBLANES - 2:SUBLANES - 1], hist[SUBLANES - 1:], self.w))
            self.tail = us[n - SUBLANES:]
            if off + n == self.seg:
                self.carry_ref[self.slot, s] = self.tail
                self.buf_ref[s, self.slot] = self.tail
            pos += n
        return _cat(outs, 0)


def _pad_history(prev):
    pad = [(0, 0)] * prev.ndim
    pad[-2] = (SUBLANES - prev.shape[-2], 0)
    return jnp.pad(prev, pad)


def _unpack_history(buf, rows):
    nb, slots, _, c = buf.shape
    return buf[:, :, SUBLANES - rows:, :].swapaxes(1, 2).reshape(nb, rows, slots * c)


def _attention(q, k_ref, v_ref, row0, seg, head_dim):
    width = q.shape[1]
    scale = head_dim ** -0.5
    spans = _segment_spans(row0, q.shape[0], seg)
    head_cols = [slice(h * head_dim, (h + 1) * head_dim) for h in range(width // head_dim)]
    scores, pos = [], 0
    for s, _, n in spans:
        for cs in head_cols:
            qh = q[pos:pos + n, cs].astype(BF16)
            kh = k_ref[s, :, cs].astype(BF16)
            scores.append(lax.dot_general(qh, kh, (((1,), (1,)), ((), ())),
                                          preferred_element_type=F32) * scale)
        pos += n
    rows = []
    for si, (s, _, n) in enumerate(spans):
        heads = []
        for hi, cs in enumerate(head_cols):
            sc = scores[si * len(head_cols) + hi]
            m = jnp.max(sc, axis=-1, keepdims=True)
            p = jnp.exp(sc - m)
            l = jnp.sum(p, axis=-1, keepdims=True)
            heads.append(_dot(p.astype(BF16), v_ref[s, :, cs].astype(BF16)) / l)
        rows.append(_cat(heads, 1))
    return _cat(rows, 0)


def _pipelined(starts, lead, finish):
    cur = lead(starts[0])
    for k, r0 in enumerate(starts):
        nxt = lead(starts[k + 1]) if k + 1 < len(starts) else None
        finish(r0, cur)
        cur = nxt


def _init_rows(x, g_ref, o_ref, xn_ref, r0, rb):
    xn = _rms(x, g_ref[...]).astype(BF16)
    xn_ref[r0:r0 + rb, :] = xn
    o_ref[r0:r0 + rb, :] = x
    return xn


def _stream_x_tile(x_hbm, xbuf_ref, sem, i, j):
    tm = xbuf_ref.shape[0]

    def copy(tile):
        return pltpu.make_async_copy(x_hbm.at[pl.ds(pl.multiple_of(tile * tm, tm), tm), :], xbuf_ref, sem)

    @pl.when(jnp.logical_and(i == 0, j == 0))
    def _():
        copy(0).start()

    @pl.when(j == 0)
    def _():
        copy(i).wait()

    @pl.when(jnp.logical_and(j == 1, i + 1 < pl.num_programs(0)))
    def _():
        copy(i + 1).start()


def _attention_step(xn_ref, wq_ref, k_ref, v_ref, woh_ref, o_ref, rb, seg, head_dim):
    def lead(r0):
        return _dot(xn_ref[r0:r0 + rb, :], wq_ref[...])

    def finish(r0, q):
        y = _attention(q, k_ref, v_ref, r0, seg, head_dim)
        o_ref[r0:r0 + rb, :] += _dot(y.astype(BF16), woh_ref[...])

    _pipelined(range(0, xn_ref.shape[0], rb), lead, finish)


def _mix_conv_kernel(x_hbm, g_ref, wb_ref, wc_ref, wh_ref, wq_ref, cw_ref, prev_ref, k_ref, v_ref,
                     wom_ref, woh_ref, o_ref, buf_ref, xn_ref, carry_ref, x_ref, x_sem, *, seg, rb,
                     tiles_per_batch, nc, head_dim):
    i = pl.program_id(0)
    j = pl.program_id(1)
    tm = x_ref.shape[0]
    _stream_x_tile(x_hbm, x_ref, x_sem, i, j)

    def mixer_step(new_tile):
        first = True if tiles_per_batch == 1 else (i % tiles_per_batch) == 0
        conv = _RowConv(cw_ref[...], prev_ref, carry_ref, buf_ref, j, first, seg)

        def lead(r0):
            xn = (_init_rows(x_ref[r0:r0 + rb, :], g_ref, o_ref, xn_ref, r0, rb) if new_tile
                  else xn_ref[r0:r0 + rb, :])
            return _dot(xn, wc_ref[...]), _dot(xn, wh_ref[...]), _dot(xn, wb_ref[...])

        def finish(r0, chb):
            c, h, b = chb
            y = b * conv.block(c * h, r0)
            o_ref[r0:r0 + rb, :] += _dot(y.astype(BF16), wom_ref[...])

        _pipelined(range(0, tm, rb), lead, finish)

    pl.when(j == 0)(functools.partial(mixer_step, True))
    pl.when(jnp.logical_and(j > 0, j < nc))(functools.partial(mixer_step, False))

    @pl.when(j >= nc)
    def _():
        _attention_step(xn_ref, wq_ref, k_ref, v_ref, woh_ref, o_ref, rb, seg, head_dim)


def _mix_conv(x, g, w_in, conv_w, prev, mem_k, mem_v, w_out, *, layer, slot, tm, rb, seg, batch_len,
              cw, cwh, head_dim):
    m, d = x.shape
    dm = conv_w.shape[-1]
    nb, n_mem, dq = mem_k.shape[1:]
    nseg = tm // seg
    tpb = batch_len // tm if nseg == 1 else 1
    nc, nh = dm // cw, dq // cwh
    mc = lambda j: jnp.minimum(j, nc - 1)
    hc = lambda j: jnp.maximum(j - nc, 0)
    in_specs = [
        pl.BlockSpec(memory_space=pl.ANY),
        pl.BlockSpec((None, 1, d), lambda i, j: (layer, 0, 0)),
        pl.BlockSpec((None, d, cw), lambda i, j: (slot, 0, mc(j))),
        pl.BlockSpec((None, d, cw), lambda i, j: (slot, 0, nc + mc(j))),
        pl.BlockSpec((None, d, cw), lambda i, j: (slot, 0, 2 * nc + mc(j))),
        pl.BlockSpec((None, d, cwh), lambda i, j: (slot, 0, 3 * dm // cwh + hc(j))),
        pl.BlockSpec((None, 3, cw), lambda i, j: (slot, 0, mc(j))),
        pl.BlockSpec((None, nseg, SUBLANES, cw), lambda i, j: (slot, i // tpb, 0, mc(j))),
        pl.BlockSpec((None, nseg, n_mem, cwh), lambda i, j: (layer, i // tpb, 0, hc(j))),
        pl.BlockSpec((None, nseg, n_mem, cwh), lambda i, j: (layer, i // tpb, 0, hc(j))),
        pl.BlockSpec((None, cw, d), lambda i, j: (layer, mc(j), 0)),
        pl.BlockSpec((None, cwh, d), lambda i, j: (layer, dm // cwh + hc(j), 0)),
    ]
    out_specs = [
        pl.BlockSpec((tm, d), lambda i, j: (i, 0)),
        pl.BlockSpec((nseg, nc, SUBLANES, cw), lambda i, j: (i // tpb, 0, 0, 0)),
    ]
    scratch = [pltpu.VMEM((tm, d), BF16), pltpu.VMEM((nc, nseg, SUBLANES, cw), F32),
               pltpu.VMEM((tm, d), F32), pltpu.SemaphoreType.DMA(())]
    est = (3 * tm * d * 4 + tm * d * 2 + 2 * (3 * d * cw + d * cwh + (cw + cwh) * d) * 2
           + 4 * nseg * n_mem * cwh * 4 + 8 * rb * cw * 4)
    x_out, buf = pl.pallas_call(
        functools.partial(_mix_conv_kernel, seg=seg, rb=rb, tiles_per_batch=tpb, nc=nc,
                          head_dim=head_dim),
        grid=(m // tm, nc + nh),
        in_specs=in_specs,
        out_specs=out_specs,
        out_shape=[jax.ShapeDtypeStruct((m, d), F32),
                   jax.ShapeDtypeStruct((nb, nc, SUBLANES, cw), F32)],
        scratch_shapes=scratch,
        compiler_params=_compiler_params(est),
        name="mix_conv",
    )(x, g, w_in, w_in, w_in, w_in, conv_w, _pad_history(prev), mem_k, mem_v, w_out, w_out)
    return x_out, _unpack_history(buf, prev.shape[-2])


def _mix_gmlp_kernel(x_hbm, g_ref, wv_ref, wu_ref, wq_ref, lng_ref, lnb_ref, ws_ref, bias_ref, k_ref,
                     v_ref, wom_ref, woh_ref, o_ref, *rest, seg, rb, rblk, nc, head_dim, group_dim,
                     emit_v):
    if emit_v:
        vout_ref, xn_ref, vs_ref, x_sem = rest
    else:
        vout_ref = None
        xn_ref, vs_ref, x_sem = rest
    i = pl.program_id(0)
    j = pl.program_id(1)
    tm = xn_ref.shape[0]
    cw = wv_ref.shape[1]

    def x_copies(tile):
        rows = pl.ds(pl.multiple_of(tile * tm, tm), tm)
        return [pltpu.make_async_copy(x_hbm.at[rows, pl.ds(c * cw, cw)], vs_ref.at[c], x_sem.at[c])
                for c in range(nc)]

    @pl.when(jnp.logical_and(i == 0, j == 0))
    def _():
        for cp in x_copies(0):
            cp.start()

    @pl.when(j == 0)
    def _():
        for cp in x_copies(i):
            cp.wait()

    @pl.when(jnp.logical_and(j == 2 * nc, i + 1 < pl.num_programs(0)))
    def _():
        for cp in x_copies(i + 1):
            cp.start()

    def v_step(new_tile):
        for r0 in range(0, tm, rb):
            if new_tile:
                x = _cat([vs_ref[c, r0:r0 + rb, :] for c in range(nc)], 1)
                xn = _init_rows(x, g_ref, o_ref, xn_ref, r0, rb)
                vs_ref[0, r0:r0 + rb, :] = _gelu(_dot(xn, wv_ref[...]))
            else:
                vs_ref[j, r0:r0 + rb, :] = _gelu(_dot(xn_ref[r0:r0 + rb, :], wv_ref[...]))

    pl.when(j == 0)(functools.partial(v_step, True))
    pl.when(jnp.logical_and(j > 0, j < nc))(functools.partial(v_step, False))

    def layer_norm_rows(r0):
        width = nc * cw
        rows = slice(r0, r0 + rb)
        tot = vs_ref[0, rows, :].sum(axis=-1, keepdims=True)
        for c in range(1, nc):
            tot += vs_ref[c, rows, :].sum(axis=-1, keepdims=True)
        mean = tot / width
        sq = None
        for c in range(nc):
            dv = vs_ref[c, rows, :] - mean
            part = (dv * dv).sum(axis=-1, keepdims=True)
            sq = part if sq is None else sq + part
        rstd = lax.rsqrt(sq / width + EPS)
        for c in range(nc):
            cs = slice(c * cw, (c + 1) * cw)
            vn = (vs_ref[c, rows, :] - mean) * rstd * lng_ref[:, cs] + lnb_ref[:, cs]
            vs_ref[c, rows, :] = vn
            if emit_v:
                vout_ref[rows, cs] = vn

    def u_step(first_u):
        tri = (lax.broadcasted_iota(jnp.int32, (rblk, rblk), 0)
               >= lax.broadcasted_iota(jnp.int32, (rblk, rblk), 1))
        wmats = [jnp.where(tri, ws_ref[gi, :rblk, :rblk], 0.0).astype(BF16)
                 for gi in range(cw // group_dim)]

        def lead(r0):
            u = _dot(xn_ref[r0:r0 + rb, :], wu_ref[...])
            if first_u:
                layer_norm_rows(r0)
            vn = vs_ref[j - nc, r0:r0 + rb, :].astype(BF16)
            cols = []
            for gi, wmat in enumerate(wmats):
                blocks = [_dot(wmat, vn[r:r + rblk, gi * group_dim:(gi + 1) * group_dim])
                          for r in range(0, rb, rblk)]
                cols.append(_cat(blocks, 0))
            return u, _cat(cols, 1)

        def finish(r0, um):
            u, mixed = um
            bias = _cat([_cat([jnp.broadcast_to(bias_ref[gi, :rblk, :], (rblk, group_dim))]
                              * (rb // rblk), 0) for gi in range(len(wmats))], 1)
            y = _gelu(u) * (mixed + bias)
            o_ref[r0:r0 + rb, :] += _dot(y.astype(BF16), wom_ref[...])

        _pipelined(range(0, tm, rb), lead, finish)

    pl.when(j == nc)(functools.partial(u_step, True))
    pl.when(jnp.logical_and(j > nc, j < 2 * nc))(functools.partial(u_step, False))

    @pl.when(j >= 2 * nc)
    def _():
        _attention_step(xn_ref, wq_ref, k_ref, v_ref, woh_ref, o_ref, rb, seg, head_dim)


def _mix_gmlp(x, g, w_in, ln_g, ln_b, ws, bias, mem_k, mem_v, w_out, *, layer, slot, tm, rb, seg,
              batch_len, cw, cwh, head_dim, emit_v):
    m, d = x.shape
    dm = ln_g.shape[-1]
    nb, n_mem, dq = mem_k.shape[1:]
    groups, chunk = ws.shape[1:3]
    group_dim = dm // groups
    nseg = tm // seg
    tpb = batch_len // tm if nseg == 1 else 1
    rblk = min(chunk, seg)
    assert seg % rblk == 0 and rb % rblk == 0 and cw % group_dim == 0
    nc, nh = dm // cw, dq // cwh
    gpc = cw // group_dim
    uc = lambda j: jnp.clip(j - nc, 0, nc - 1)
    hc = lambda j: jnp.maximum(j - 2 * nc, 0)
    in_specs = [
        pl.BlockSpec(memory_space=pl.ANY),
        pl.BlockSpec((None, 1, d), lambda i, j: (layer, 0, 0)),
        pl.BlockSpec((None, d, cw), lambda i, j: (slot, 0, nc + jnp.minimum(j, nc - 1))),
        pl.BlockSpec((None, d, cw), lambda i, j: (slot, 0, uc(j))),
        pl.BlockSpec((None, d, cwh), lambda i, j: (slot, 0, 2 * dm // cwh + hc(j))),
        pl.BlockSpec((None, 1, dm), lambda i, j: (slot, 0, 0)),
        pl.BlockSpec((None, 1, dm), lambda i, j: (slot, 0, 0)),
        pl.BlockSpec((None, gpc, chunk, chunk), lambda i, j: (slot, uc(j), 0, 0)),
        pl.BlockSpec((None, gpc, chunk, 1), lambda i, j: (slot, uc(j), 0, 0)),
        pl.BlockSpec((None, nseg, n_mem, cwh), lambda i, j: (layer, i // tpb, 0, hc(j))),
        pl.BlockSpec((None, nseg, n_mem, cwh), lambda i, j: (layer, i // tpb, 0, hc(j))),
        pl.BlockSpec((None, cw, d), lambda i, j: (layer, uc(j), 0)),
        pl.BlockSpec((None, cwh, d), lambda i, j: (layer, dm // cwh + hc(j), 0)),
    ]
    out_specs = [pl.BlockSpec((tm, d), lambda i, j: (i, 0))]
    out_shape = [jax.ShapeDtypeStruct((m, d), F32)]
    if emit_v:
        out_specs.append(pl.BlockSpec((tm, dm), lambda i, j: (i, 0)))
        out_shape.append(jax.ShapeDtypeStruct((m, dm), F32))
    assert dm == d and nh >= 1
    est = (2 * tm * d * 4 + tm * d * 2 + tm * dm * 4 * (3 if emit_v else 1)
           + 2 * (2 * d * cw + d * cwh + (cw + cwh) * d) * 2
           + 4 * nseg * n_mem * cwh * 4 + 8 * rb * cw * 4)
    outs = pl.pallas_call(
        functools.partial(_mix_gmlp_kernel, seg=seg, rb=rb, rblk=rblk, nc=nc, head_dim=head_dim,
                          group_dim=group_dim, emit_v=emit_v),
        grid=(m // tm, 2 * nc + nh),
        in_specs=in_specs,
        out_specs=out_specs,
        out_shape=out_shape,
        scratch_shapes=[pltpu.VMEM((tm, d), BF16), pltpu.VMEM((nc, tm, cw), F32),
                        pltpu.SemaphoreType.DMA((nc,))],
        compiler_params=_compiler_params(est),
        name="mix_gmlp",
    )(x, g, w_in, w_in, w_in, ln_g, ln_b, ws, bias, mem_k, mem_v, w_out, w_out)
    return (outs[0], outs[1]) if emit_v else (outs[0], None)


def _ffn_kernel(x_ref, g_ref, wa_ref, wg_ref, cw_ref, cb_ref, prev_ref, wdn_ref, gf_ref, o_ref,
                buf_ref, xn_ref, carry_ref, u0_ref, u1_ref, *, seg, rb, tiles_per_batch, final_norm):
    i = pl.program_id(0)
    j = pl.program_id(1)
    tm = x_ref.shape[0]
    fc = wdn_ref.shape[0]
    u_refs = (u0_ref, u1_ref)

    first = True if tiles_per_batch == 1 else (i % tiles_per_batch) == 0

    def step(new_tile, normed_out=False):
        conv = _RowConv(cw_ref[...], prev_ref, carry_ref, buf_ref, j, first, seg)

        def lead(r0):
            xn = (_init_rows(x_ref[r0:r0 + rb, :], g_ref, o_ref, xn_ref, r0, rb) if new_tile
                  else xn_ref[r0:r0 + rb, :])
            u_ref = u_refs[(r0 // rb) % 2]
            u_ref[:, 0:fc] = _dot(xn, wa_ref[...])
            u_ref[:, fc:2 * fc] = _dot(xn, wg_ref[...])
            return u_ref

        def finish(r0, u_ref):
            z = conv.block(u_ref[...], r0) + cb_ref[...]
            h = _silu(z[:, fc:]) * z[:, :fc]
            acc = o_ref[r0:r0 + rb, :] + _dot(h.astype(BF16), wdn_ref[...])
            o_ref[r0:r0 + rb, :] = _rms(acc, gf_ref[...]) if normed_out else acc

        _pipelined(range(0, tm, rb), lead, finish)

    last = pl.num_programs(1) - 1
    pl.when(j == 0)(functools.partial(step, True))
    if final_norm:
        pl.when(jnp.logical_and(j > 0, j < last))(functools.partial(step, False))
        pl.when(jnp.logical_and(j > 0, j == last))(functools.partial(step, False, True))
    else:
        pl.when(j > 0)(functools.partial(step, False))


def _ffn(x, g, w_up, conv_w, conv_b, prev, w_down, g_final, *, layer, tm, rb, seg, batch_len, fc,
         final_norm):
    m, d = x.shape
    dff = w_down.shape[1]
    nj = dff // fc
    assert nj > 1
    nseg = tm // seg
    tpb = batch_len // tm if nseg == 1 else 1
    nb = prev.shape[1]
    in_specs = [
        pl.BlockSpec((tm, d), lambda i, j: (i, 0)),
        pl.BlockSpec((None, 1, d), lambda i, j: (layer, 0, 0)),
        pl.BlockSpec((None, d, fc), lambda i, j: (layer, 0, j)),
        pl.BlockSpec((None, d, fc), lambda i, j: (layer, 0, nj + j)),
        pl.BlockSpec((None, 3, 2 * fc), lambda i, j: (layer, 0, j)),
        pl.BlockSpec((None, 1, 2 * fc), lambda i, j: (layer, 0, j)),
        pl.BlockSpec((None, nseg, SUBLANES, 2 * fc), lambda i, j: (layer, i // tpb, 0, j)),
        pl.BlockSpec((None, fc, d), lambda i, j: (layer, j, 0)),
        pl.BlockSpec((1, d), lambda i, j: (0, 0)),
    ]
    out_specs = [
        pl.BlockSpec((tm, d), lambda i, j: (i, 0)),
        pl.BlockSpec((nseg, nj, SUBLANES, 2 * fc), lambda i, j: (i // tpb, 0, 0, 0)),
    ]
    scratch = [pltpu.VMEM((tm, d), BF16), pltpu.VMEM((nj, nseg, SUBLANES, 2 * fc), F32),
               pltpu.VMEM((rb, 2 * fc), F32), pltpu.VMEM((rb, 2 * fc), F32)]
    est = 4 * tm * d * 4 + tm * d * 2 + 2 * (3 * d * fc) * 2 + 10 * rb * 2 * fc * 4
    x_out, buf = pl.pallas_call(
        functools.partial(_ffn_kernel, seg=seg, rb=rb, tiles_per_batch=tpb, final_norm=final_norm),
        grid=(m // tm, nj),
        in_specs=in_specs,
        out_specs=out_specs,
        out_shape=[jax.ShapeDtypeStruct((m, d), F32),
                   jax.ShapeDtypeStruct((nb, nj, SUBLANES, 2 * fc), F32)],
        scratch_shapes=scratch,
        compiler_params=_compiler_params(est),
        name="conv_ffn",
    )(x, g, w_up, w_up, conv_w, conv_b, _pad_history(prev), w_down, g_final)
    return x_out, _unpack_history(buf, prev.shape[-2])


def _memkv_kernel(m_ref, g_ref, wk_ref, wv_ref, ko_ref, vo_ref):
    mn = _rms(m_ref[...], g_ref[0]).astype(BF16)
    ko_ref[0] = _dot(mn, wk_ref[0])
    vo_ref[0] = _dot(mn, wv_ref[0])


def _memkv(mem, g, wk, wv, *, tn):
    rows, d = mem.shape
    depth, _, dq = wk.shape
    shape = jax.ShapeDtypeStruct((depth, rows, dq), F32)
    return pl.pallas_call(
        _memkv_kernel,
        grid=(depth, dq // tn),
        in_specs=[
            pl.BlockSpec((rows, d), lambda l, n: (0, 0)),
            pl.BlockSpec((1, 1, d), lambda l, n: (l, 0, 0)),
            pl.BlockSpec((1, d, tn), lambda l, n: (l, 0, n)),
            pl.BlockSpec((1, d, tn), lambda l, n: (l, 0, n)),
        ],
        out_specs=[pl.BlockSpec((1, rows, tn), lambda l, n: (l, 0, n))] * 2,
        out_shape=[shape, shape],
        compiler_params=pltpu.CompilerParams(dimension_semantics=("arbitrary", "arbitrary")),
        name="memory_kv",
    )(mem, g, wk, wv)


def _group_cols(t, nj, fc):
    lead = t.shape[:-1]
    return t.reshape(lead + (2, nj, fc)).swapaxes(-3, -2).reshape(lead + (2 * nj * fc,))


def _ungroup_cols(t, nj, fc):
    lead = t.shape[:-1]
    return t.reshape(lead + (nj, 2, fc)).swapaxes(-3, -2).reshape(lead + (2 * nj * fc,))


def _tiling(t, tm, rb):
    return dict(tm=tm, rb=rb, seg=min(t, tm), batch_len=t)


def _trunk(x3, mem_k, mem_v, conv_a_prev, ffn_prev, p, *, tiles, emit_v):
    nb, t, d = x3.shape
    x = x3.reshape(nb * t, d)
    depth = p["norm_mix_g"].shape[0]
    conv_a_new, ffn_new, gmlp_v = [], [], []
    mixer = dict(head_dim=p["head_dim"])
    for i in range(depth):
        kind, jx = i % 2, i // 2
        if kind == 0:
            x, buf = _mix_conv(x, p["norm_mix_g"], p["w_in_a"], p["conv_a_w"], conv_a_prev, mem_k,
                               mem_v, p["w_out"], layer=i, slot=jx, cw=p["cw_conv"], cwh=p["cwh_conv"], **mixer,
                               **_tiling(t, *tiles["mix_conv"]))
            conv_a_new.append(buf)
        else:
            x, v_rows = _mix_gmlp(x, p["norm_mix_g"], p["w_in_b"], p["gmlp_norm_g"], p["gmlp_norm_b"],
                                  p["gmlp_ws"], p["gmlp_bias"], mem_k, mem_v, p["w_out"], layer=i,
                                  slot=jx, emit_v=emit_v, cw=p["cw_gmlp"], cwh=p["cwh_gmlp"], **mixer,
                                  **_tiling(t, *tiles["mix_gmlp"]))
            gmlp_v.append(v_rows)
        x, buf = _ffn(x, p["norm_ffn_g"], p["w_up"], p["ffn_conv_w"], p["ffn_conv_b"], ffn_prev,
                      p["w_down"], p["norm_final_g"], layer=i, fc=p["fc"],
                      final_norm=(i == depth - 1), **_tiling(t, *tiles["ffn"]))
        ffn_new.append(_ungroup_cols(buf, p["nj"], p["fc"]))
    y = x.reshape(nb, t, d)
    v_out = jnp.stack([v.reshape(nb, t, -1) for v in gmlp_v]) if emit_v else None
    return y, jnp.stack(conv_a_new), jnp.stack(ffn_new), v_out


def kernel(x_prompt, x_sample, mem_prompt, cache_conv_a, cache_ffn_conv, cache_mem_k, cache_mem_v,
           norm_mix_g, norm_mem_g, w_mem_k, w_mem_v, w_in_a, conv_a_w, w_in_b, gmlp_norm_g,
           gmlp_norm_b, gmlp_ws, gmlp_bias, w_out, norm_ffn_g, w_up, ffn_conv_w, ffn_conv_b, w_down,
           norm_final_g):
    b, s, d = x_prompt.shape
    nb_s, t_s, _ = x_sample.shape
    depth = norm_mix_g.shape[0]
    n_mem = mem_prompt.shape[1]
    heads, head_dim = cache_mem_k.shape[-2:]
    dq = heads * head_dim
    dff = w_down.shape[1]
    fc = 512
    assert dff % fc == 0
    nj = dff // fc

    p = dict(
        norm_mix_g=norm_mix_g[:, None, :], norm_ffn_g=norm_ffn_g[:, None, :],
        norm_final_g=norm_final_g[None],
        w_in_a=w_in_a.astype(BF16), w_in_b=w_in_b.astype(BF16), w_out=w_out.astype(BF16),
        conv_a_w=conv_a_w, gmlp_norm_g=gmlp_norm_g[:, None, :], gmlp_norm_b=gmlp_norm_b[:, None, :],
        gmlp_ws=gmlp_ws, gmlp_bias=gmlp_bias[..., None],
        w_up=w_up.astype(BF16), w_down=w_down.astype(BF16),
        ffn_conv_w=_group_cols(ffn_conv_w, nj, fc),
        ffn_conv_b=_group_cols(ffn_conv_b, nj, fc)[:, None, :],
        fc=fc, nj=nj, cw_conv=512, cw_gmlp=512, cwh_conv=512, cwh_gmlp=512, head_dim=head_dim,
    )

    mk, mv = _memkv(mem_prompt.reshape(b * n_mem, d), norm_mem_g[:, None, :],
                    w_mem_k.astype(BF16), w_mem_v.astype(BF16), tn=512)
    mk = mk.reshape(depth, b, n_mem, dq)
    mv = mv.reshape(depth, b, n_mem, dq)
    conv_a_zero = jnp.zeros((cache_conv_a.shape[0], b) + cache_conv_a.shape[2:], F32)
    ffn_zero = jnp.zeros((depth, b) + cache_ffn_conv.shape[2:], F32)
    y_prompt, conv_a_prompt, ffn_conv_prompt, _ = _trunk(
        x_prompt, mk, mv, conv_a_zero, ffn_zero, p, emit_v=False,
        tiles=dict(mix_conv=(1024, 1024), mix_gmlp=(1024, 512), ffn=(1024, 512)))

    y_sample, conv_a_sample, ffn_conv_sample, gmlp_v_sample = _trunk(
        x_sample, cache_mem_k.reshape(depth, nb_s, n_mem, dq),
        cache_mem_v.reshape(depth, nb_s, n_mem, dq), cache_conv_a,
        _group_cols(cache_ffn_conv, nj, fc), p, emit_v=True,
        tiles=dict.fromkeys(("mix_conv", "mix_gmlp", "ffn"), (nb_s * t_s, nb_s * t_s)))

    return (y_prompt, y_sample, conv_a_prompt, ffn_conv_prompt,
            mk.reshape(depth, b, n_mem, heads, head_dim), mv.reshape(depth, b, n_mem, heads, head_dim),
            conv_a_sample, ffn_conv_sample, gmlp_v_sample)
```

```python
import functools

import jax
import jax.numpy as jnp
from jax import lax
from jax.experimental import pallas as pl
from jax.experimental.pallas import tpu as pltpu

EPS = 1e-6
SUBLANES = 8
VMEM_REQUEST_CAP = 62 * 1024 * 1024

F32 = jnp.float32
BF16 = jnp.bfloat16


def _vmem_limit(est_bytes):
    return int(min(VMEM_REQUEST_CAP, max(32 * 1024 * 1024, est_bytes * 5 // 4)))


def _compiler_params(est_bytes):
    return pltpu.CompilerParams(dimension_semantics=("arbitrary", "arbitrary"),
                                vmem_limit_bytes=_vmem_limit(est_bytes))


def _dot(a, b):
    return jnp.dot(a, b, preferred_element_type=F32)


def _rms(x, g):
    return x * lax.rsqrt(jnp.mean(x * x, axis=-1, keepdims=True) + EPS) * g


def _gelu(x):
    return 0.5 * x * (1.0 + lax.erf(x * (2.0 ** -0.5)))


def _silu(x):
    return (0.5 * x) * (1.0 + jnp.tanh(0.5 * x))


def _cat(parts, axis):
    return parts[0] if len(parts) == 1 else jnp.concatenate(parts, axis=axis)


def _segment_spans(row0, nrows, seg):
    spans, pos = [], row0
    while pos < row0 + nrows:
        s, off = divmod(pos, seg)
        n = min(seg - off, row0 + nrows - pos)
        spans.append((s, off, n))
        pos += n
    return spans


def _conv3(u, h0, h1, w):
    s, c = u.shape
    r1 = pltpu.roll(u, 1, axis=0)
    r2 = pltpu.roll(u, 2, axis=0)
    row = lax.broadcasted_iota(jnp.int32, (SUBLANES, c), 0)
    top1 = jnp.where(row == 0, h1, r1[:SUBLANES])
    top2 = jnp.where(row == 0, h0, jnp.where(row == 1, h1, r2[:SUBLANES]))
    if s > SUBLANES:
        r1 = jnp.concatenate([top1, r1[SUBLANES:]], axis=0)
        r2 = jnp.concatenate([top2, r2[SUBLANES:]], axis=0)
    else:
        r1, r2 = top1, top2
    return w[0:1] * r2 + w[1:2] * r1 + w[2:3] * u


class _RowConv:
    def __init__(self, w, prev_ref, carry_ref, buf_ref, slot, first, seg):
        self.w, self.carry_ref, self.buf_ref, self.slot, self.seg = w, carry_ref, buf_ref, slot, seg
        self.tail = None

        def load_prev():
            for s in range(prev_ref.shape[0]):
                carry_ref[slot, s] = prev_ref[s]

        if first is True:
            load_prev()
        else:
            pl.when(first)(load_prev)

    def block(self, u, row0):
        outs, pos = [], 0
        for s, off, n in _segment_spans(row0, u.shape[0], self.seg):
            us = u[pos:pos + n]
            hist = self.carry_ref[self.slot, s] if off == 0 else self.tail
            outs.append(_conv3(us, hist[SUBLANES - 2:SUBLANES - 1], hist[SUBLANES - 1:], self.w))
            self.tail = us[n - SUBLANES:]
            if off + n == self.seg:
                self.carry_ref[self.slot, s] = self.tail
                self.buf_ref[s, self.slot] = self.tail
            pos += n
        return _cat(outs, 0)


def _pad_history(prev):
    pad = [(0, 0)] * prev.ndim
    pad[-2] = (SUBLANES - prev.shape[-2], 0)
    return jnp.pad(prev, pad)


def _unpack_history(buf, rows):
    nb, slots, _, c = buf.shape
    return buf[:, :, SUBLANES - rows:, :].swapaxes(1, 2).reshape(nb, rows, slots * c)


def _attention(q, k_ref, v_ref, row0, seg, head_dim):
    width = q.shape[1]
    scale = head_dim ** -0.5
    spans = _segment_spans(row0, q.shape[0], seg)
    head_cols = [slice(h * head_dim, (h + 1) * head_dim) for h in range(width // head_dim)]
    scores, pos = [], 0
    for s, _, n in spans:
        for cs in head_cols:
            qh = q[pos:pos + n, cs].astype(BF16)
            kh = k_ref[s, :, cs].astype(BF16)
            scores.append(lax.dot_general(qh, kh, (((1,), (1,)), ((), ())),
                                          preferred_element_type=F32) * scale)
        pos += n
    rows = []
    for si, (s, _, n) in enumerate(spans):
        heads = []
        for hi, cs in enumerate(head_cols):
            sc = scores[si * len(head_cols) + hi]
            m = jnp.max(sc, axis=-1, keepdims=True)
            p = jnp.exp(sc - m)
            l = jnp.sum(p, axis=-1, keepdims=True)
            heads.append(_dot(p.astype(BF16), v_ref[s, :, cs].astype(BF16)) / l)
        rows.append(_cat(heads, 1))
    return _cat(rows, 0)


def _pipelined(starts, lead, finish):
    cur = lead(starts[0])
    for k, r0 in enumerate(starts):
        nxt = lead(starts[k + 1]) if k + 1 < len(starts) else None
        finish(r0, cur)
        cur = nxt


def _init_rows(x, g_ref, o_ref, xn_ref, r0, rb):
    xn = _rms(x, g_ref[...]).astype(BF16)
    xn_ref[r0:r0 + rb, :] = xn
    o_ref[r0:r0 + rb, :] = x
    return xn


def _stream_x_tile(x_hbm, xbuf_ref, sem, i, j):
    tm = xbuf_ref.shape[0]

    def copy(tile):
        return pltpu.make_async_copy(x_hbm.at[pl.ds(pl.multiple_of(tile * tm, tm), tm), :], xbuf_ref, sem)

    @pl.when(jnp.logical_and(i == 0, j == 0))
    def _():
        copy(0).start()

    @pl.when(j == 0)
    def _():
        copy(i).wait()

    @pl.when(jnp.logical_and(j == 1, i + 1 < pl.num_programs(0)))
    def _():
        copy(i + 1).start(priority=1)


def _attention_step(xn_ref, wq_ref, k_ref, v_ref, woh_ref, o_ref, rb, seg, head_dim):
    def lead(r0):
        return _dot(xn_ref[r0:r0 + rb, :], wq_ref[...])

    def finish(r0, q):
        y = _attention(q, k_ref, v_ref, r0, seg, head_dim)
        o_ref[r0:r0 + rb, :] += _dot(y.astype(BF16), woh_ref[...])

    _pipelined(range(0, xn_ref.shape[0], rb), lead, finish)


def _mix_conv_kernel(x_hbm, g_ref, wb_ref, wc_ref, wh_ref, wq_ref, cw_ref, prev_ref, k_ref, v_ref,
                     wom_ref, woh_ref, o_ref, buf_ref, xn_ref, carry_ref, x_ref, x_sem, *, seg, rb,
                     tiles_per_batch, nc, head_dim):
    i = pl.program_id(0)
    j = pl.program_id(1)
    tm = x_ref.shape[0]
    _stream_x_tile(x_hbm, x_ref, x_sem, i, j)

    def mixer_step(new_tile):
        first = True if tiles_per_batch == 1 else (i % tiles_per_batch) == 0
        conv = _RowConv(cw_ref[...], prev_ref, carry_ref, buf_ref, j, first, seg)

        def lead(r0):
            xn = (_init_rows(x_ref[r0:r0 + rb, :], g_ref, o_ref, xn_ref, r0, rb) if new_tile
                  else xn_ref[r0:r0 + rb, :])
            return _dot(xn, wc_ref[...]), _dot(xn, wh_ref[...]), _dot(xn, wb_ref[...])

        def finish(r0, chb):
            c, h, b = chb
            y = b * conv.block(c * h, r0)
            o_ref[r0:r0 + rb, :] += _dot(y.astype(BF16), wom_ref[...])

        _pipelined(range(0, tm, rb), lead, finish)

    pl.when(j == 0)(functools.partial(mixer_step, True))
    pl.when(jnp.logical_and(j > 0, j < nc))(functools.partial(mixer_step, False))

    @pl.when(j >= nc)
    def _():
        _attention_step(xn_ref, wq_ref, k_ref, v_ref, woh_ref, o_ref, rb, seg, head_dim)


def _mix_conv(x, g, w_in, conv_w, prev, mem_k, mem_v, w_out, *, layer, slot, tm, rb, seg, batch_len,
              cw, cwh, head_dim):
    m, d = x.shape
    dm = conv_w.shape[-1]
    nb, n_mem, dq = mem_k.shape[1:]
    nseg = tm // seg
    tpb = batch_len // tm if nseg == 1 else 1
    nc, nh = dm // cw, dq // cwh
    mc = lambda j: jnp.minimum(j, nc - 1)
    hc = lambda j: jnp.maximum(j - nc, 0)
    in_specs = [
        pl.BlockSpec(memory_space=pl.ANY),
        pl.BlockSpec((None, 1, d), lambda i, j: (layer, 0, 0)),
        pl.BlockSpec((None, d, cw), lambda i, j: (slot, 0, mc(j))),
        pl.BlockSpec((None, d, cw), lambda i, j: (slot, 0, nc + mc(j))),
        pl.BlockSpec((None, d, cw), lambda i, j: (slot, 0, 2 * nc + mc(j))),
        pl.BlockSpec((None, d, cwh), lambda i, j: (slot, 0, 3 * dm // cwh + hc(j))),
        pl.BlockSpec((None, 3, cw), lambda i, j: (slot, 0, mc(j))),
        pl.BlockSpec((None, nseg, SUBLANES, cw), lambda i, j: (slot, i // tpb, 0, mc(j))),
        pl.BlockSpec((None, nseg, n_mem, cwh), lambda i, j: (layer, i // tpb, 0, hc(j))),
        pl.BlockSpec((None, nseg, n_mem, cwh), lambda i, j: (layer, i // tpb, 0, hc(j))),
        pl.BlockSpec((None, cw, d), lambda i, j: (layer, mc(j), 0)),
        pl.BlockSpec((None, cwh, d), lambda i, j: (layer, dm // cwh + hc(j), 0)),
    ]
    out_specs = [
        pl.BlockSpec((tm, d), lambda i, j: (i, 0)),
        pl.BlockSpec((nseg, nc, SUBLANES, cw), lambda i, j: (i // tpb, 0, 0, 0)),
    ]
    scratch = [pltpu.VMEM((tm, d), BF16), pltpu.VMEM((nc, nseg, SUBLANES, cw), F32),
               pltpu.VMEM((tm, d), F32), pltpu.SemaphoreType.DMA(())]
    est = (3 * tm * d * 4 + tm * d * 2 + 2 * (3 * d * cw + d * cwh + (cw + cwh) * d) * 2
           + 4 * nseg * n_mem * cwh * 4 + 8 * rb * cw * 4)
    x_out, buf = pl.pallas_call(
        functools.partial(_mix_conv_kernel, seg=seg, rb=rb, tiles_per_batch=tpb, nc=nc,
                          head_dim=head_dim),
        grid=(m // tm, nc + nh),
        in_specs=in_specs,
        out_specs=out_specs,
        out_shape=[jax.ShapeDtypeStruct((m, d), F32),
                   jax.ShapeDtypeStruct((nb, nc, SUBLANES, cw), F32)],
        scratch_shapes=scratch,
        compiler_params=_compiler_params(est),
        name="mix_conv",
    )(x, g, w_in, w_in, w_in, w_in, conv_w, _pad_history(prev), mem_k, mem_v, w_out, w_out)
    return x_out, _unpack_history(buf, prev.shape[-2])


def _mix_gmlp_kernel(x_hbm, g_ref, wv_ref, wu_ref, wq_ref, lng_ref, lnb_ref, ws_ref, bias_ref, k_ref,
                     v_ref, wom_ref, woh_ref, o_ref, *rest, seg, rb, rblk, nc, head_dim, group_dim,
                     emit_v):
    if emit_v:
        vout_ref, xn_ref, vs_ref, x_sem = rest
    else:
        vout_ref = None
        xn_ref, vs_ref, x_sem = rest
    i = pl.program_id(0)
    j = pl.program_id(1)
    tm = xn_ref.shape[0]
    cw = wv_ref.shape[1]

    def x_copies(tile):
        rows = pl.ds(pl.multiple_of(tile * tm, tm), tm)
        return [pltpu.make_async_copy(x_hbm.at[rows, pl.ds(c * cw, cw)], vs_ref.at[c], x_sem.at[c])
                for c in range(nc)]

    @pl.when(jnp.logical_and(i == 0, j == 0))
    def _():
        for cp in x_copies(0):
            cp.start()

    @pl.when(j == 0)
    def _():
        for cp in x_copies(i):
            cp.wait()

    @pl.when(jnp.logical_and(j == 2 * nc, i + 1 < pl.num_programs(0)))
    def _():
        for cp in x_copies(i + 1):
            cp.start(priority=1)

    def v_step(new_tile):
        for r0 in range(0, tm, rb):
            if new_tile:
                x = _cat([vs_ref[c, r0:r0 + rb, :] for c in range(nc)], 1)
                xn = _init_rows(x, g_ref, o_ref, xn_ref, r0, rb)
                vs_ref[0, r0:r0 + rb, :] = _gelu(_dot(xn, wv_ref[...]))
            else:
                vs_ref[j, r0:r0 + rb, :] = _gelu(_dot(xn_ref[r0:r0 + rb, :], wv_ref[...]))

    pl.when(j == 0)(functools.partial(v_step, True))
    pl.when(jnp.logical_and(j > 0, j < nc))(functools.partial(v_step, False))

    def layer_norm_rows(r0):
        width = nc * cw
        rows = slice(r0, r0 + rb)
        tot = vs_ref[0, rows, :].sum(axis=-1, keepdims=True)
        for c in range(1, nc):
            tot += vs_ref[c, rows, :].sum(axis=-1, keepdims=True)
        mean = tot / width
        sq = None
        for c in range(nc):
            dv = vs_ref[c, rows, :] - mean
            part = (dv * dv).sum(axis=-1, keepdims=True)
            sq = part if sq is None else sq + part
        rstd = lax.rsqrt(sq / width + EPS)
        for c in range(nc):
            cs = slice(c * cw, (c + 1) * cw)
            vn = (vs_ref[c, rows, :] - mean) * rstd * lng_ref[:, cs] + lnb_ref[:, cs]
            vs_ref[c, rows, :] = vn
            if emit_v:
                vout_ref[rows, cs] = vn

    def u_step(first_u):
        tri = (lax.broadcasted_iota(jnp.int32, (rblk, rblk), 0)
               >= lax.broadcasted_iota(jnp.int32, (rblk, rblk), 1))
        wmats = [jnp.where(tri, ws_ref[gi, :rblk, :rblk], 0.0).astype(BF16)
                 for gi in range(cw // group_dim)]

        def lead(r0):
            u = _dot(xn_ref[r0:r0 + rb, :], wu_ref[...])
            if first_u:
                layer_norm_rows(r0)
            vn = vs_ref[j - nc, r0:r0 + rb, :].astype(BF16)
            cols = []
            for gi, wmat in enumerate(wmats):
                blocks = [_dot(wmat, vn[r:r + rblk, gi * group_dim:(gi + 1) * group_dim])
                          for r in range(0, rb, rblk)]
                cols.append(_cat(blocks, 0))
            return u, _cat(cols, 1)

        def finish(r0, um):
            u, mixed = um
            bias = _cat([_cat([jnp.broadcast_to(bias_ref[gi, :rblk, :], (rblk, group_dim))]
                              * (rb // rblk), 0) for gi in range(len(wmats))], 1)
            y = _gelu(u) * (mixed + bias)
            o_ref[r0:r0 + rb, :] += _dot(y.astype(BF16), wom_ref[...])

        _pipelined(range(0, tm, rb), lead, finish)

    pl.when(j == nc)(functools.partial(u_step, True))
    pl.when(jnp.logical_and(j > nc, j < 2 * nc))(functools.partial(u_step, False))

    @pl.when(j >= 2 * nc)
    def _():
        _attention_step(xn_ref, wq_ref, k_ref, v_ref, woh_ref, o_ref, rb, seg, head_dim)


def _mix_gmlp(x, g, w_in, ln_g, ln_b, ws, bias, mem_k, mem_v, w_out, *, layer, slot, tm, rb, seg,
              batch_len, cw, cwh, head_dim, emit_v):
    m, d = x.shape
    dm = ln_g.shape[-1]
    nb, n_mem, dq = mem_k.shape[1:]
    groups, chunk = ws.shape[1:3]
    group_dim = dm // groups
    nseg = tm // seg
    tpb = batch_len // tm if nseg == 1 else 1
    rblk = min(chunk, seg)
    assert seg % rblk == 0 and rb % rblk == 0 and cw % group_dim == 0
    nc, nh = dm // cw, dq // cwh
    gpc = cw // group_dim
    uc = lambda j: jnp.clip(j - nc, 0, nc - 1)
    hc = lambda j: jnp.maximum(j - 2 * nc, 0)
    in_specs = [
        pl.BlockSpec(memory_space=pl.ANY),
        pl.BlockSpec((None, 1, d), lambda i, j: (layer, 0, 0)),
        pl.BlockSpec((None, d, cw), lambda i, j: (slot, 0, nc + jnp.minimum(j, nc - 1))),
        pl.BlockSpec((None, d, cw), lambda i, j: (slot, 0, uc(j))),
        pl.BlockSpec((None, d, cwh), lambda i, j: (slot, 0, 2 * dm // cwh + hc(j))),
        pl.BlockSpec((None, 1, dm), lambda i, j: (slot, 0, 0)),
        pl.BlockSpec((None, 1, dm), lambda i, j: (slot, 0, 0)),
        pl.BlockSpec((None, gpc, chunk, chunk), lambda i, j: (slot, uc(j), 0, 0)),
        pl.BlockSpec((None, gpc, chunk, 1), lambda i, j: (slot, uc(j), 0, 0)),
        pl.BlockSpec((None, nseg, n_mem, cwh), lambda i, j: (layer, i // tpb, 0, hc(j))),
        pl.BlockSpec((None, nseg, n_mem, cwh), lambda i, j: (layer, i // tpb, 0, hc(j))),
        pl.BlockSpec((None, cw, d), lambda i, j: (layer, uc(j), 0)),
        pl.BlockSpec((None, cwh, d), lambda i, j: (layer, dm // cwh + hc(j), 0)),
    ]
    out_specs = [pl.BlockSpec((tm, d), lambda i, j: (i, 0))]
    out_shape = [jax.ShapeDtypeStruct((m, d), F32)]
    if emit_v:
        out_specs.append(pl.BlockSpec((tm, dm), lambda i, j: (i, 0)))
        out_shape.append(jax.ShapeDtypeStruct((m, dm), F32))
    assert dm == d and nh >= 1
    est = (2 * tm * d * 4 + tm * d * 2 + tm * dm * 4 * (3 if emit_v else 1)
           + 2 * (2 * d * cw + d * cwh + (cw + cwh) * d) * 2
           + 4 * nseg * n_mem * cwh * 4 + 8 * rb * cw * 4)
    outs = pl.pallas_call(
        functools.partial(_mix_gmlp_kernel, seg=seg, rb=rb, rblk=rblk, nc=nc, head_dim=head_dim,
                          group_dim=group_dim, emit_v=emit_v),
        grid=(m // tm, 2 * nc + nh),
        in_specs=in_specs,
        out_specs=out_specs,
        out_shape=out_shape,
        scratch_shapes=[pltpu.VMEM((tm, d), BF16), pltpu.VMEM((nc, tm, cw), F32),
                        pltpu.SemaphoreType.DMA((nc,))],
        compiler_params=_compiler_params(est),
        name="mix_gmlp",
    )(x, g, w_in, w_in, w_in, ln_g, ln_b, ws, bias, mem_k, mem_v, w_out, w_out)
    return (outs[0], outs[1]) if emit_v else (outs[0], None)


def _ffn_kernel(x_ref, g_ref, wa_ref, wg_ref, cw_ref, cb_ref, prev_ref, wdn_ref, gf_ref, o_ref,
                buf_ref, xn_ref, carry_ref, u0_ref, u1_ref, *, seg, rb, tiles_per_batch, final_norm):
    i = pl.program_id(0)
    j = pl.program_id(1)
    tm = x_ref.shape[0]
    fc = wdn_ref.shape[0]
    u_refs = (u0_ref, u1_ref)

    first = True if tiles_per_batch == 1 else (i % tiles_per_batch) == 0

    def step(new_tile, normed_out=False):
        conv = _RowConv(cw_ref[...], prev_ref, carry_ref, buf_ref, j, first, seg)

        def lead(r0):
            xn = (_init_rows(x_ref[r0:r0 + rb, :], g_ref, o_ref, xn_ref, r0, rb) if new_tile
                  else xn_ref[r0:r0 + rb, :])
            u_ref = u_refs[(r0 // rb) % 2]
            u_ref[:, 0:fc] = _dot(xn, wa_ref[...])
            u_ref[:, fc:2 * fc] = _dot(xn, wg_ref[...])
            return u_ref

        def finish(r0, u_ref):
            z = conv.block(u_ref[...], r0) + cb_ref[...]
            h = _silu(z[:, fc:]) * z[:, :fc]
            acc = o_ref[r0:r0 + rb, :] + _dot(h.astype(BF16), wdn_ref[...])
            o_ref[r0:r0 + rb, :] = _rms(acc, gf_ref[...]) if normed_out else acc

        _pipelined(range(0, tm, rb), lead, finish)

    last = pl.num_programs(1) - 1
    pl.when(j == 0)(functools.partial(step, True))
    if final_norm:
        pl.when(jnp.logical_and(j > 0, j < last))(functools.partial(step, False))
        pl.when(jnp.logical_and(j > 0, j == last))(functools.partial(step, False, True))
    else:
        pl.when(j > 0)(functools.partial(step, False))


def _ffn(x, g, w_up, conv_w, conv_b, prev, w_down, g_final, *, layer, tm, rb, seg, batch_len, fc,
         final_norm):
    m, d = x.shape
    dff = w_down.shape[1]
    nj = dff // fc
    assert nj > 1
    nseg = tm // seg
    tpb = batch_len // tm if nseg == 1 else 1
    nb = prev.shape[1]
    in_specs = [
        pl.BlockSpec((tm, d), lambda i, j: (i, 0)),
        pl.BlockSpec((None, 1, d), lambda i, j: (layer, 0, 0)),
        pl.BlockSpec((None, d, fc), lambda i, j: (layer, 0, j)),
        pl.BlockSpec((None, d, fc), lambda i, j: (layer, 0, nj + j)),
        pl.BlockSpec((None, 3, 2 * fc), lambda i, j: (layer, 0, j)),
        pl.BlockSpec((None, 1, 2 * fc), lambda i, j: (layer, 0, j)),
        pl.BlockSpec((None, nseg, SUBLANES, 2 * fc), lambda i, j: (layer, i // tpb, 0, j)),
        pl.BlockSpec((None, fc, d), lambda i, j: (layer, j, 0)),
        pl.BlockSpec((1, d), lambda i, j: (0, 0)),
    ]
    out_specs = [
        pl.BlockSpec((tm, d), lambda i, j: (i, 0)),
        pl.BlockSpec((nseg, nj, SUBLANES, 2 * fc), lambda i, j: (i // tpb, 0, 0, 0)),
    ]
    scratch = [pltpu.VMEM((tm, d), BF16), pltpu.VMEM((nj, nseg, SUBLANES, 2 * fc), F32),
               pltpu.VMEM((rb, 2 * fc), F32), pltpu.VMEM((rb, 2 * fc), F32)]
    est = 4 * tm * d * 4 + tm * d * 2 + 2 * (3 * d * fc) * 2 + 10 * rb * 2 * fc * 4
    x_out, buf = pl.pallas_call(
        functools.partial(_ffn_kernel, seg=seg, rb=rb, tiles_per_batch=tpb, final_norm=final_norm),
        grid=(m // tm, nj),
        in_specs=in_specs,
        out_specs=out_specs,
        out_shape=[jax.ShapeDtypeStruct((m, d), F32),
                   jax.ShapeDtypeStruct((nb, nj, SUBLANES, 2 * fc), F32)],
        scratch_shapes=scratch,
        compiler_params=_compiler_params(est),
        name="conv_ffn",
    )(x, g, w_up, w_up, conv_w, conv_b, _pad_history(prev), w_down, g_final)
    return x_out, _unpack_history(buf, prev.shape[-2])


def _memkv_kernel(m_ref, g_ref, wk_ref, wv_ref, ko_ref, vo_ref):
    mn = _rms(m_ref[...], g_ref[0]).astype(BF16)
    ko_ref[0] = _dot(mn, wk_ref[0])
    vo_ref[0] = _dot(mn, wv_ref[0])


def _memkv(mem, g, wk, wv, *, tn):
    rows, d = mem.shape
    depth, _, dq = wk.shape
    shape = jax.ShapeDtypeStruct((depth, rows, dq), F32)
    return pl.pallas_call(
        _memkv_kernel,
        grid=(depth, dq // tn),
        in_specs=[
            pl.BlockSpec((rows, d), lambda l, n: (0, 0)),
            pl.BlockSpec((1, 1, d), lambda l, n: (l, 0, 0)),
            pl.BlockSpec((1, d, tn), lambda l, n: (l, 0, n)),
            pl.BlockSpec((1, d, tn), lambda l, n: (l, 0, n)),
        ],
        out_specs=[pl.BlockSpec((1, rows, tn), lambda l, n: (l, 0, n))] * 2,
        out_shape=[shape, shape],
        compiler_params=pltpu.CompilerParams(dimension_semantics=("arbitrary", "arbitrary")),
        name="memory_kv",
    )(mem, g, wk, wv)


def _group_cols(t, nj, fc):
    lead = t.shape[:-1]
    return t.reshape(lead + (2, nj, fc)).swapaxes(-3, -2).reshape(lead + (2 * nj * fc,))


def _ungroup_cols(t, nj, fc):
    lead = t.shape[:-1]
    return t.reshape(lead + (nj, 2, fc)).swapaxes(-3, -2).reshape(lead + (2 * nj * fc,))


def _tiling(t, tm, rb):
    return dict(tm=tm, rb=rb, seg=min(t, tm), batch_len=t)


def _trunk(x3, mem_k, mem_v, conv_a_prev, ffn_prev, p, *, tiles, emit_v):
    nb, t, d = x3.shape
    x = x3.reshape(nb * t, d)
    depth = p["norm_mix_g"].shape[0]
    conv_a_new, ffn_new, gmlp_v = [], [], []
    mixer = dict(head_dim=p["head_dim"])
    for i in range(depth):
        kind, jx = i % 2, i // 2
        if kind == 0:
            x, buf = _mix_conv(x, p["norm_mix_g"], p["w_in_a"], p["conv_a_w"], conv_a_prev, mem_k,
                               mem_v, p["w_out"], layer=i, slot=jx, cw=p["cw_conv"], cwh=p["cwh_conv"], **mixer,
                               **_tiling(t, *tiles["mix_conv"]))
            conv_a_new.append(buf)
        else:
            x, v_rows = _mix_gmlp(x, p["norm_mix_g"], p["w_in_b"], p["gmlp_norm_g"], p["gmlp_norm_b"],
                                  p["gmlp_ws"], p["gmlp_bias"], mem_k, mem_v, p["w_out"], layer=i,
                                  slot=jx, emit_v=emit_v, cw=p["cw_gmlp"], cwh=p["cwh_gmlp"], **mixer,
                                  **_tiling(t, *tiles["mix_gmlp"]))
            gmlp_v.append(v_rows)
        x, buf = _ffn(x, p["norm_ffn_g"], p["w_up"], p["ffn_conv_w"], p["ffn_conv_b"], ffn_prev,
                      p["w_down"], p["norm_final_g"], layer=i, fc=p["fc"],
                      final_norm=(i == depth - 1), **_tiling(t, *tiles["ffn"]))
        ffn_new.append(_ungroup_cols(buf, p["nj"], p["fc"]))
    y = x.reshape(nb, t, d)
    v_out = jnp.stack([v.reshape(nb, t, -1) for v in gmlp_v]) if emit_v else None
    return y, jnp.stack(conv_a_new), jnp.stack(ffn_new), v_out


def kernel(x_prompt, x_sample, mem_prompt, cache_conv_a, cache_ffn_conv, cache_mem_k, cache_mem_v,
           norm_mix_g, norm_mem_g, w_mem_k, w_mem_v, w_in_a, conv_a_w, w_in_b, gmlp_norm_g,
           gmlp_norm_b, gmlp_ws, gmlp_bias, w_out, norm_ffn_g, w_up, ffn_conv_w, ffn_conv_b, w_down,
           norm_final_g):
    b, s, d = x_prompt.shape
    nb_s, t_s, _ = x_sample.shape
    depth = norm_mix_g.shape[0]
    n_mem = mem_prompt.shape[1]
    heads, head_dim = cache_mem_k.shape[-2:]
    dq = heads * head_dim
    dff = w_down.shape[1]
    fc = 512
    assert dff % fc == 0
    nj = dff // fc

    p = dict(
        norm_mix_g=norm_mix_g[:, None, :], norm_ffn_g=norm_ffn_g[:, None, :],
        norm_final_g=norm_final_g[None],
        w_in_a=w_in_a.astype(BF16), w_in_b=w_in_b.astype(BF16), w_out=w_out.astype(BF16),
        conv_a_w=conv_a_w, gmlp_norm_g=gmlp_norm_g[:, None, :], gmlp_norm_b=gmlp_norm_b[:, None, :],
        gmlp_ws=gmlp_ws, gmlp_bias=gmlp_bias[..., None],
        w_up=w_up.astype(BF16), w_down=w_down.astype(BF16),
        ffn_conv_w=_group_cols(ffn_conv_w, nj, fc),
        ffn_conv_b=_group_cols(ffn_conv_b, nj, fc)[:, None, :],
        fc=fc, nj=nj, cw_conv=512, cw_gmlp=512, cwh_conv=512, cwh_gmlp=512, head_dim=head_dim,
    )

    mk, mv = _memkv(mem_prompt.reshape(b * n_mem, d), norm_mem_g[:, None, :],
                    w_mem_k.astype(BF16), w_mem_v.astype(BF16), tn=512)
    mk = mk.reshape(depth, b, n_mem, dq)
    mv = mv.reshape(depth, b, n_mem, dq)
    conv_a_zero = jnp.zeros((cache_conv_a.shape[0], b) + cache_conv_a.shape[2:], F32)
    ffn_zero = jnp.zeros((depth, b) + cache_ffn_conv.shape[2:], F32)
    y_prompt, conv_a_prompt, ffn_conv_prompt, _ = _trunk(
        x_prompt, mk, mv, conv_a_zero, ffn_zero, p, emit_v=False,
        tiles=dict(mix_conv=(1024, 1024), mix_gmlp=(1024, 512), ffn=(1024, 512)))

    y_sample, conv_a_sample, ffn_conv_sample, gmlp_v_sample = _trunk(
        x_sample, cache_mem_k.reshape(depth, nb_s, n_mem, dq),
        cache_mem_v.reshape(depth, nb_s, n_mem, dq), cache_conv_a,
        _group_cols(cache_ffn_conv, nj, fc), p, emit_v=True,
        tiles=dict.fromkeys(("mix_conv", "mix_gmlp", "ffn"), (nb_s * t_s, nb_s * t_s)))

    return (y_prompt, y_sample, conv_a_prompt, ffn_conv_prompt,
            mk.reshape(depth, b, n_mem, heads, head_dim), mv.reshape(depth, b, n_mem, heads, head_dim),
            conv_a_sample, ffn_conv_sample, gmlp_v_sample)
```
